```python
import math
import jax
import jax.numpy as jnp
from jax import lax
import numpy as np

D_MODEL = 1024
BATCH = 16
SEQ = 4096
DEPTH = 2
DEC_BATCH = 16
DEC_SEQ = 32
PAST_LEN = 4096

CHUNK = 64
Q_BLOCK = 128
N_EVEN = (DEPTH + 1) // 2
N_ODD = DEPTH // 2
A_WIDTH = D_MODEL // 2
A_QK_DIM = 64
A_V_DIM = 2 * A_QK_DIM
A_HEADS = A_WIDTH // A_V_DIM
B_WIDTH = D_MODEL // 2
B_DK = 128
B_HEADS = B_WIDTH // B_DK
B_DV = B_WIDTH // B_HEADS
C_WIDTH = D_MODEL
C_DK = 128
C_DV = 128
C_HEADS = C_WIDTH // C_DK
CONV_W = 4
D_FF = 2816
EVEN_PROJ = 3 * A_WIDTH + 4 * B_WIDTH
ODD_PROJ = 4 * C_WIDTH + 2 * C_HEADS
NORM_EPS = 1e-6
DIFF_NORM_EPS = 1e-5

kernel_name = 'hybrid_streaming_encoder_step'


def rms_norm(x, gain, eps=NORM_EPS):
    x32 = x.astype(jnp.float32)
    y = x32 * lax.rsqrt(jnp.mean(x32 * x32, axis=-1, keepdims=True) + eps)
    return (y * gain.astype(jnp.float32)).astype(x.dtype)


def l2_normalize(x, eps=1e-6):
    x32 = x.astype(jnp.float32)
    return x32 * lax.rsqrt(jnp.sum(x32 * x32, axis=-1, keepdims=True) + eps)


def swiglu(h, w_in, w_out):
    gate, up = jnp.split(h @ w_in, 2, axis=-1)
    return (jax.nn.silu(gate) * up) @ w_out


def alibi_slopes():
    return 2.0 ** (-8.0 * jnp.arange(1, A_HEADS + 1, dtype=jnp.float32) / A_HEADS)


def to_chunks(x, size):
    b, t = x.shape[:2]
    x = x.reshape(b, t // size, size, *x.shape[2:])
    return jnp.moveaxis(jnp.moveaxis(x, 1, 0), 2, 3)


def from_chunks(x):
    x = jnp.moveaxis(jnp.moveaxis(x, 3, 2), 0, 1)
    b, nc, size = x.shape[:3]
    return x.reshape(b, nc * size, *x.shape[3:])


def diff_attention_block(q, k, v, q_pos, k_pos, lam, lam_init, gain):
    s = jnp.einsum('bqhmd,bkhmd->bhmqk', q, k).astype(jnp.float32) * (A_QK_DIM ** -0.5)
    dist = jnp.abs(q_pos[:, None] - k_pos[None, :]).astype(jnp.float32)
    visible = (k_pos[None, :] // CHUNK) <= (q_pos[:, None] // CHUNK)
    bias = jnp.where(visible, -alibi_slopes()[:, None, None] * dist, -jnp.inf)
    p = jax.nn.softmax(s + bias[None, :, None], axis=-1)
    p = p[:, :, 0] - lam * p[:, :, 1]
    o = jnp.einsum('bhqk,bkhe->bqhe', p.astype(v.dtype), v)
    return rms_norm(o, gain, DIFF_NORM_EPS) * (1.0 - lam_init)


def diff_attention_prompt(q, k, v, lam, lam_init, gain):
    b, t = q.shape[:2]
    nb = t // Q_BLOCK
    k_pos = jnp.arange(t, dtype=jnp.int32)
    q_blocks = jnp.moveaxis(q.reshape(b, nb, Q_BLOCK, A_HEADS, 2, A_QK_DIM), 1, 0)
    starts = jnp.arange(nb, dtype=jnp.int32) * Q_BLOCK

    def one_block(args):
        qb, st = args
        q_pos = st + jnp.arange(Q_BLOCK, dtype=jnp.int32)
        return diff_attention_block(qb, k, v, q_pos, k_pos, lam, lam_init, gain)

    o = lax.map(one_block, (q_blocks, starts))
    return jnp.moveaxis(o, 0, 1).reshape(b, t, A_HEADS, A_V_DIM)


def hgrn2_recurrence(q, k, v, log_f, s0):
    size = min(CHUNK, q.shape[1])
    incl = jnp.tril(jnp.ones((size, size), dtype=bool))

    def step(s, inp):
        qc, kc, vc, lfc = inp
        cb = jnp.cumsum(lfc, axis=2)
        rel = jnp.exp(jnp.where(incl[:, :, None], cb[:, :, :, None, :] - cb[:, :, None, :, :], -jnp.inf))
        scores = jnp.einsum('bhtd,bhsd,bhtsd->bhts', qc, kc, rel)
        o = scores @ vc + jnp.einsum('bhtd,bhde->bhte', qc * jnp.exp(cb), s)
        c_last = cb[:, :, -1:, :]
        s = jnp.exp(c_last[:, :, 0, :, None]) * s + jnp.einsum('bhsd,bhse->bhde', kc * jnp.exp(c_last - cb), vc)
        return s, o

    xs = tuple(to_chunks(a.astype(jnp.float32), size) for a in (q, k, v, log_f))
    s, o = lax.scan(step, s0.astype(jnp.float32), xs)
    return from_chunks(o), s


def gated_delta_recurrence(q, k, v, beta, g, s0):
    size = min(CHUNK, q.shape[1])
    strict = jnp.tril(jnp.ones((size, size), dtype=bool), -1)
    incl = jnp.tril(jnp.ones((size, size), dtype=bool))
    eye = jnp.eye(size, dtype=jnp.float32)

    def step(s, inp):
        qc, kc, vc, bc, gc = inp
        gam = jnp.cumsum(gc, axis=-1)
        diff = gam[..., :, None] - gam[..., None, :]
        d_strict = jnp.exp(jnp.where(strict, diff, -jnp.inf))
        d_incl = jnp.exp(jnp.where(incl, diff, -jnp.inf))
        kt = jnp.swapaxes(kc, -1, -2)
        m = eye + bc[..., :, None] * (kc @ kt) * d_strict
        w = lax.linalg.triangular_solve(m, kc * (bc * jnp.exp(gam))[..., None], left_side=True, lower=True, unit_diagonal=True)
        u = lax.linalg.triangular_solve(m, vc * bc[..., None], left_side=True, lower=True, unit_diagonal=True)
        v_new = u - w @ s
        o = (qc * jnp.exp(gam)[..., None]) @ s + ((qc @ kt) * d_incl) @ v_new
        g_last = gam[..., -1:]
        s = jnp.exp(g_last)[..., None] * s + jnp.swapaxes(kc * jnp.exp(g_last - gam)[..., None], -1, -2) @ v_new
        return s, o

    xs = tuple(to_chunks(a.astype(jnp.float32), size) for a in (q, k, v, beta, g))
    s, o = lax.scan(step, s0.astype(jnp.float32), xs)
    return from_chunks(o), s


def causal_conv(x, hist, w):
    t = x.shape[1]
    xp = jnp.concatenate([hist.astype(x.dtype), x], axis=1)
    y = sum(xp[:, j:j + t] * w[j] for j in range(CONV_W))
    return jax.nn.silu(y), xp[:, t:]


def even_mixer(h, layer, w_in, lam_vecs, diff_gain, lb, hgrn_gain, w_out, past_k, past_v, s0):
    b, t, _ = h.shape
    proj = h @ w_in
    sizes = [A_WIDTH, A_WIDTH, A_WIDTH, B_WIDTH, B_WIDTH, B_WIDTH]
    qa, ka, va, qb, fb, ib, gb = jnp.split(proj, list(np.cumsum(sizes)), axis=-1)
    qa = qa.reshape(b, t, A_HEADS, 2, A_QK_DIM)
    ka = ka.reshape(b, t, A_HEADS, 2, A_QK_DIM)
    va = va.reshape(b, t, A_HEADS, A_V_DIM)
    lam_init = 0.8 - 0.6 * math.exp(-0.3 * layer)
    lv = lam_vecs.astype(jnp.float32)
    lam = jnp.exp(jnp.sum(lv[0] * lv[1])) - jnp.exp(jnp.sum(lv[2] * lv[3])) + lam_init
    if past_k is None:
        oa = diff_attention_prompt(qa, ka, va, lam, lam_init, diff_gain)
    else:
        p_len = past_k.shape[1]
        k_all = jnp.concatenate([past_k.astype(ka.dtype), ka], axis=1)
        v_all = jnp.concatenate([past_v.astype(va.dtype), va], axis=1)
        q_pos = p_len + jnp.arange(t, dtype=jnp.int32)
        k_pos = jnp.arange(p_len + t, dtype=jnp.int32)
        oa = diff_attention_block(qa, k_all, v_all, q_pos, k_pos, lam, lam_init, diff_gain)
    f = lb + (1.0 - lb) * jax.nn.sigmoid(fb.astype(jnp.float32))
    heads = lambda a: a.reshape(b, t, B_HEADS, -1)
    ob, s_new = hgrn2_recurrence(heads(qb), heads(1.0 - f), heads(ib), heads(jnp.log(f)), s0)
    ob = rms_norm(ob, hgrn_gain) * jax.nn.silu(heads(gb).astype(jnp.float32))
    o = jnp.concatenate([oa.reshape(b, t, A_WIDTH).astype(h.dtype), ob.reshape(b, t, B_WIDTH).astype(h.dtype)], axis=-1)
    return o @ w_out, ka, va, s_new


def odd_mixer(h, w_in, conv_w, a_log, dt_bias, out_gain, w_out, conv_hist, s0):
    b, t, _ = h.shape
    proj = h @ w_in
    qkv, gate, a, bl = jnp.split(proj, [3 * C_WIDTH, 4 * C_WIDTH, 4 * C_WIDTH + C_HEADS], axis=-1)
    qkv, conv_new = causal_conv(qkv, conv_hist, conv_w)
    q, k, v = jnp.split(qkv, 3, axis=-1)
    q = l2_normalize(q.reshape(b, t, C_HEADS, C_DK)) * (C_DK ** -0.5)
    k = l2_normalize(k.reshape(b, t, C_HEADS, C_DK))
    v = v.reshape(b, t, C_HEADS, C_DV)
    beta = jax.nn.sigmoid(bl.astype(jnp.float32))
    g = -jnp.exp(a_log.astype(jnp.float32)) * jax.nn.softplus(a.astype(jnp.float32) + dt_bias.astype(jnp.float32))
    o, s_new = gated_delta_recurrence(q, k, v, beta, g, s0)
    o = rms_norm(o, out_gain) * jax.nn.silu(gate.reshape(b, t, C_HEADS, C_DV).astype(jnp.float32))
    return o.reshape(b, t, C_WIDTH).astype(h.dtype) @ w_out, s_new, conv_new


def setup_inputs(seed: int = 0) -> dict:
    key = jax.random.key(seed)
    ks = jax.random.split(key, 32)
    nrm = lambda k, shape, scale: scale * jax.random.normal(k, shape, jnp.float32)
    gain = lambda k, shape: 1.0 + 0.02 * jax.random.normal(k, shape, jnp.float32)
    dt = jnp.exp(jax.random.uniform(ks[24], (N_ODD, C_HEADS), jnp.float32, math.log(1e-3), math.log(1e-1)))
    return {
        'x_prompt': nrm(ks[0], (BATCH, SEQ, D_MODEL), 1.0),
        'x_sample': nrm(ks[1], (DEC_BATCH, DEC_SEQ, D_MODEL), 1.0),
        'cache_attn_k': nrm(ks[2], (N_EVEN, DEC_BATCH, PAST_LEN, A_HEADS, 2, A_QK_DIM), 1.0),
        'cache_attn_v': nrm(ks[3], (N_EVEN, DEC_BATCH, PAST_LEN, A_HEADS, A_V_DIM), 1.0),
        'state_hgrn': nrm(ks[4], (N_EVEN, DEC_BATCH, B_HEADS, B_DK, B_DV), 0.5),
        'state_delta': nrm(ks[5], (N_ODD, DEC_BATCH, C_HEADS, C_DK, C_DV), 0.1),
        'cache_conv': nrm(ks[6], (N_ODD, DEC_BATCH, CONV_W - 1, 3 * C_WIDTH), 1.0),
        'ffn1_norm': gain(ks[7], (DEPTH, D_MODEL)),
        'ffn1_w_in': nrm(ks[8], (DEPTH, D_MODEL, 2 * D_FF), D_MODEL ** -0.5),
        'ffn1_w_out': nrm(ks[9], (DEPTH, D_FF, D_MODEL), D_FF ** -0.5),
        'mix_norm': gain(ks[10], (DEPTH, D_MODEL)),
        'ffn2_norm': gain(ks[11], (DEPTH, D_MODEL)),
        'ffn2_w_in': nrm(ks[12], (DEPTH, D_MODEL, 2 * D_FF), D_MODEL ** -0.5),
        'ffn2_w_out': nrm(ks[13], (DEPTH, D_FF, D_MODEL), D_FF ** -0.5),
        'even_w_in': nrm(ks[14], (N_EVEN, D_MODEL, EVEN_PROJ), D_MODEL ** -0.5),
        'diff_lambda': nrm(ks[15], (N_EVEN, 4, A_QK_DIM), 0.1),
        'diff_norm': gain(ks[16], (N_EVEN, A_V_DIM)),
        'hgrn_lb_logits': nrm(ks[17], (N_EVEN + 1, B_WIDTH), 0.5),
        'hgrn_norm': gain(ks[18], (N_EVEN, B_DV)),
        'even_w_out': nrm(ks[19], (N_EVEN, A_WIDTH + B_WIDTH, D_MODEL), (A_WIDTH + B_WIDTH) ** -0.5),
        'odd_w_in': nrm(ks[20], (N_ODD, D_MODEL, ODD_PROJ), D_MODEL ** -0.5),
        'conv_w': nrm(ks[21], (N_ODD, CONV_W, 3 * C_WIDTH), CONV_W ** -0.5),
        'delta_A_log': jnp.log(jax.random.uniform(ks[22], (N_ODD, C_HEADS), jnp.float32, 1.0, 16.0)),
        'delta_dt_bias': dt + jnp.log(-jnp.expm1(-dt)),
        'delta_norm': gain(ks[23], (N_ODD, C_DV)),
        'odd_w_out': nrm(ks[25], (N_ODD, C_WIDTH, D_MODEL), C_WIDTH ** -0.5),
        'final_norm': gain(ks[26], (D_MODEL,)),
    }


def reference(x_prompt, x_sample, cache_attn_k, cache_attn_v, state_hgrn, state_delta, cache_conv,
              ffn1_norm, ffn1_w_in, ffn1_w_out, mix_norm, ffn2_norm, ffn2_w_in, ffn2_w_out,
              even_w_in, diff_lambda, diff_norm, hgrn_lb_logits, hgrn_norm, even_w_out,
              odd_w_in, conv_w, delta_A_log, delta_dt_bias, delta_norm, odd_w_out, final_norm):
    lower_bounds = jnp.cumsum(jax.nn.softmax(hgrn_lb_logits.astype(jnp.float32), axis=0), axis=0)

    def run(x, past_k, past_v, hgrn0, delta0, conv0):
        ks, vs, hs, ds, cs = [], [], [], [], []
        for layer in range(DEPTH):
            j = layer // 2
            x = x + 0.5 * swiglu(rms_norm(x, ffn1_norm[layer]), ffn1_w_in[layer], ffn1_w_out[layer])
            h = rms_norm(x, mix_norm[layer])
            if layer % 2 == 0:
                pk = None if past_k is None else past_k[j]
                pv = None if past_v is None else past_v[j]
                o, k, v, s = even_mixer(h, layer, even_w_in[j], diff_lambda[j], diff_norm[j], lower_bounds[j],
                                        hgrn_norm[j], even_w_out[j], pk, pv, hgrn0[j])
                ks.append(k)
                vs.append(v)
                hs.append(s.astype(x.dtype))
            else:
                o, s, c = odd_mixer(h, odd_w_in[j], conv_w[j], delta_A_log[j], delta_dt_bias[j], delta_norm[j],
                                    odd_w_out[j], conv0[j], delta0[j])
                ds.append(s.astype(x.dtype))
                cs.append(c)
            x = x + o
            x = x + 0.5 * swiglu(rms_norm(x, ffn2_norm[layer]), ffn2_w_in[layer], ffn2_w_out[layer])
        return (rms_norm(x, final_norm), jnp.stack(ks, axis=0), jnp.stack(vs, axis=0),
                jnp.stack(hs, axis=0), jnp.stack(ds, axis=0), jnp.stack(cs, axis=0))

    bp = x_prompt.shape[0]
    zeros_h = jnp.zeros((N_EVEN, bp, B_HEADS, B_DK, B_DV), jnp.float32)
    zeros_d = jnp.zeros((N_ODD, bp, C_HEADS, C_DK, C_DV), jnp.float32)
    zeros_c = jnp.zeros((N_ODD, bp, CONV_W - 1, 3 * C_WIDTH), x_prompt.dtype)
    y_prompt, k_p, v_p, h_p, d_p, c_p = run(x_prompt, None, None, zeros_h, zeros_d, zeros_c)
    y_sample, k_s, v_s, h_s, d_s, c_s = run(x_sample, cache_attn_k, cache_attn_v, state_hgrn, state_delta, cache_conv)
    return (y_prompt, y_sample, k_p, v_p, h_p, d_p, c_p, k_s, v_s, h_s, d_s, c_s)
```

```python
import functools
import math

import jax
import jax.numpy as jnp
from jax import lax
from jax.experimental import pallas as pl
from jax.experimental.pallas import tpu as pltpu

CHUNK = 64
A_QK_DIM = 64
A_HEAD = 2 * A_QK_DIM
B_DK = 128
C_DK = 128
CONV_W = 4
NORM_EPS = 1e-6
DIFF_NORM_EPS = 1e-5
L2_EPS = 1e-6

LANES = 128
BF16_ROWS = 16
FF_CHUNK = 256
TOKEN_TILE = 512
ATTN_TILE = 256
VMEM_LIMIT = 56 * 1024 * 1024

_BF = jnp.bfloat16
_F32 = jnp.float32
_HI = lax.Precision.HIGHEST
_NT = (((1,), (1,)), ((), ()))
_TN = (((0,), (0,)), ((), ()))


def _params(*sem):
    return pltpu.CompilerParams(dimension_semantics=sem, vmem_limit_bytes=VMEM_LIMIT)


def _rms(x, gain, eps):
    return x * lax.rsqrt(jnp.mean(x * x, axis=-1, keepdims=True) + eps) * gain


def _silu(x):
    return x * jax.nn.sigmoid(x)


def _const_spec(shape):
    nd = len(shape)
    return pl.BlockSpec(shape, lambda *_: (0,) * nd)


def _ffn_kernel(*refs, n_mix, final, n_chunks):
    refs = list(refs)
    x_ref = refs.pop(0)
    mix = [(refs.pop(0), refs.pop(0)) for _ in range(n_mix)]
    g_ref, wg_ref, wu_ref, wo_ref = refs[:4]
    refs = refs[4:]
    fg_ref = refs.pop(0) if final else None
    o_ref, acc_ref = refs

    x = x_ref[...]
    for a_ref, w_ref in mix:
        x = x + jnp.dot(a_ref[...], w_ref[...], preferred_element_type=_F32)
    o_ref[...] = x
    h = _rms(x, g_ref[...], NORM_EPS).astype(_BF)
    for c in range(n_chunks):
        gate = jnp.dot(h, wg_ref[c], preferred_element_type=_F32)
        up = jnp.dot(h, wu_ref[c], preferred_element_type=_F32)
        act = (_silu(gate) * up).astype(_BF)
        part = jnp.dot(act, wo_ref[c], preferred_element_type=_F32)
        if c == 0:
            acc_ref[...] = part
        else:
            acc_ref[...] += part
    y = o_ref[...] + 0.5 * acc_ref[...]
    if final:
        y = _rms(y, fg_ref[...], NORM_EPS)
    o_ref[...] = y


def _ffn(x, mix, gain, wg, wu, wo, final_gain=None):
    n, d = x.shape
    tm = min(TOKEN_TILE, n)
    n_chunks = wg.shape[0]
    row = lambda i: (i, 0)
    args, specs = [x], [pl.BlockSpec((tm, d), row)]
    for a, w in mix:
        args += [a, w]
        specs += [pl.BlockSpec((tm, a.shape[1]), row), _const_spec(w.shape)]
    args += [gain, wg, wu, wo]
    specs += [_const_spec(gain.shape), _const_spec(wg.shape), _const_spec(wu.shape), _const_spec(wo.shape)]
    if final_gain is not None:
        args.append(final_gain)
        specs.append(_const_spec(final_gain.shape))
    kern = functools.partial(_ffn_kernel, n_mix=len(mix), final=final_gain is not None, n_chunks=n_chunks)
    return pl.pallas_call(
        kern,
        grid=(n // tm,),
        in_specs=specs,
        out_specs=pl.BlockSpec((tm, d), row),
        out_shape=jax.ShapeDtypeStruct((n, d), _F32),
        scratch_shapes=[pltpu.VMEM((tm, d), _F32)],
        compiler_params=_params("parallel"),
        name="ffn",
    )(*args)


def _proj_kernel(x_ref, g_ref, w_ref, *out_refs, groups):
    h = _rms(x_ref[...], g_ref[...], NORM_EPS).astype(_BF)
    for o_ref, (c0, c1, scale) in zip(out_refs, groups):
        for s in range(c0, c1, 512):
            e = min(s + 512, c1)
            r = jnp.dot(h, w_ref[:, s:e], preferred_element_type=_F32)
            if scale != 1.0:
                r = r * scale
            o_ref[:, s - c0:e - c0] = r.astype(o_ref.dtype)


def _proj(x, gain, w, groups, dtypes):
    n, d = x.shape
    tm = min(TOKEN_TILE, n)
    row = lambda i: (i, 0)
    return pl.pallas_call(
        functools.partial(_proj_kernel, groups=groups),
        grid=(n // tm,),
        in_specs=[pl.BlockSpec((tm, d), row), _const_spec(gain.shape), _const_spec(w.shape)],
        out_specs=[pl.BlockSpec((tm, c1 - c0), row) for c0, c1, _ in groups],
        out_shape=[jax.ShapeDtypeStruct((n, c1 - c0), dt) for (c0, c1, _), dt in zip(groups, dtypes)],
        compiler_params=_params("parallel"),
        name="proj",
    )(x, gain, w)


def _lane_bcast(m, n):
    if n <= LANES:
        return m[:, :n]
    return jnp.concatenate([m] * (n // LANES), axis=1)


def _attn_kernel(scal_ref, q_ref, k_ref, v_ref, *rest, tq, tk, n_past, lam_init):
    if n_past:
        pk_ref, pv_ref, gain_ref, o_ref, m_ref, l_ref, acc_ref = rest
    else:
        gain_ref, o_ref, m_ref, l_ref, acc_ref = rest
    head = pl.program_id(1)
    qi = pl.program_id(2)
    lam = scal_ref[0]
    slope = scal_ref[1 + head]

    q = q_ref[0]
    lane = lax.broadcasted_iota(jnp.int32, q.shape, 1)
    zero = jnp.zeros_like(q)
    q_maps = (jnp.where(lane < A_QK_DIM, q, zero), jnp.where(lane >= A_QK_DIM, q, zero))

    m_ref[...] = jnp.full(m_ref.shape, -jnp.inf, _F32)
    l_ref[...] = jnp.zeros(l_ref.shape, _F32)
    acc_ref[...] = jnp.zeros(acc_ref.shape, _F32)

    def update(kt, vt, bias):
        n = kt.shape[0]
        for mi in range(2):
            s = lax.dot_general(q_maps[mi], kt, _NT, preferred_element_type=_F32) + bias
            m_prev = m_ref[mi]
            m_next = jnp.maximum(m_prev, jnp.max(s, axis=1, keepdims=True))
            p = jnp.exp(s - _lane_bcast(m_next, n))
            alpha = jnp.exp(m_prev - m_next)
            l_ref[mi] = alpha * l_ref[mi] + jnp.sum(p, axis=1, keepdims=True)
            acc_ref[mi] = alpha * acc_ref[mi] + jnp.dot(p.astype(_BF), vt, preferred_element_type=_F32)
            m_ref[mi] = m_next

    def rel(nk):
        r = lax.broadcasted_iota(jnp.int32, (tq, nk), 0)
        c = lax.broadcasted_iota(jnp.int32, (tq, nk), 1)
        return r, c

    r, c = rel(tk)
    base = (-slope) * (r - c).astype(_F32)
    if n_past:
        past_len = n_past * tk

        def past_tile(kj, carry):
            off = pl.multiple_of(kj * tk, tk)
            gap = (past_len - kj * tk).astype(_F32)
            update(pk_ref[0, pl.ds(off, tk), :].astype(_BF), pv_ref[0, pl.ds(off, tk), :].astype(_BF),
                   base - slope * gap)
            return carry

        lax.fori_loop(0, n_past, past_tile, 0)
        cur_k, cur_v, nk = k_ref[0], v_ref[0], tq
    else:
        def past_tile(kj, carry):
            off = pl.multiple_of(kj * tk, tk)
            gap = ((qi - kj) * tk).astype(_F32)
            update(k_ref[0, pl.ds(off, tk), :].astype(_BF), v_ref[0, pl.ds(off, tk), :].astype(_BF),
                   base - slope * gap)
            return carry

        lax.fori_loop(0, qi, past_tile, 0)
        off = pl.multiple_of(qi * tk, tk)
        cur_k, cur_v, nk = k_ref[0, pl.ds(off, tk), :], v_ref[0, pl.ds(off, tk), :], tk

    r, c = rel(nk)
    dist = jnp.abs(r - c).astype(_F32)
    bias = jnp.where((c // CHUNK) <= (r // CHUNK), (-slope) * dist, -jnp.inf)
    update(cur_k.astype(_BF), cur_v.astype(_BF), bias)

    o = acc_ref[0] / l_ref[0] - lam * (acc_ref[1] / l_ref[1])
    o = _rms(o, gain_ref[...], DIFF_NORM_EPS) * (1.0 - lam_init)
    o_ref[0] = o.astype(o_ref.dtype)


def _attention(scal, q, k, v, past_k, past_v, gain, lam_init):
    b, t, width = q.shape
    heads = width // A_HEAD
    has_past = past_k is not None
    if has_past:
        tq, tk = t, ATTN_TILE
        assert t <= CHUNK and past_k.shape[1] % tk == 0 and tk % CHUNK == 0
        n_past = past_k.shape[1] // tk
    else:
        tq = tk = min(ATTN_TILE, t)
        assert t % tq == 0 and tq % CHUNK == 0
        n_past = 0
    tile = lambda bi, hi, qi: (bi, qi, hi)
    whole = lambda bi, hi, qi: (bi, 0, hi)
    args = [scal, q, k, v]
    specs = [pl.BlockSpec(memory_space=pltpu.SMEM),
             pl.BlockSpec((1, tq, A_HEAD), tile),
             pl.BlockSpec((1, t, A_HEAD), whole),
             pl.BlockSpec((1, t, A_HEAD), whole)]
    if has_past:
        args += [past_k, past_v]
        specs += [pl.BlockSpec((1, past_k.shape[1], A_HEAD), whole)] * 2
    args.append(gain)
    specs.append(_const_spec(gain.shape))
    kern = functools.partial(_attn_kernel, tq=tq, tk=tk, n_past=n_past, lam_init=lam_init)
    return pl.pallas_call(
        kern,
        grid=(b, heads, t // tq),
        in_specs=specs,
        out_specs=pl.BlockSpec((1, tq, A_HEAD), tile),
        out_shape=jax.ShapeDtypeStruct((b, t, width), _BF),
        scratch_shapes=[pltpu.VMEM((2, tq, LANES), _F32)] * 3,
        compiler_params=_params("parallel", "parallel", "arbitrary"),
        name="attn",
    )(*args)


def _hgrn_kernel(hg_ref, lb_ref, gain_ref, s0_ref, ob_ref, sout_ref, st_ref, row_ref, w_ref, r_ref, *, L, heads):
    t = pl.program_id(1)
    width = heads * B_DK

    @pl.when(t == 0)
    def _():
        for h in range(heads):
            st_ref[h] = s0_ref[0, h].T

    rr = lax.broadcasted_iota(jnp.int32, (L, L), 0)
    cc = lax.broadcasted_iota(jnp.int32, (L, L), 1)
    tri = (rr >= cc).astype(_F32)
    ones = jnp.ones((LANES, LANES), _BF)
    blk = BF16_ROWS
    nb = L // blk
    blk_row = lax.broadcasted_iota(jnp.int32, (blk, LANES), 0)

    for h in range(heads):
        sl = lambda g: slice(g * width + h * B_DK, g * width + (h + 1) * B_DK)
        qb = hg_ref[0, :, sl(0)]
        fb = hg_ref[0, :, sl(1)]
        ib = hg_ref[0, :, sl(2)]
        gb = hg_ref[0, :, sl(3)]
        lb = lb_ref[:, h * B_DK:(h + 1) * B_DK]
        f = lb + (1.0 - lb) * jax.nn.sigmoid(fb)
        kk = 1.0 - f
        cb = jnp.dot(tri, jnp.log(f), precision=_HI, preferred_element_type=_F32)
        row_ref[0] = cb
        row_ref[1] = kk
        row_ref[2] = ib
        st = st_ref[h]
        o = lax.dot_general((qb * jnp.exp(cb)).astype(_BF), st.astype(_BF), _NT, preferred_element_type=_F32)

        q_blk = [qb[j * blk:(j + 1) * blk] for j in range(nb)]
        cb_blk = [cb[j * blk:(j + 1) * blk] for j in range(nb)]
        off = 0
        for s in range(L):
            cbs = row_ref[0, s:s + 1, :]
            ks = row_ref[1, s:s + 1, :]
            for j in range(s // blk, nb):
                d = cb_blk[j] - cbs
                if j == s // blk:
                    d = jnp.where(blk_row >= s - j * blk, d, -jnp.inf)
                w_ref[off:off + blk, :] = (q_blk[j] * (ks * jnp.exp(d))).astype(_BF)
                off += blk
        r_ref[0:off, :] = jnp.dot(w_ref[0:off, :], ones, preferred_element_type=_F32)
        o_blk = [o[j * blk:(j + 1) * blk] for j in range(nb)]
        off = 0
        for s in range(L):
            vs = row_ref[2, s:s + 1, :]
            for j in range(s // blk, nb):
                o_blk[j] = o_blk[j] + r_ref[off:off + blk, :] * vs
                off += blk
        o = jnp.concatenate(o_blk, axis=0)

        ob = _rms(o, gain_ref[...], NORM_EPS) * _silu(gb)
        ob_ref[0, :, h * B_DK:(h + 1) * B_DK] = ob.astype(ob_ref.dtype)

        cl = cb[L - 1:L, :]
        k_tail = kk * jnp.exp(cl - cb)
        st_new = st * jnp.exp(cl) + lax.dot_general(ib.astype(_BF), k_tail.astype(_BF), _TN,
                                                    preferred_element_type=_F32)
        st_ref[h] = st_new

    @pl.when(t == pl.num_programs(1) - 1)
    def _():
        for h in range(heads):
            sout_ref[0, h] = st_ref[h].T


def _hgrn(hg, lb, gain, s0):
    b, t, four_w = hg.shape
    width = four_w // 4
    heads = width // B_DK
    L = min(CHUNK, t)
    n_rows = sum(L - BF16_ROWS * (s // BF16_ROWS) for s in range(L))
    kern = functools.partial(_hgrn_kernel, L=L, heads=heads)
    state_spec = pl.BlockSpec((1, heads, B_DK, B_DK), lambda bi, ti: (bi, 0, 0, 0))
    return pl.pallas_call(
        kern,
        grid=(b, t // L),
        in_specs=[pl.BlockSpec((1, L, four_w), lambda bi, ti: (bi, ti, 0)),
                  _const_spec(lb.shape), _const_spec(gain.shape), state_spec],
        out_specs=[pl.BlockSpec((1, L, width), lambda bi, ti: (bi, ti, 0)), state_spec],
        out_shape=[jax.ShapeDtypeStruct((b, t, width), _BF),
                   jax.ShapeDtypeStruct((b, heads, B_DK, B_DK), _F32)],
        scratch_shapes=[pltpu.VMEM((heads, B_DK, B_DK), _F32),
                        pltpu.VMEM((3, L, LANES), _F32),
                        pltpu.VMEM((n_rows, LANES), _BF),
                        pltpu.VMEM((n_rows, LANES), _F32)],
        compiler_params=_params("parallel", "arbitrary"),
        name="hgrn",
    )(hg, lb, gain, s0)


def _softplus(x):
    return jnp.maximum(x, 0.0) + jnp.log1p(jnp.exp(-jnp.abs(x)))


def _delta_kernel(qkv_ref, gate_ref, ab_ref, cw_ref, alog_ref, dtb_ref, gain_ref, hist_ref, s0_ref,
                  o_ref, sout_ref, cout_ref, xp_ref, s_ref, *, L, heads):
    t = pl.program_id(1)
    width = heads * C_DK
    pad = 8
    lo = pad - (CONV_W - 1)

    @pl.when(t == 0)
    def _():
        xp_ref[lo:pad, :] = hist_ref[0]
        for h in range(heads):
            s_ref[h] = s0_ref[0, h]

    xp_ref[pad:pad + L, :] = qkv_ref[0]
    tail = xp_ref[lo + L:pad + L, :]
    cout_ref[0] = tail

    def conv(c0):
        y = cw_ref[0:1, c0:c0 + C_DK] * xp_ref[lo:lo + L, c0:c0 + C_DK]
        for j in range(1, CONV_W):
            y = y + cw_ref[j:j + 1, c0:c0 + C_DK] * xp_ref[lo + j:lo + j + L, c0:c0 + C_DK]
        return _silu(y)

    rr = lax.broadcasted_iota(jnp.int32, (L, L), 0)
    cc = lax.broadcasted_iota(jnp.int32, (L, L), 1)
    lower = (rr >= cc).astype(_F32)
    upper = (rr <= cc).astype(_F32)
    eye = (rr == cc).astype(_F32)

    ab = ab_ref[0]
    g_all = -jnp.exp(alog_ref[...]) * _softplus(ab + dtb_ref[...])
    beta_all = jax.nn.sigmoid(ab)
    gam_col = jnp.dot(lower, g_all, precision=_HI, preferred_element_type=_F32)
    gam_row = lax.dot_general(g_all, upper, _TN, precision=_HI, preferred_element_type=_F32)

    for h in range(heads):
        gc = gam_col[:, h:h + 1]
        gr = gam_row[h:h + 1, :]
        bc = beta_all[:, heads + h:heads + h + 1]
        d_incl = jnp.exp(jnp.where(rr >= cc, gc - gr, -jnp.inf))
        d_strict = jnp.where(rr > cc, d_incl, 0.0)

        q = conv(h * C_DK)
        k = conv(width + h * C_DK)
        v = conv(2 * width + h * C_DK)
        q = q * lax.rsqrt(jnp.sum(q * q, axis=-1, keepdims=True) + L2_EPS) * (C_DK ** -0.5)
        k = k * lax.rsqrt(jnp.sum(k * k, axis=-1, keepdims=True) + L2_EPS)
        kb = k.astype(_BF)
        kk = lax.dot_general(kb, kb, _NT, preferred_element_type=_F32)
        qk = lax.dot_general(q.astype(_BF), kb, _NT, preferred_element_type=_F32)

        a = bc * kk * d_strict
        inv = eye - a
        p = a
        pw = 1
        while 2 * pw < L:
            pb = p.astype(_BF)
            p = jnp.dot(pb, pb, preferred_element_type=_F32)
            pw *= 2
            inv = inv + jnp.dot(inv.astype(_BF), p.astype(_BF), preferred_element_type=_F32)

        eg = jnp.exp(gc)
        rhs = jnp.concatenate([k * (bc * eg), v * bc], axis=1).astype(_BF)
        wu = jnp.dot(inv.astype(_BF), rhs, preferred_element_type=_F32)
        w = wu[:, :C_DK]
        u = wu[:, C_DK:]
        s = s_ref[h]
        sb = s.astype(_BF)
        v_new = u - jnp.dot(w.astype(_BF), sb, preferred_element_type=_F32)
        vb = v_new.astype(_BF)
        o = (jnp.dot((q * eg).astype(_BF), sb, preferred_element_type=_F32)
             + jnp.dot((qk * d_incl).astype(_BF), vb, preferred_element_type=_F32))
        gl = gc[L - 1:L, :]
        s_new = jnp.exp(gl) * s + lax.dot_general((k * jnp.exp(gl - gc)).astype(_BF), vb, _TN,
                                                  preferred_element_type=_F32)
        s_ref[h] = s_new

        on = _rms(o, gain_ref[...], NORM_EPS) * _silu(gate_ref[0, :, h * C_DK:(h + 1) * C_DK])
        o_ref[0, :, h * C_DK:(h + 1) * C_DK] = on.astype(o_ref.dtype)

    xp_ref[lo:pad, :] = tail

    @pl.when(t == pl.num_programs(1) - 1)
    def _():
        for h in range(heads):
            sout_ref[0, h] = s_ref[h]


def _delta(qkv, gate, ab, conv_w, alog, dtb, gain, hist, s0):
    b, t, three_w = qkv.shape
    width = three_w // 3
    heads = width // C_DK
    L = min(CHUNK, t)
    kern = functools.partial(_delta_kernel, L=L, heads=heads)
    step = lambda bi, ti: (bi, ti, 0)
    per_b3 = lambda bi, ti: (bi, 0, 0)
    state_spec = pl.BlockSpec((1, heads, C_DK, C_DK), lambda bi, ti: (bi, 0, 0, 0))
    hist_spec = pl.BlockSpec((1, CONV_W - 1, three_w), per_b3)
    return pl.pallas_call(
        kern,
        grid=(b, t // L),
        in_specs=[pl.BlockSpec((1, L, three_w), step), pl.BlockSpec((1, L, width), step),
                  pl.BlockSpec((1, L, LANES), step), _const_spec(conv_w.shape), _const_spec(alog.shape),
                  _const_spec(dtb.shape), _const_spec(gain.shape), hist_spec, state_spec],
        out_specs=[pl.BlockSpec((1, L, width), step), state_spec, hist_spec],
        out_shape=[jax.ShapeDtypeStruct((b, t, width), _BF),
                   jax.ShapeDtypeStruct((b, heads, C_DK, C_DK), _F32),
                   jax.ShapeDtypeStruct((b, CONV_W - 1, three_w), _F32)],
        scratch_shapes=[pltpu.VMEM((8 + L, three_w), _F32),
                        pltpu.VMEM((heads, C_DK, C_DK), _F32)],
        compiler_params=_params("parallel", "arbitrary"),
        name="delta",
    )(qkv, gate, ab, conv_w, alog, dtb, gain, hist, s0)


def _ffn_weights(w_in, w_out):
    d, two_ff = w_in.shape
    d_ff = two_ff // 2
    n_chunks = d_ff // FF_CHUNK
    split = lambda w: w.astype(_BF).reshape(d, n_chunks, FF_CHUNK).transpose(1, 0, 2)
    return split(w_in[:, :d_ff]), split(w_in[:, d_ff:]), w_out.astype(_BF).reshape(n_chunks, FF_CHUNK, d)


def _row(v):
    return v.astype(_F32).reshape(1, -1)


def _pad_lanes(v):
    return jnp.pad(_row(v), ((0, 0), (0, LANES - v.shape[-1])))


def kernel(x_prompt, x_sample, cache_attn_k, cache_attn_v, state_hgrn, state_delta, cache_conv, ffn1_norm, ffn1_w_in, ffn1_w_out, mix_norm, ffn2_norm, ffn2_w_in, ffn2_w_out, even_w_in, diff_lambda, diff_norm, hgrn_lb_logits, hgrn_norm, even_w_out, odd_w_in, conv_w, delta_A_log, delta_dt_bias, delta_norm, odd_w_out, final_norm):
    depth, d_model = ffn1_norm.shape
    a_width = cache_attn_k.shape[3] * cache_attn_k.shape[4] * cache_attn_k.shape[5]
    a_heads = cache_attn_k.shape[3]
    b_heads, b_width = state_hgrn.shape[2], state_hgrn.shape[2] * state_hgrn.shape[3]
    c_heads, c_width = state_delta.shape[2], state_delta.shape[2] * state_delta.shape[3]

    lower_bounds = jnp.cumsum(jax.nn.softmax(hgrn_lb_logits.astype(_F32), axis=0), axis=0)
    slopes = 2.0 ** (-8.0 * jnp.arange(1, a_heads + 1, dtype=_F32) / a_heads)
    ffn1 = [_ffn_weights(ffn1_w_in[l], ffn1_w_out[l]) for l in range(depth)]
    ffn2 = [_ffn_weights(ffn2_w_in[l], ffn2_w_out[l]) for l in range(depth)]

    def run(x, past_k, past_v, hgrn0, delta0, conv0):
        b, t, _ = x.shape
        x = x.reshape(b * t, d_model)
        ks, vs, hs, ds, cs = [], [], [], [], []
        mix = []
        for layer in range(depth):
            j = layer // 2
            x = _ffn(x, mix, _row(ffn1_norm[layer]), *ffn1[layer])
            if layer % 2 == 0:
                groups = ((0, a_width, A_QK_DIM ** -0.5), (a_width, 2 * a_width, 1.0),
                          (2 * a_width, 3 * a_width, 1.0), (3 * a_width, 3 * a_width + 4 * b_width, 1.0))
                q, k, v, hg = _proj(x, _row(mix_norm[layer]), even_w_in[j].astype(_BF), groups,
                                    (_BF, _F32, _F32, _F32))
                lam_init = 0.8 - 0.6 * math.exp(-0.3 * layer)
                lv = diff_lambda[j].astype(_F32)
                lam = jnp.exp(jnp.sum(lv[0] * lv[1])) - jnp.exp(jnp.sum(lv[2] * lv[3])) + lam_init
                scal = jnp.concatenate([lam.reshape(1), slopes])
                seq = lambda a: a.reshape(b, t, -1)
                pk = None if past_k is None else past_k[j].reshape(b, -1, a_width)
                pv = None if past_v is None else past_v[j].reshape(b, -1, a_width)
                oa = _attention(scal, seq(q), seq(k), seq(v), pk, pv, _row(diff_norm[j]), lam_init)
                ob, s_new = _hgrn(seq(hg), _row(lower_bounds[j]), _row(hgrn_norm[j]), hgrn0[j])
                ks.append(k.reshape(b, t, a_heads, 2, A_QK_DIM))
                vs.append(v.reshape(b, t, a_heads, A_HEAD))
                hs.append(s_new)
                w_out = even_w_out[j].astype(_BF)
                mix = [(oa.reshape(b * t, a_width), w_out[:a_width]), (ob.reshape(b * t, b_width), w_out[a_width:])]
            else:
                w_in = odd_w_in[j]
                w_cat = jnp.concatenate([w_in[:, :4 * c_width],
                                         jnp.pad(w_in[:, 4 * c_width:], ((0, 0), (0, LANES - 2 * c_heads)))],
                                        axis=1).astype(_BF)
                groups = ((0, 3 * c_width, 1.0), (3 * c_width, 4 * c_width, 1.0),
                          (4 * c_width, 4 * c_width + LANES, 1.0))
                qkv, gate, ab = _proj(x, _row(mix_norm[layer]), w_cat, groups, (_F32, _F32, _F32))
                seq = lambda a: a.reshape(b, t, -1)
                o, s_new, c_new = _delta(seq(qkv), seq(gate), seq(ab), conv_w[j].astype(_F32),
                                         _pad_lanes(delta_A_log[j]), _pad_lanes(delta_dt_bias[j]),
                                         _row(delta_norm[j]), conv0[j], delta0[j])
                ds.append(s_new)
                cs.append(c_new)
                mix = [(o.reshape(b * t, c_width), odd_w_out[j].astype(_BF))]
            final = _row(final_norm) if layer == depth - 1 else None
            x = _ffn(x, mix, _row(ffn2_norm[layer]), *ffn2[layer], final_gain=final)
            mix = []
        return (x.reshape(b, t, d_model), jnp.stack(ks, axis=0), jnp.stack(vs, axis=0),
                jnp.stack(hs, axis=0), jnp.stack(ds, axis=0), jnp.stack(cs, axis=0))

    bp = x_prompt.shape[0]
    n_even, n_odd = state_hgrn.shape[0], state_delta.shape[0]
    zeros_h = jnp.zeros((n_even, bp) + state_hgrn.shape[2:], _F32)
    zeros_d = jnp.zeros((n_odd, bp) + state_delta.shape[2:], _F32)
    zeros_c = jnp.zeros((n_odd, bp) + cache_conv.shape[2:], x_prompt.dtype)
    y_p, k_p, v_p, h_p, d_p, c_p = run(x_prompt, None, None, zeros_h, zeros_d, zeros_c)
    y_s, k_s, v_s, h_s, d_s, c_s = run(x_sample, cache_attn_k, cache_attn_v, state_hgrn, state_delta, cache_conv)
    return (y_p, y_s, k_p, v_p, h_p, d_p, c_p, k_s, v_s, h_s, d_s, c_s)
```

```python
import functools
import math

import jax
import jax.numpy as jnp
from jax import lax
from jax.experimental import pallas as pl
from jax.experimental.pallas import tpu as pltpu

CHUNK = 64
A_QK_DIM = 64
A_HEAD = 2 * A_QK_DIM
B_DK = 128
C_DK = 128
CONV_W = 4
NORM_EPS = 1e-6
DIFF_NORM_EPS = 1e-5
L2_EPS = 1e-6

LANES = 128
BF16_ROWS = 16
FF_CHUNK = 256
TOKEN_TILE = 512
ATTN_TILE = 512
DELTA_CHUNKS = 2
VMEM_LIMIT = 56 * 1024 * 1024

_BF = jnp.bfloat16
_F32 = jnp.float32
_HI = lax.Precision.HIGHEST
_NT = (((1,), (1,)), ((), ()))
_TN = (((0,), (0,)), ((), ()))


def _params(*sem):
    return pltpu.CompilerParams(dimension_semantics=sem, vmem_limit_bytes=VMEM_LIMIT)


def _rms(x, gain, eps):
    return x * lax.rsqrt(jnp.mean(x * x, axis=-1, keepdims=True) + eps) * gain


def _silu(x):
    return x * jax.nn.sigmoid(x)


def _const_spec(shape):
    nd = len(shape)
    return pl.BlockSpec(shape, lambda *_: (0,) * nd)


def _mm(a, b):
    return jnp.dot(a.astype(_BF), b.astype(_BF), preferred_element_type=_F32)


def _ffn_kernel(*refs, n_mix, final, n_chunks):
    refs = list(refs)
    x_ref = refs.pop(0)
    mix = [(refs.pop(0), refs.pop(0)) for _ in range(n_mix)]
    g_ref, wg_ref, wu_ref, wo_ref = refs[:4]
    refs = refs[4:]
    fg_ref = refs.pop(0) if final else None
    o_ref, acc_ref = refs

    x = x_ref[...]
    for a_ref, w_ref in mix:
        x = x + jnp.dot(a_ref[...], w_ref[...], preferred_element_type=_F32)
    o_ref[...] = x
    h = _rms(x, g_ref[...], NORM_EPS).astype(_BF)
    for c in range(n_chunks):
        gate = jnp.dot(h, wg_ref[c], preferred_element_type=_F32)
        up = jnp.dot(h, wu_ref[c], preferred_element_type=_F32)
        act = (_silu(gate) * up).astype(_BF)
        part = jnp.dot(act, wo_ref[c], preferred_element_type=_F32)
        if c == 0:
            acc_ref[...] = part
        else:
            acc_ref[...] += part
    y = o_ref[...] + 0.5 * acc_ref[...]
    if final:
        y = _rms(y, fg_ref[...], NORM_EPS)
    o_ref[...] = y


def _ffn(x, mix, gain, wg, wu, wo, final_gain=None):
    n, d = x.shape
    tm = min(TOKEN_TILE, n)
    n_chunks = wg.shape[0]
    row = lambda i: (i, 0)
    args, specs = [x], [pl.BlockSpec((tm, d), row)]
    for a, w in mix:
        args += [a, w]
        specs += [pl.BlockSpec((tm, a.shape[1]), row), _const_spec(w.shape)]
    args += [gain, wg, wu, wo]
    specs += [_const_spec(gain.shape), _const_spec(wg.shape), _const_spec(wu.shape), _const_spec(wo.shape)]
    if final_gain is not None:
        args.append(final_gain)
        specs.append(_const_spec(final_gain.shape))
    kern = functools.partial(_ffn_kernel, n_mix=len(mix), final=final_gain is not None, n_chunks=n_chunks)
    return pl.pallas_call(
        kern,
        grid=(n // tm,),
        in_specs=specs,
        out_specs=pl.BlockSpec((tm, d), row),
        out_shape=jax.ShapeDtypeStruct((n, d), _F32),
        scratch_shapes=[pltpu.VMEM((tm, d), _F32)],
        compiler_params=_params("parallel"),
        name="ffn",
    )(*args)


def _proj_kernel(x_ref, g_ref, w_ref, *out_refs, groups):
    h = _rms(x_ref[...], g_ref[...], NORM_EPS).astype(_BF)
    for o_ref, (c0, c1, scale) in zip(out_refs, groups):
        for s in range(c0, c1, 512):
            e = min(s + 512, c1)
            r = jnp.dot(h, w_ref[:, s:e], preferred_element_type=_F32)
            if scale != 1.0:
                r = r * scale
            o_ref[:, s - c0:e - c0] = r.astype(o_ref.dtype)


def _proj(x, gain, w, groups, dtypes):
    n, d = x.shape
    tm = min(TOKEN_TILE, n)
    row = lambda i: (i, 0)
    return pl.pallas_call(
        functools.partial(_proj_kernel, groups=groups),
        grid=(n // tm,),
        in_specs=[pl.BlockSpec((tm, d), row), _const_spec(gain.shape), _const_spec(w.shape)],
        out_specs=[pl.BlockSpec((tm, c1 - c0), row) for c0, c1, _ in groups],
        out_shape=[jax.ShapeDtypeStruct((n, c1 - c0), dt) for (c0, c1, _), dt in zip(groups, dtypes)],
        compiler_params=_params("parallel"),
        name="proj",
    )(x, gain, w)


def _lane_bcast(m, n):
    if n <= LANES:
        return m[:, :n]
    return jnp.concatenate([m] * (n // LANES), axis=1)


def _attn_kernel(scal_ref, q_ref, k_ref, v_ref, *rest, tq, tk, n_past, lam_init):
    if n_past:
        pk_ref, pv_ref, gain_ref, o_ref, m_ref, l_ref, acc_ref = rest
    else:
        gain_ref, o_ref, m_ref, l_ref, acc_ref = rest
    head = pl.program_id(1)
    qi = pl.program_id(2)
    lam = scal_ref[0]
    slope = scal_ref[1 + head]

    q = q_ref[0]
    lane = lax.broadcasted_iota(jnp.int32, q.shape, 1)
    zero = jnp.zeros_like(q)
    q_maps = (jnp.where(lane < A_QK_DIM, q, zero), jnp.where(lane >= A_QK_DIM, q, zero))

    m_ref[...] = jnp.full(m_ref.shape, -jnp.inf, _F32)
    l_ref[...] = jnp.zeros(l_ref.shape, _F32)
    acc_ref[...] = jnp.zeros(acc_ref.shape, _F32)

    def update(kt, vt, bias, shift):
        n = kt.shape[0]
        kt = kt.astype(_BF)
        v_aug = jnp.concatenate([vt.astype(_BF), jnp.ones((n, LANES), _BF)], axis=1)
        s = [lax.dot_general(q_maps[mi], kt, _NT, preferred_element_type=_F32) + bias for mi in range(2)]
        p, alpha = [], []
        for mi in range(2):
            m_prev = m_ref[mi]
            m_next = jnp.maximum(m_prev, jnp.max(s[mi], axis=1, keepdims=True) - shift)
            p.append(jnp.exp(s[mi] - _lane_bcast(m_next + shift, n)).astype(_BF))
            alpha.append(jnp.exp(m_prev - m_next))
            m_ref[mi] = m_next
        for mi in range(2):
            pv = jnp.dot(p[mi], v_aug, preferred_element_type=_F32)
            acc_ref[mi] = alpha[mi] * acc_ref[mi] + pv[:, :LANES]
            l_ref[mi] = alpha[mi] * l_ref[mi] + pv[:, LANES:]

    def rel(nk):
        r = lax.broadcasted_iota(jnp.int32, (tq, nk), 0)
        c = lax.broadcasted_iota(jnp.int32, (tq, nk), 1)
        return r, c

    r, c = rel(tk)
    base = (-slope) * (r - c).astype(_F32)
    if n_past:
        past_len = n_past * tk

        def past_tile(kj, carry):
            off = pl.multiple_of(kj * tk, tk)
            gap = (past_len - kj * tk).astype(_F32)
            update(pk_ref[0, pl.ds(off, tk), :], pv_ref[0, pl.ds(off, tk), :], base, slope * gap)
            return carry

        lax.fori_loop(0, n_past, past_tile, 0)
        cur_k, cur_v, nk = k_ref[0], v_ref[0], tq
    else:
        def past_tile(kj, carry):
            off = pl.multiple_of(kj * tk, tk)
            gap = ((qi - kj) * tk).astype(_F32)
            update(k_ref[0, pl.ds(off, tk), :], v_ref[0, pl.ds(off, tk), :], base, slope * gap)
            return carry

        lax.fori_loop(0, qi, past_tile, 0)
        off = pl.multiple_of(qi * tk, tk)
        cur_k, cur_v, nk = k_ref[0, pl.ds(off, tk), :], v_ref[0, pl.ds(off, tk), :], tk

    r, c = rel(nk)
    dist = jnp.abs(r - c).astype(_F32)
    bias = jnp.where((c // CHUNK) <= (r // CHUNK), (-slope) * dist, -jnp.inf)
    update(cur_k, cur_v, bias, 0.0)

    o = acc_ref[0] / l_ref[0] - lam * (acc_ref[1] / l_ref[1])
    o = _rms(o, gain_ref[...], DIFF_NORM_EPS) * (1.0 - lam_init)
    o_ref[0] = o.astype(o_ref.dtype)


def _attention(scal, q, k, v, past_k, past_v, gain, lam_init):
    b, t, width = q.shape
    heads = width // A_HEAD
    has_past = past_k is not None
    if has_past:
        tq, tk = t, ATTN_TILE
        assert t <= CHUNK and past_k.shape[1] % tk == 0 and tk % CHUNK == 0
        n_past = past_k.shape[1] // tk
    else:
        tq = tk = min(ATTN_TILE, t)
        assert t % tq == 0 and tq % CHUNK == 0
        n_past = 0
    tile = lambda bi, hi, qi: (bi, qi, hi)
    whole = lambda bi, hi, qi: (bi, 0, hi)
    args = [scal, q, k, v]
    specs = [pl.BlockSpec(memory_space=pltpu.SMEM),
             pl.BlockSpec((1, tq, A_HEAD), tile),
             pl.BlockSpec((1, t, A_HEAD), whole),
             pl.BlockSpec((1, t, A_HEAD), whole)]
    if has_past:
        args += [past_k, past_v]
        specs += [pl.BlockSpec((1, past_k.shape[1], A_HEAD), whole)] * 2
    args.append(gain)
    specs.append(_const_spec(gain.shape))
    kern = functools.partial(_attn_kernel, tq=tq, tk=tk, n_past=n_past, lam_init=lam_init)
    return pl.pallas_call(
        kern,
        grid=(b, heads, t // tq),
        in_specs=specs,
        out_specs=pl.BlockSpec((1, tq, A_HEAD), tile),
        out_shape=jax.ShapeDtypeStruct((b, t, width), _BF),
        scratch_shapes=[pltpu.VMEM((2, tq, LANES), _F32)] * 3,
        compiler_params=_params("parallel", "parallel", "arbitrary"),
        name="attn",
    )(*args)


def _hgrn_kernel(hg_ref, lb_ref, gain_ref, s0_ref, ob_ref, sout_ref, st_ref, row_ref, w_ref, r_ref, *, L, heads):
    t = pl.program_id(1)
    width = heads * B_DK
    hs = range(heads)

    @pl.when(t == 0)
    def _():
        for h in hs:
            st_ref[h] = s0_ref[0, h].T

    rr = lax.broadcasted_iota(jnp.int32, (L, L), 0)
    cc = lax.broadcasted_iota(jnp.int32, (L, L), 1)
    tri = (rr >= cc).astype(_F32)
    ones = jnp.ones((LANES, LANES), _BF)
    blk = BF16_ROWS
    nb = L // blk
    blk_row = lax.broadcasted_iota(jnp.int32, (blk, LANES), 0)

    def operand(g, h):
        return hg_ref[0, :, g * width + h * B_DK:g * width + (h + 1) * B_DK]

    qb = [operand(0, h) for h in hs]
    ib = [operand(2, h) for h in hs]
    f = []
    for h in hs:
        lb = lb_ref[:, h * B_DK:(h + 1) * B_DK]
        f.append(lb + (1.0 - lb) * jax.nn.sigmoid(operand(1, h)))
    kk = [1.0 - f[h] for h in hs]
    cb = [jnp.dot(tri, jnp.log(f[h]), precision=_HI, preferred_element_type=_F32) for h in hs]
    st = [st_ref[h] for h in hs]
    o_inter = [lax.dot_general((qb[h] * jnp.exp(cb[h])).astype(_BF), st[h].astype(_BF), _NT,
                               preferred_element_type=_F32) for h in hs]
    st_new = []
    for h in hs:
        cl = cb[h][L - 1:L, :]
        k_tail = kk[h] * jnp.exp(cl - cb[h])
        st_new.append(st[h] * jnp.exp(cl)
                      + lax.dot_general(ib[h].astype(_BF), k_tail.astype(_BF), _TN, preferred_element_type=_F32))
    for h in hs:
        st_ref[h] = st_new[h]

    for h in hs:
        row_ref[h, 0] = cb[h]
        row_ref[h, 1] = kk[h]
        row_ref[h, 2] = ib[h]
        q_blk = [qb[h][j * blk:(j + 1) * blk] for j in range(nb)]
        cb_blk = [cb[h][j * blk:(j + 1) * blk] for j in range(nb)]
        off = 0
        for s in range(L):
            cbs = row_ref[h, 0, s:s + 1, :]
            ks = row_ref[h, 1, s:s + 1, :]
            for j in range(s // blk, nb):
                d = cb_blk[j] - cbs
                if j == s // blk:
                    d = jnp.where(blk_row >= s - j * blk, d, -jnp.inf)
                w_ref[h, off:off + blk, :] = (q_blk[j] * (ks * jnp.exp(d))).astype(_BF)
                off += blk
        r_ref[h] = jnp.dot(w_ref[h], ones, preferred_element_type=_F32)

    for h in hs:
        o_blk = [o_inter[h][j * blk:(j + 1) * blk] for j in range(nb)]
        off = 0
        for s in range(L):
            vs = row_ref[h, 2, s:s + 1, :]
            for j in range(s // blk, nb):
                o_blk[j] = o_blk[j] + r_ref[h, off:off + blk, :] * vs
                off += blk
        o = jnp.concatenate(o_blk, axis=0)
        ob = _rms(o, gain_ref[...], NORM_EPS) * _silu(operand(3, h))
        ob_ref[0, :, h * B_DK:(h + 1) * B_DK] = ob.astype(ob_ref.dtype)

    @pl.when(t == pl.num_programs(1) - 1)
    def _():
        for h in hs:
            sout_ref[0, h] = st_ref[h].T


def _hgrn(hg, lb, gain, s0):
    b, t, four_w = hg.shape
    width = four_w // 4
    heads = width // B_DK
    L = min(CHUNK, t)
    n_rows = sum(L - BF16_ROWS * (s // BF16_ROWS) for s in range(L))
    kern = functools.partial(_hgrn_kernel, L=L, heads=heads)
    state_spec = pl.BlockSpec((1, heads, B_DK, B_DK), lambda bi, ti: (bi, 0, 0, 0))
    return pl.pallas_call(
        kern,
        grid=(b, t // L),
        in_specs=[pl.BlockSpec((1, L, four_w), lambda bi, ti: (bi, ti, 0)),
                  _const_spec(lb.shape), _const_spec(gain.shape), state_spec],
        out_specs=[pl.BlockSpec((1, L, width), lambda bi, ti: (bi, ti, 0)), state_spec],
        out_shape=[jax.ShapeDtypeStruct((b, t, width), _BF),
                   jax.ShapeDtypeStruct((b, heads, B_DK, B_DK), _F32)],
        scratch_shapes=[pltpu.VMEM((heads, B_DK, B_DK), _F32),
                        pltpu.VMEM((heads, 3, L, LANES), _F32),
                        pltpu.VMEM((heads, n_rows, LANES), _BF),
                        pltpu.VMEM((heads, n_rows, LANES), _F32)],
        compiler_params=_params("parallel", "arbitrary"),
        name="hgrn",
    )(hg, lb, gain, s0)


def _softplus(x):
    return jnp.maximum(x, 0.0) + jnp.log1p(jnp.exp(-jnp.abs(x)))


def _delta_kernel(qkv_ref, gate_ref, ab_ref, cw_ref, alog_ref, dtb_ref, gain_ref, hist_ref, s0_ref,
                  o_ref, sout_ref, cout_ref, xp_ref, s_ref, *, L, heads, nc):
    t = pl.program_id(1)
    width = heads * C_DK
    rows = nc * L
    pad = 8
    lo = pad - (CONV_W - 1)
    hs = range(heads)
    pairs = [(c, h) for c in range(nc) for h in hs]

    @pl.when(t == 0)
    def _():
        xp_ref[lo:pad, :] = hist_ref[0]
        for h in hs:
            s_ref[h] = s0_ref[0, h]

    xp_ref[pad:pad + rows, :] = qkv_ref[0]
    tail = xp_ref[lo + rows:pad + rows, :]
    cout_ref[0] = tail

    def conv(c0):
        y = cw_ref[0:1, c0:c0 + C_DK] * xp_ref[lo:lo + rows, c0:c0 + C_DK]
        for j in range(1, CONV_W):
            y = y + cw_ref[j:j + 1, c0:c0 + C_DK] * xp_ref[lo + j:lo + j + rows, c0:c0 + C_DK]
        return _silu(y)

    rr = lax.broadcasted_iota(jnp.int32, (L, L), 0)
    cc = lax.broadcasted_iota(jnp.int32, (L, L), 1)
    lower = (rr >= cc).astype(_F32)
    upper = (rr <= cc).astype(_F32)
    eye = (rr == cc).astype(_F32)

    ab = ab_ref[0]
    g_all = -jnp.exp(alog_ref[...]) * _softplus(ab + dtb_ref[...])
    beta_all = jax.nn.sigmoid(ab)
    chunk = lambda x, c: x[c * L:(c + 1) * L]
    gam_col = [jnp.dot(lower, chunk(g_all, c), precision=_HI, preferred_element_type=_F32) for c in range(nc)]
    gam_row = [lax.dot_general(chunk(g_all, c), upper, _TN, precision=_HI, preferred_element_type=_F32)
               for c in range(nc)]

    q_all, k_all, v_all = [], [], []
    for h in hs:
        q = conv(h * C_DK)
        k = conv(width + h * C_DK)
        q_all.append(q * lax.rsqrt(jnp.sum(q * q, axis=-1, keepdims=True) + L2_EPS) * (C_DK ** -0.5))
        k_all.append(k * lax.rsqrt(jnp.sum(k * k, axis=-1, keepdims=True) + L2_EPS))
        v_all.append(conv(2 * width + h * C_DK))
    xp_ref[lo:pad, :] = tail

    q, k, v, gc, bc, d_incl, d_strict = {}, {}, {}, {}, {}, {}, {}
    for c, h in pairs:
        p = (c, h)
        q[p], k[p], v[p] = chunk(q_all[h], c), chunk(k_all[h], c), chunk(v_all[h], c)
        gc[p] = gam_col[c][:, h:h + 1]
        bc[p] = chunk(beta_all, c)[:, heads + h:heads + h + 1]
        d_incl[p] = jnp.exp(jnp.where(rr >= cc, gc[p] - gam_row[c][h:h + 1, :], -jnp.inf))
        d_strict[p] = jnp.where(rr > cc, d_incl[p], 0.0)

    qkk = {p: lax.dot_general(jnp.concatenate([q[p], k[p]], axis=0).astype(_BF), k[p].astype(_BF), _NT,
                              preferred_element_type=_F32) for p in pairs}

    a = {p: bc[p] * qkk[p][L:] * d_strict[p] for p in pairs}
    inv = {p: eye - a[p] for p in pairs}
    pw = 2
    power = {p: _mm(a[p], a[p]) for p in pairs} if pw < L else None
    while pw < L:
        nxt = {p: _mm(power[p], power[p]) for p in pairs} if 2 * pw < L else None
        inv = {p: inv[p] + _mm(inv[p], power[p]) for p in pairs}
        power = nxt
        pw *= 2

    eg = {p: jnp.exp(gc[p]) for p in pairs}
    wu = {p: _mm(inv[p], jnp.concatenate([k[p] * (bc[p] * eg[p]), v[p] * bc[p]], axis=1)) for p in pairs}

    state = [s_ref[h] for h in hs]
    for c in range(nc):
        ps = [(c, h) for h in hs]
        ws = {p: _mm(jnp.concatenate([wu[p][:, :C_DK], q[p] * eg[p]], axis=0), state[p[1]]) for p in ps}
        v_new = {p: (wu[p][:, C_DK:] - ws[p][:L]).astype(_BF) for p in ps}
        o = {p: ws[p][L:] + _mm(qkk[p][:L] * d_incl[p], v_new[p]) for p in ps}
        for p in ps:
            h = p[1]
            gl = gc[p][L - 1:L, :]
            state[h] = jnp.exp(gl) * state[h] + lax.dot_general(
                (k[p] * jnp.exp(gl - gc[p])).astype(_BF), v_new[p], _TN, preferred_element_type=_F32)
            gate = gate_ref[0, c * L:(c + 1) * L, h * C_DK:(h + 1) * C_DK]
            on = _rms(o[p], gain_ref[...], NORM_EPS) * _silu(gate)
            o_ref[0, c * L:(c + 1) * L, h * C_DK:(h + 1) * C_DK] = on.astype(o_ref.dtype)
    for h in hs:
        s_ref[h] = state[h]

    @pl.when(t == pl.num_programs(1) - 1)
    def _():
        for h in hs:
            sout_ref[0, h] = s_ref[h]


def _delta(qkv, gate, ab, conv_w, alog, dtb, gain, hist, s0):
    b, t, three_w = qkv.shape
    width = three_w // 3
    heads = width // C_DK
    L = min(CHUNK, t)
    nc = DELTA_CHUNKS if t % (DELTA_CHUNKS * L) == 0 else 1
    rows = nc * L
    kern = functools.partial(_delta_kernel, L=L, heads=heads, nc=nc)
    step = lambda bi, ti: (bi, ti, 0)
    per_b3 = lambda bi, ti: (bi, 0, 0)
    state_spec = pl.BlockSpec((1, heads, C_DK, C_DK), lambda bi, ti: (bi, 0, 0, 0))
    hist_spec = pl.BlockSpec((1, CONV_W - 1, three_w), per_b3)
    return pl.pallas_call(
        kern,
        grid=(b, t // rows),
        in_specs=[pl.BlockSpec((1, rows, three_w), step), pl.BlockSpec((1, rows, width), step),
                  pl.BlockSpec((1, rows, LANES), step), _const_spec(conv_w.shape), _const_spec(alog.shape),
                  _const_spec(dtb.shape), _const_spec(gain.shape), hist_spec, state_spec],
        out_specs=[pl.BlockSpec((1, rows, width), step), state_spec, hist_spec],
        out_shape=[jax.ShapeDtypeStruct((b, t, width), _BF),
                   jax.ShapeDtypeStruct((b, heads, C_DK, C_DK), _F32),
                   jax.ShapeDtypeStruct((b, CONV_W - 1, three_w), _F32)],
        scratch_shapes=[pltpu.VMEM((8 + rows, three_w), _F32),
                        pltpu.VMEM((heads, C_DK, C_DK), _F32)],
        compiler_params=_params("parallel", "arbitrary"),
        name="delta",
    )(qkv, gate, ab, conv_w, alog, dtb, gain, hist, s0)


def _ffn_weights(w_in, w_out):
    d, two_ff = w_in.shape
    d_ff = two_ff // 2
    n_chunks = d_ff // FF_CHUNK
    split = lambda w: w.astype(_BF).reshape(d, n_chunks, FF_CHUNK).transpose(1, 0, 2)
    return split(w_in[:, :d_ff]), split(w_in[:, d_ff:]), w_out.astype(_BF).reshape(n_chunks, FF_CHUNK, d)


def _row(v):
    return v.astype(_F32).reshape(1, -1)


def _pad_lanes(v):
    return jnp.pad(_row(v), ((0, 0), (0, LANES - v.shape[-1])))


def kernel(x_prompt, x_sample, cache_attn_k, cache_attn_v, state_hgrn, state_delta, cache_conv, ffn1_norm, ffn1_w_in, ffn1_w_out, mix_norm, ffn2_norm, ffn2_w_in, ffn2_w_out, even_w_in, diff_lambda, diff_norm, hgrn_lb_logits, hgrn_norm, even_w_out, odd_w_in, conv_w, delta_A_log, delta_dt_bias, delta_norm, odd_w_out, final_norm):
    depth, d_model = ffn1_norm.shape
    a_width = cache_attn_k.shape[3] * cache_attn_k.shape[4] * cache_attn_k.shape[5]
    a_heads = cache_attn_k.shape[3]
    b_heads, b_width = state_hgrn.shape[2], state_hgrn.shape[2] * state_hgrn.shape[3]
    c_heads, c_width = state_delta.shape[2], state_delta.shape[2] * state_delta.shape[3]

    lower_bounds = jnp.cumsum(jax.nn.softmax(hgrn_lb_logits.astype(_F32), axis=0), axis=0)
    slopes = 2.0 ** (-8.0 * jnp.arange(1, a_heads + 1, dtype=_F32) / a_heads)
    ffn1 = [_ffn_weights(ffn1_w_in[l], ffn1_w_out[l]) for l in range(depth)]
    ffn2 = [_ffn_weights(ffn2_w_in[l], ffn2_w_out[l]) for l in range(depth)]

    def run(x, past_k, past_v, hgrn0, delta0, conv0):
        b, t, _ = x.shape
        x = x.reshape(b * t, d_model)
        ks, vs, hs, ds, cs = [], [], [], [], []
        mix = []
        for layer in range(depth):
            j = layer // 2
            x = _ffn(x, mix, _row(ffn1_norm[layer]), *ffn1[layer])
            if layer % 2 == 0:
                groups = ((0, a_width, A_QK_DIM ** -0.5), (a_width, 2 * a_width, 1.0),
                          (2 * a_width, 3 * a_width, 1.0), (3 * a_width, 3 * a_width + 4 * b_width, 1.0))
                q, k, v, hg = _proj(x, _row(mix_norm[layer]), even_w_in[j].astype(_BF), groups,
                                    (_BF, _F32, _F32, _F32))
                lam_init = 0.8 - 0.6 * math.exp(-0.3 * layer)
                lv = diff_lambda[j].astype(_F32)
                lam = jnp.exp(jnp.sum(lv[0] * lv[1])) - jnp.exp(jnp.sum(lv[2] * lv[3])) + lam_init
                scal = jnp.concatenate([lam.reshape(1), slopes])
                seq = lambda a: a.reshape(b, t, -1)
                pk = None if past_k is None else past_k[j].reshape(b, -1, a_width)
                pv = None if past_v is None else past_v[j].reshape(b, -1, a_width)
                oa = _attention(scal, seq(q), seq(k), seq(v), pk, pv, _row(diff_norm[j]), lam_init)
                ob, s_new = _hgrn(seq(hg), _row(lower_bounds[j]), _row(hgrn_norm[j]), hgrn0[j])
                ks.append(k.reshape(b, t, a_heads, 2, A_QK_DIM))
                vs.append(v.reshape(b, t, a_heads, A_HEAD))
                hs.append(s_new)
                w_out = even_w_out[j].astype(_BF)
                mix = [(oa.reshape(b * t, a_width), w_out[:a_width]), (ob.reshape(b * t, b_width), w_out[a_width:])]
            else:
                w_in = odd_w_in[j]
                w_cat = jnp.concatenate([w_in[:, :4 * c_width],
                                         jnp.pad(w_in[:, 4 * c_width:], ((0, 0), (0, LANES - 2 * c_heads)))],
                                        axis=1).astype(_BF)
                groups = ((0, 3 * c_width, 1.0), (3 * c_width, 4 * c_width, 1.0),
                          (4 * c_width, 4 * c_width + LANES, 1.0))
                qkv, gate, ab = _proj(x, _row(mix_norm[layer]), w_cat, groups, (_F32, _F32, _F32))
                seq = lambda a: a.reshape(b, t, -1)
                o, s_new, c_new = _delta(seq(qkv), seq(gate), seq(ab), conv_w[j].astype(_F32),
                                         _pad_lanes(delta_A_log[j]), _pad_lanes(delta_dt_bias[j]),
                                         _row(delta_norm[j]), conv0[j], delta0[j])
                ds.append(s_new)
                cs.append(c_new)
                mix = [(o.reshape(b * t, c_width), odd_w_out[j].astype(_BF))]
            final = _row(final_norm) if layer == depth - 1 else None
            x = _ffn(x, mix, _row(ffn2_norm[layer]), *ffn2[layer], final_gain=final)
            mix = []
        return (x.reshape(b, t, d_model), jnp.stack(ks, axis=0), jnp.stack(vs, axis=0),
                jnp.stack(hs, axis=0), jnp.stack(ds, axis=0), jnp.stack(cs, axis=0))

    bp = x_prompt.shape[0]
    n_even, n_odd = state_hgrn.shape[0], state_delta.shape[0]
    zeros_h = jnp.zeros((n_even, bp) + state_hgrn.shape[2:], _F32)
    zeros_d = jnp.zeros((n_odd, bp) + state_delta.shape[2:], _F32)
    zeros_c = jnp.zeros((n_odd, bp) + cache_conv.shape[2:], x_prompt.dtype)
    y_p, k_p, v_p, h_p, d_p, c_p = run(x_prompt, None, None, zeros_h, zeros_d, zeros_c)
    y_s, k_s, v_s, h_s, d_s, c_s = run(x_sample, cache_attn_k, cache_attn_v, state_hgrn, state_delta, cache_conv)
    return (y_p, y_s, k_p, v_p, h_p, d_p, c_p, k_s, v_s, h_s, d_s, c_s)
```

```python
import functools
import math

import jax
import jax.numpy as jnp
from jax import lax
from jax.experimental import pallas as pl
from jax.experimental.pallas import tpu as pltpu

CHUNK = 64
A_QK_DIM = 64
A_HEAD = 2 * A_QK_DIM
B_DK = 128
C_DK = 128
CONV_W = 4
NORM_EPS = 1e-6
DIFF_NORM_EPS = 1e-5
L2_EPS = 1e-6
LOG2E = math.log2(math.e)

LANES = 128
HGRN_BLOCK = 16
FF_CHUNK = 256
TOKEN_TILE = 512
ATTN_TILE = 512
DELTA_CHUNKS = 2
HGRN_CHUNKS = 4
VMEM_LIMIT = 56 * 1024 * 1024

_BF = jnp.bfloat16
_F32 = jnp.float32
_HI = lax.Precision.HIGHEST
_NT = (((1,), (1,)), ((), ()))
_TN = (((0,), (0,)), ((), ()))


def _params(*sem):
    return pltpu.CompilerParams(dimension_semantics=sem, vmem_limit_bytes=VMEM_LIMIT)


def _rms(x, gain, eps):
    return x * lax.rsqrt(jnp.mean(x * x, axis=-1, keepdims=True) + eps) * gain


def _silu(x):
    return x * jax.nn.sigmoid(x)


def _const_spec(shape):
    nd = len(shape)
    return pl.BlockSpec(shape, lambda *_: (0,) * nd)


def _mm(a, b):
    return jnp.dot(a.astype(_BF), b.astype(_BF), preferred_element_type=_F32)


def _ffn_kernel(*refs, n_mix, final, n_chunks):
    refs = list(refs)
    x_ref = refs.pop(0)
    mix = [(refs.pop(0), refs.pop(0)) for _ in range(n_mix)]
    g_ref, wg_ref, wu_ref, wo_ref = refs[:4]
    refs = refs[4:]
    fg_ref = refs.pop(0) if final else None
    o_ref, acc_ref = refs

    x = x_ref[...]
    for a_ref, w_ref in mix:
        x = x + jnp.dot(a_ref[...], w_ref[...], preferred_element_type=_F32)
    o_ref[...] = x
    h = _rms(x, g_ref[...], NORM_EPS).astype(_BF)
    for c in range(n_chunks):
        gate = jnp.dot(h, wg_ref[c], preferred_element_type=_F32)
        up = jnp.dot(h, wu_ref[c], preferred_element_type=_F32)
        act = (_silu(gate) * up).astype(_BF)
        part = jnp.dot(act, wo_ref[c], preferred_element_type=_F32)
        if c == 0:
            acc_ref[...] = part
        else:
            acc_ref[...] += part
    y = o_ref[...] + 0.5 * acc_ref[...]
    if final:
        y = _rms(y, fg_ref[...], NORM_EPS)
    o_ref[...] = y


def _ffn(x, mix, gain, wg, wu, wo, final_gain=None):
    n, d = x.shape
    tm = min(TOKEN_TILE, n)
    n_chunks = wg.shape[0]
    row = lambda i: (i, 0)
    args, specs = [x], [pl.BlockSpec((tm, d), row)]
    for a, w in mix:
        args += [a, w]
        specs += [pl.BlockSpec((tm, a.shape[1]), row), _const_spec(w.shape)]
    args += [gain, wg, wu, wo]
    specs += [_const_spec(gain.shape), _const_spec(wg.shape), _const_spec(wu.shape), _const_spec(wo.shape)]
    if final_gain is not None:
        args.append(final_gain)
        specs.append(_const_spec(final_gain.shape))
    kern = functools.partial(_ffn_kernel, n_mix=len(mix), final=final_gain is not None, n_chunks=n_chunks)
    return pl.pallas_call(
        kern,
        grid=(n // tm,),
        in_specs=specs,
        out_specs=pl.BlockSpec((tm, d), row),
        out_shape=jax.ShapeDtypeStruct((n, d), _F32),
        scratch_shapes=[pltpu.VMEM((tm, d), _F32)],
        compiler_params=_params("parallel"),
        name="ffn",
    )(*args)


def _proj_kernel(x_ref, g_ref, w_ref, *out_refs, groups):
    h = _rms(x_ref[...], g_ref[...], NORM_EPS).astype(_BF)
    for o_ref, (c0, c1, scale) in zip(out_refs, groups):
        for s in range(c0, c1, 512):
            e = min(s + 512, c1)
            r = jnp.dot(h, w_ref[:, s:e], preferred_element_type=_F32)
            if scale != 1.0:
                r = r * scale
            o_ref[:, s - c0:e - c0] = r.astype(o_ref.dtype)


def _proj(x, gain, w, groups, dtypes):
    n, d = x.shape
    tm = min(TOKEN_TILE, n)
    row = lambda i: (i, 0)
    return pl.pallas_call(
        functools.partial(_proj_kernel, groups=groups),
        grid=(n // tm,),
        in_specs=[pl.BlockSpec((tm, d), row), _const_spec(gain.shape), _const_spec(w.shape)],
        out_specs=[pl.BlockSpec((tm, c1 - c0), row) for c0, c1, _ in groups],
        out_shape=[jax.ShapeDtypeStruct((n, c1 - c0), dt) for (c0, c1, _), dt in zip(groups, dtypes)],
        compiler_params=_params("parallel"),
        name="proj",
    )(x, gain, w)


def _lane_bcast(m, n):
    if n <= LANES:
        return m[:, :n]
    return jnp.concatenate([m] * (n // LANES), axis=1)


def _attn_kernel(scal_ref, q_ref, k_ref, v_ref, *rest, tq, tk, n_past, lam_init):
    if n_past:
        pk_ref, pv_ref, gain_ref, o_ref, m_ref, l_ref, acc_ref, base_ref, diag_ref = rest
    else:
        gain_ref, o_ref, m_ref, l_ref, acc_ref, base_ref, diag_ref = rest
    head = pl.program_id(1)
    qi = pl.program_id(2)
    lam = scal_ref[0]
    slope = scal_ref[1 + head] * LOG2E
    nk = tq if n_past else tk

    @pl.when(qi == 0)
    def _():
        r = lax.broadcasted_iota(jnp.int32, (tq, tk), 0)
        c = lax.broadcasted_iota(jnp.int32, (tq, tk), 1)
        base_ref[...] = (-slope) * (r - c).astype(_F32)
        r = lax.broadcasted_iota(jnp.int32, (tq, nk), 0)
        c = lax.broadcasted_iota(jnp.int32, (tq, nk), 1)
        dist = jnp.abs(r - c).astype(_F32)
        diag_ref[...] = jnp.where((c // CHUNK) <= (r // CHUNK), (-slope) * dist, -jnp.inf)

    q = q_ref[0]
    lane = lax.broadcasted_iota(jnp.int32, q.shape, 1)
    zero = jnp.zeros_like(q)
    q_maps = (jnp.where(lane < A_QK_DIM, q, zero), jnp.where(lane >= A_QK_DIM, q, zero))

    m_ref[...] = jnp.full(m_ref.shape, -jnp.inf, _F32)
    l_ref[...] = jnp.zeros(l_ref.shape, _F32)
    acc_ref[...] = jnp.zeros(acc_ref.shape, _F32)

    def update(kt, vt, bias, shift):
        n = kt.shape[0]
        kt = kt.astype(_BF)
        v_aug = jnp.concatenate([vt.astype(_BF), jnp.ones((n, LANES), _BF)], axis=1)
        s = [lax.dot_general(q_maps[mi], kt, _NT, preferred_element_type=_F32) + bias for mi in range(2)]
        p, alpha = [], []
        for mi in range(2):
            m_prev = m_ref[mi]
            m_next = jnp.maximum(m_prev, jnp.max(s[mi], axis=1, keepdims=True) - shift)
            p.append(jnp.exp2(s[mi] - _lane_bcast(m_next + shift, n)).astype(_BF))
            alpha.append(jnp.exp2(m_prev - m_next))
            m_ref[mi] = m_next
        for mi in range(2):
            pv = jnp.dot(p[mi], v_aug, preferred_element_type=_F32)
            acc_ref[mi] = alpha[mi] * acc_ref[mi] + pv[:, :LANES]
            l_ref[mi] = alpha[mi] * l_ref[mi] + pv[:, LANES:]

    if n_past:
        past_len = n_past * tk

        def past_tile(kj, carry):
            off = pl.multiple_of(kj * tk, tk)
            gap = jnp.asarray(past_len - kj * tk, _F32)
            update(pk_ref[0, pl.ds(off, tk), :], pv_ref[0, pl.ds(off, tk), :], base_ref[...], slope * gap)
            return carry

        lax.fori_loop(0, n_past, past_tile, 0)
        cur_k, cur_v = k_ref[0], v_ref[0]
    else:
        def past_tile(kj, carry):
            off = pl.multiple_of(kj * tk, tk)
            gap = jnp.asarray((qi - kj) * tk, _F32)
            update(k_ref[0, pl.ds(off, tk), :], v_ref[0, pl.ds(off, tk), :], base_ref[...], slope * gap)
            return carry

        lax.fori_loop(0, qi, past_tile, 0)
        off = pl.multiple_of(qi * tk, tk)
        cur_k, cur_v = k_ref[0, pl.ds(off, tk), :], v_ref[0, pl.ds(off, tk), :]

    update(cur_k, cur_v, diag_ref[...], 0.0)

    o = acc_ref[0] / l_ref[0] - lam * (acc_ref[1] / l_ref[1])
    o = _rms(o, gain_ref[...], DIFF_NORM_EPS) * (1.0 - lam_init)
    o_ref[0] = o.astype(o_ref.dtype)


def _attention(scal, q, k, v, past_k, past_v, gain, lam_init):
    b, t, width = q.shape
    heads = width // A_HEAD
    has_past = past_k is not None
    if has_past:
        tq, tk = t, ATTN_TILE
        assert t <= CHUNK and past_k.shape[1] % tk == 0 and tk % CHUNK == 0
        n_past = past_k.shape[1] // tk
    else:
        tq = tk = min(ATTN_TILE, t)
        assert t % tq == 0 and tq % CHUNK == 0
        n_past = 0
    tile = lambda bi, hi, qi: (bi, qi, hi)
    whole = lambda bi, hi, qi: (bi, 0, hi)
    args = [scal, q, k, v]
    specs = [pl.BlockSpec(memory_space=pltpu.SMEM),
             pl.BlockSpec((1, tq, A_HEAD), tile),
             pl.BlockSpec((1, t, A_HEAD), whole),
             pl.BlockSpec((1, t, A_HEAD), whole)]
    if has_past:
        args += [past_k, past_v]
        specs += [pl.BlockSpec((1, past_k.shape[1], A_HEAD), whole)] * 2
    args.append(gain)
    specs.append(_const_spec(gain.shape))
    kern = functools.partial(_attn_kernel, tq=tq, tk=tk, n_past=n_past, lam_init=lam_init)
    return pl.pallas_call(
        kern,
        grid=(b, heads, t // tq),
        in_specs=specs,
        out_specs=pl.BlockSpec((1, tq, A_HEAD), tile),
        out_shape=jax.ShapeDtypeStruct((b, t, width), _BF),
        scratch_shapes=[pltpu.VMEM((2, tq, LANES), _F32)] * 3
        + [pltpu.VMEM((tq, tk), _F32), pltpu.VMEM((tq, tq if has_past else tk), _F32)],
        compiler_params=_params("parallel", "parallel", "arbitrary"),
        name="attn",
    )(*args)


def _hgrn_kernel(hg_ref, lb_ref, gain_ref, s0_ref, ob_ref, sout_ref, st_ref, row_ref, w_ref, r_ref, *, L, heads,
                 nc):
    t = pl.program_id(1)
    width = heads * B_DK
    hs = range(heads)
    us = range(nc * heads)

    @pl.when(t == 0)
    def _():
        for h in hs:
            st_ref[h] = s0_ref[0, h].T

    rr = lax.broadcasted_iota(jnp.int32, (L, L), 0)
    cc = lax.broadcasted_iota(jnp.int32, (L, L), 1)
    tri = (rr >= cc).astype(_F32)
    ones = jnp.ones((LANES, LANES), _BF)
    blk, half = HGRN_BLOCK, HGRN_BLOCK // 2
    nb = L // blk
    blk_row = lax.broadcasted_iota(jnp.int32, (blk, LANES), 0)
    half_row = lax.broadcasted_iota(jnp.int32, (half, LANES), 0)
    rows = lambda x, a, b: x[a:b]

    def operand(g, u):
        c, h = divmod(u, heads)
        return hg_ref[0, c * L:(c + 1) * L, g * width + h * B_DK:g * width + (h + 1) * B_DK]

    qb = [operand(0, u) for u in us]
    ib = [operand(2, u) for u in us]
    f = []
    for u in us:
        lb = lb_ref[:, (u % heads) * B_DK:(u % heads + 1) * B_DK]
        f.append(lb + (1.0 - lb) * jax.nn.sigmoid(operand(1, u)))
    kk = [1.0 - f[u] for u in us]
    cb = [jnp.dot(tri, jnp.log2(f[u]), precision=_HI, preferred_element_type=_F32) for u in us]
    st = [st_ref[h] for h in hs]
    o_acc = [None] * len(us)
    for c in range(nc):
        for h in hs:
            u = c * heads + h
            o_acc[u] = lax.dot_general((qb[u] * jnp.exp2(cb[u])).astype(_BF), st[h].astype(_BF), _NT,
                                       preferred_element_type=_F32)
        for h in hs:
            u = c * heads + h
            cl = cb[u][L - 1:L, :]
            k_tail = kk[u] * jnp.exp2(cl - cb[u])
            st[h] = st[h] * jnp.exp2(cl) + lax.dot_general(ib[u].astype(_BF), k_tail.astype(_BF), _TN,
                                                           preferred_element_type=_F32)
    for h in hs:
        st_ref[h] = st[h]
    for h in us:
        row_ref[h, 0] = cb[h]
        row_ref[h, 1] = kk[h]
        row_ref[h, 2] = ib[h]

    if nb > 1:
        seg_start = [blk * i * (i - 1) // 2 for i in range(1, nb + 1)]
        n_keys = seg_start[-1]
        qr = lax.broadcasted_iota(jnp.int32, (L - blk, n_keys), 0)
        kc = lax.broadcasted_iota(jnp.int32, (L - blk, n_keys), 1)
        seg = jnp.zeros((L - blk, n_keys), jnp.int32)
        for i in range(2, nb):
            seg = jnp.where(kc >= seg_start[i - 1], i - 1, seg)
        own = seg == qr // blk
        scores = []
        for h in us:
            ref = [row_ref[h, 0, blk * i - 1:blk * i, :] for i in range(1, nb)]
            q_t = jnp.concatenate([rows(qb[h], blk * i, blk * (i + 1))
                                   * jnp.exp2(rows(cb[h], blk * i, blk * (i + 1)) - ref[i - 1])
                                   for i in range(1, nb)], axis=0)
            k_t = jnp.concatenate([rows(kk[h], 0, blk * i) * jnp.exp2(ref[i - 1] - rows(cb[h], 0, blk * i))
                                   for i in range(1, nb)], axis=0)
            scores.append(lax.dot_general(q_t.astype(_BF), k_t.astype(_BF), _NT, preferred_element_type=_F32))
        for h in us:
            v_t = jnp.concatenate([rows(ib[h], 0, blk * i) for i in range(1, nb)], axis=0)
            o_off = _mm(jnp.where(own, scores[h], 0.0), v_t)
            o_acc[h] = jnp.concatenate([rows(o_acc[h], 0, blk), rows(o_acc[h], blk, L) + o_off], axis=0)

    for h in us:
        pieces = []
        for i in range(nb):
            cb_i, q_i = rows(cb[h], blk * i, blk * (i + 1)), rows(qb[h], blk * i, blk * (i + 1))
            for sl in range(blk):
                s = blk * i + sl
                cbs = row_ref[h, 0, s:s + 1, :]
                ks = row_ref[h, 1, s:s + 1, :]
                if sl < half:
                    d = jnp.where(blk_row >= sl, cb_i - cbs, -jnp.inf)
                    pieces.append(q_i * (ks * jnp.exp2(d)))
                else:
                    d = jnp.where(half_row >= sl - half, cb_i[half:] - cbs, -jnp.inf)
                    pieces.append(q_i[half:] * (ks * jnp.exp2(d)))
        w_ref[h] = jnp.concatenate(pieces, axis=0).astype(_BF)
        r_ref[h] = jnp.dot(w_ref[h], ones, preferred_element_type=_F32)

    for h in us:
        o8 = [rows(o_acc[h], half * j, half * (j + 1)) for j in range(L // half)]
        off = 0
        for i in range(nb):
            for sl in range(blk):
                vs = row_ref[h, 2, blk * i + sl:blk * i + sl + 1, :]
                if sl < half:
                    o8[2 * i] = o8[2 * i] + r_ref[h, off:off + half, :] * vs
                    off += half
                o8[2 * i + 1] = o8[2 * i + 1] + r_ref[h, off:off + half, :] * vs
                off += half
        o = jnp.concatenate(o8, axis=0)
        ob = _rms(o, gain_ref[...], NORM_EPS) * _silu(operand(3, h))
        c, hd = divmod(h, heads)
        ob_ref[0, c * L:(c + 1) * L, hd * B_DK:(hd + 1) * B_DK] = ob.astype(ob_ref.dtype)

    @pl.when(t == pl.num_programs(1) - 1)
    def _():
        for h in hs:
            sout_ref[0, h] = st_ref[h].T


def _hgrn(hg, lb, gain, s0):
    b, t, four_w = hg.shape
    width = four_w // 4
    heads = width // B_DK
    L = min(CHUNK, t)
    half = HGRN_BLOCK // 2
    n_rows = (L // HGRN_BLOCK) * (half * HGRN_BLOCK + half * half)
    nc = HGRN_CHUNKS if t % (HGRN_CHUNKS * L) == 0 else 1
    kern = functools.partial(_hgrn_kernel, L=L, heads=heads, nc=nc)
    state_spec = pl.BlockSpec((1, heads, B_DK, B_DK), lambda bi, ti: (bi, 0, 0, 0))
    return pl.pallas_call(
        kern,
        grid=(b, t // (nc * L)),
        in_specs=[pl.BlockSpec((1, nc * L, four_w), lambda bi, ti: (bi, ti, 0)),
                  _const_spec(lb.shape), _const_spec(gain.shape), state_spec],
        out_specs=[pl.BlockSpec((1, nc * L, width), lambda bi, ti: (bi, ti, 0)), state_spec],
        out_shape=[jax.ShapeDtypeStruct((b, t, width), _BF),
                   jax.ShapeDtypeStruct((b, heads, B_DK, B_DK), _F32)],
        scratch_shapes=[pltpu.VMEM((heads, B_DK, B_DK), _F32),
                        pltpu.VMEM((nc * heads, 3, L, LANES), _F32),
                        pltpu.VMEM((nc * heads, n_rows, LANES), _BF),
                        pltpu.VMEM((nc * heads, n_rows, LANES), _F32)],
        compiler_params=_params("parallel", "arbitrary"),
        name="hgrn",
    )(hg, lb, gain, s0)


def _softplus(x):
    return jnp.maximum(x, 0.0) + jnp.log1p(jnp.exp(-jnp.abs(x)))


def _delta_kernel(qkv_ref, gate_ref, ab_ref, cw_ref, alog_ref, dtb_ref, gain_ref, hist_ref, s0_ref,
                  o_ref, sout_ref, cout_ref, xp_ref, s_ref, *, L, heads, nc):
    t = pl.program_id(1)
    width = heads * C_DK
    rows = nc * L
    pad = 8
    lo = pad - (CONV_W - 1)
    hs = range(heads)
    pairs = [(c, h) for c in range(nc) for h in hs]

    @pl.when(t == 0)
    def _():
        xp_ref[lo:pad, :] = hist_ref[0]
        for h in hs:
            s_ref[h] = s0_ref[0, h]

    xp_ref[pad:pad + rows, :] = qkv_ref[0]
    tail = xp_ref[lo + rows:pad + rows, :]
    cout_ref[0] = tail

    def conv(c0):
        y = cw_ref[0:1, c0:c0 + C_DK] * xp_ref[lo:lo + rows, c0:c0 + C_DK]
        for j in range(1, CONV_W):
            y = y + cw_ref[j:j + 1, c0:c0 + C_DK] * xp_ref[lo + j:lo + j + rows, c0:c0 + C_DK]
        return _silu(y)

    rr = lax.broadcasted_iota(jnp.int32, (L, L), 0)
    cc = lax.broadcasted_iota(jnp.int32, (L, L), 1)
    lower = (rr >= cc).astype(_F32)
    upper = (rr <= cc).astype(_F32)
    eye = (rr == cc).astype(_F32)

    ab = ab_ref[0]
    g_all = -jnp.exp(alog_ref[...]) * _softplus(ab + dtb_ref[...])
    beta_all = jax.nn.sigmoid(ab)
    chunk = lambda x, c: x[c * L:(c + 1) * L]
    gam_col = [jnp.dot(lower, chunk(g_all, c), precision=_HI, preferred_element_type=_F32) for c in range(nc)]
    gam_row = [lax.dot_general(chunk(g_all, c), upper, _TN, precision=_HI, preferred_element_type=_F32)
               for c in range(nc)]

    q_all, k_all, v_all = [], [], []
    for h in hs:
        q = conv(h * C_DK)
        k = conv(width + h * C_DK)
        q_all.append(q * lax.rsqrt(jnp.sum(q * q, axis=-1, keepdims=True) + L2_EPS) * (C_DK ** -0.5))
        k_all.append(k * lax.rsqrt(jnp.sum(k * k, axis=-1, keepdims=True) + L2_EPS))
        v_all.append(conv(2 * width + h * C_DK))
    xp_ref[lo:pad, :] = tail

    q, k, v, gc, bc, d_incl, d_strict = {}, {}, {}, {}, {}, {}, {}
    for c, h in pairs:
        p = (c, h)
        q[p], k[p], v[p] = chunk(q_all[h], c), chunk(k_all[h], c), chunk(v_all[h], c)
        gc[p] = gam_col[c][:, h:h + 1]
        bc[p] = chunk(beta_all, c)[:, heads + h:heads + h + 1]
        d_incl[p] = jnp.exp(jnp.where(rr >= cc, gc[p] - gam_row[c][h:h + 1, :], -jnp.inf))
        d_strict[p] = jnp.where(rr > cc, d_incl[p], 0.0)

    qkk = {p: lax.dot_general(jnp.concatenate([q[p], k[p]], axis=0).astype(_BF), k[p].astype(_BF), _NT,
                              preferred_element_type=_F32) for p in pairs}

    a = {p: bc[p] * qkk[p][L:] * d_strict[p] for p in pairs}
    inv = {p: eye - a[p] for p in pairs}
    pw = 2
    power = {p: _mm(a[p], a[p]) for p in pairs} if pw < L else None
    while pw < L:
        nxt = {p: _mm(power[p], power[p]) for p in pairs} if 2 * pw < L else None
        inv = {p: inv[p] + _mm(inv[p], power[p]) for p in pairs}
        power = nxt
        pw *= 2

    eg = {p: jnp.exp(gc[p]) for p in pairs}
    wu = {p: _mm(inv[p], jnp.concatenate([k[p] * (bc[p] * eg[p]), v[p] * bc[p]], axis=1)) for p in pairs}

    state = [s_ref[h] for h in hs]
    for c in range(nc):
        ps = [(c, h) for h in hs]
        ws = {p: _mm(jnp.concatenate([wu[p][:, :C_DK], q[p] * eg[p]], axis=0), state[p[1]]) for p in ps}
        v_new = {p: (wu[p][:, C_DK:] - ws[p][:L]).astype(_BF) for p in ps}
        o = {p: ws[p][L:] + _mm(qkk[p][:L] * d_incl[p], v_new[p]) for p in ps}
        for p in ps:
            h = p[1]
            gl = gc[p][L - 1:L, :]
            state[h] = jnp.exp(gl) * state[h] + lax.dot_general(
                (k[p] * jnp.exp(gl - gc[p])).astype(_BF), v_new[p], _TN, preferred_element_type=_F32)
            gate = gate_ref[0, c * L:(c + 1) * L, h * C_DK:(h + 1) * C_DK]
            on = _rms(o[p], gain_ref[...], NORM_EPS) * _silu(gate)
            o_ref[0, c * L:(c + 1) * L, h * C_DK:(h + 1) * C_DK] = on.astype(o_ref.dtype)
    for h in hs:
        s_ref[h] = state[h]

    @pl.when(t == pl.num_programs(1) - 1)
    def _():
        for h in hs:
            sout_ref[0, h] = s_ref[h]


def _delta(qkv, gate, ab, conv_w, alog, dtb, gain, hist, s0):
    b, t, three_w = qkv.shape
    width = three_w // 3
    heads = width // C_DK
    L = min(CHUNK, t)
    nc = DELTA_CHUNKS if t % (DELTA_CHUNKS * L) == 0 else 1
    rows = nc * L
    kern = functools.partial(_delta_kernel, L=L, heads=heads, nc=nc)
    step = lambda bi, ti: (bi, ti, 0)
    per_b3 = lambda bi, ti: (bi, 0, 0)
    state_spec = pl.BlockSpec((1, heads, C_DK, C_DK), lambda bi, ti: (bi, 0, 0, 0))
    hist_spec = pl.BlockSpec((1, CONV_W - 1, three_w), per_b3)
    return pl.pallas_call(
        kern,
        grid=(b, t // rows),
        in_specs=[pl.BlockSpec((1, rows, three_w), step), pl.BlockSpec((1, rows, width), step),
                  pl.BlockSpec((1, rows, LANES), step), _const_spec(conv_w.shape), _const_spec(alog.shape),
                  _const_spec(dtb.shape), _const_spec(gain.shape), hist_spec, state_spec],
        out_specs=[pl.BlockSpec((1, rows, width), step), state_spec, hist_spec],
        out_shape=[jax.ShapeDtypeStruct((b, t, width), _BF),
                   jax.ShapeDtypeStruct((b, heads, C_DK, C_DK), _F32),
                   jax.ShapeDtypeStruct((b, CONV_W - 1, three_w), _F32)],
        scratch_shapes=[pltpu.VMEM((8 + rows, three_w), _F32),
                        pltpu.VMEM((heads, C_DK, C_DK), _F32)],
        compiler_params=_params("parallel", "arbitrary"),
        name="delta",
    )(qkv, gate, ab, conv_w, alog, dtb, gain, hist, s0)


def _ffn_weights(w_in, w_out):
    d, two_ff = w_in.shape
    d_ff = two_ff // 2
    n_chunks = d_ff // FF_CHUNK
    split = lambda w: w.astype(_BF).reshape(d, n_chunks, FF_CHUNK).transpose(1, 0, 2)
    return split(w_in[:, :d_ff]), split(w_in[:, d_ff:]), w_out.astype(_BF).reshape(n_chunks, FF_CHUNK, d)


def _row(v):
    return v.astype(_F32).reshape(1, -1)


def _pad_lanes(v):
    return jnp.pad(_row(v), ((0, 0), (0, LANES - v.shape[-1])))


def kernel(x_prompt, x_sample, cache_attn_k, cache_attn_v, state_hgrn, state_delta, cache_conv, ffn1_norm, ffn1_w_in, ffn1_w_out, mix_norm, ffn2_norm, ffn2_w_in, ffn2_w_out, even_w_in, diff_lambda, diff_norm, hgrn_lb_logits, hgrn_norm, even_w_out, odd_w_in, conv_w, delta_A_log, delta_dt_bias, delta_norm, odd_w_out, final_norm):
    depth, d_model = ffn1_norm.shape
    a_width = cache_attn_k.shape[3] * cache_attn_k.shape[4] * cache_attn_k.shape[5]
    a_heads = cache_attn_k.shape[3]
    b_heads, b_width = state_hgrn.shape[2], state_hgrn.shape[2] * state_hgrn.shape[3]
    c_heads, c_width = state_delta.shape[2], state_delta.shape[2] * state_delta.shape[3]

    lower_bounds = jnp.cumsum(jax.nn.softmax(hgrn_lb_logits.astype(_F32), axis=0), axis=0)
    slopes = 2.0 ** (-8.0 * jnp.arange(1, a_heads + 1, dtype=_F32) / a_heads)
    ffn1 = [_ffn_weights(ffn1_w_in[l], ffn1_w_out[l]) for l in range(depth)]
    ffn2 = [_ffn_weights(ffn2_w_in[l], ffn2_w_out[l]) for l in range(depth)]

    def run(x, past_k, past_v, hgrn0, delta0, conv0):
        b, t, _ = x.shape
        x = x.reshape(b * t, d_model)
        ks, vs, hs, ds, cs = [], [], [], [], []
        mix = []
        for layer in range(depth):
            j = layer // 2
            x = _ffn(x, mix, _row(ffn1_norm[layer]), *ffn1[layer])
            if layer % 2 == 0:
                groups = ((0, a_width, A_QK_DIM ** -0.5 * LOG2E), (a_width, 2 * a_width, 1.0),
                          (2 * a_width, 3 * a_width, 1.0), (3 * a_width, 3 * a_width + 4 * b_width, 1.0))
                q, k, v, hg = _proj(x, _row(mix_norm[layer]), even_w_in[j].astype(_BF), groups,
                                    (_BF, _F32, _F32, _F32))
                lam_init = 0.8 - 0.6 * math.exp(-0.3 * layer)
                lv = diff_lambda[j].astype(_F32)
                lam = jnp.exp(jnp.sum(lv[0] * lv[1])) - jnp.exp(jnp.sum(lv[2] * lv[3])) + lam_init
                scal = jnp.concatenate([lam.reshape(1), slopes])
                seq = lambda a: a.reshape(b, t, -1)
                pk = None if past_k is None else past_k[j].reshape(b, -1, a_width)
                pv = None if past_v is None else past_v[j].reshape(b, -1, a_width)
                oa = _attention(scal, seq(q), seq(k), seq(v), pk, pv, _row(diff_norm[j]), lam_init)
                ob, s_new = _hgrn(seq(hg), _row(lower_bounds[j]), _row(hgrn_norm[j]), hgrn0[j])
                ks.append(k.reshape(b, t, a_heads, 2, A_QK_DIM))
                vs.append(v.reshape(b, t, a_heads, A_HEAD))
                hs.append(s_new)
                w_out = even_w_out[j].astype(_BF)
                mix = [(oa.reshape(b * t, a_width), w_out[:a_width]), (ob.reshape(b * t, b_width), w_out[a_width:])]
            else:
                w_in = odd_w_in[j]
                w_cat = jnp.concatenate([w_in[:, :4 * c_width],
                                         jnp.pad(w_in[:, 4 * c_width:], ((0, 0), (0, LANES - 2 * c_heads)))],
                                        axis=1).astype(_BF)
                groups = ((0, 3 * c_width, 1.0), (3 * c_width, 4 * c_width, 1.0),
                          (4 * c_width, 4 * c_width + LANES, 1.0))
                qkv, gate, ab = _proj(x, _row(mix_norm[layer]), w_cat, groups, (_F32, _F32, _F32))
                seq = lambda a: a.reshape(b, t, -1)
                o, s_new, c_new = _delta(seq(qkv), seq(gate), seq(ab), conv_w[j].astype(_F32),
                                         _pad_lanes(delta_A_log[j]), _pad_lanes(delta_dt_bias[j]),
                                         _row(delta_norm[j]), conv0[j], delta0[j])
                ds.append(s_new)
                cs.append(c_new)
                mix = [(o.reshape(b * t, c_width), odd_w_out[j].astype(_BF))]
            final = _row(final_norm) if layer == depth - 1 else None
            x = _ffn(x, mix, _row(ffn2_norm[layer]), *ffn2[layer], final_gain=final)
            mix = []
        return (x.reshape(b, t, d_model), jnp.stack(ks, axis=0), jnp.stack(vs, axis=0),
                jnp.stack(hs, axis=0), jnp.stack(ds, axis=0), jnp.stack(cs, axis=0))

    bp = x_prompt.shape[0]
    n_even, n_odd = state_hgrn.shape[0], state_delta.shape[0]
    zeros_h = jnp.zeros((n_even, bp) + state_hgrn.shape[2:], _F32)
    zeros_d = jnp.zeros((n_odd, bp) + state_delta.shape[2:], _F32)
    zeros_c = jnp.zeros((n_odd, bp) + cache_conv.shape[2:], x_prompt.dtype)
    y_p, k_p, v_p, h_p, d_p, c_p = run(x_prompt, None, None, zeros_h, zeros_d, zeros_c)
    y_s, k_s, v_s, h_s, d_s, c_s = run(x_sample, cache_attn_k, cache_attn_v, state_hgrn, state_delta, cache_conv)
    return (y_p, y_s, k_p, v_p, h_p, d_p, c_p, k_s, v_s, h_s, d_s, c_s)
```

```python
import functools
import math

import jax
import jax.numpy as jnp
import ml_dtypes
import numpy as np
from jax import lax
from jax.experimental import pallas as pl
from jax.experimental.pallas import tpu as pltpu

CHUNK = 64
A_QK_DIM = 64
A_HEAD = 2 * A_QK_DIM
B_DK = 128
C_DK = 128
CONV_W = 4
NORM_EPS = 1e-6
DIFF_NORM_EPS = 1e-5
L2_EPS = 1e-6
LOG2E = math.log2(math.e)
LOG2E_HI = float(np.float32(LOG2E).astype(ml_dtypes.bfloat16))
LOG2E_LO = float(np.float32(LOG2E - LOG2E_HI).astype(ml_dtypes.bfloat16))

LANES = 128
HGRN_BLOCK = 16
FF_CHUNK = 256
TOKEN_TILE = 512
ATTN_TILE = 512
DELTA_CHUNKS = 2
HGRN_CHUNKS = 4
VMEM_LIMIT = 56 * 1024 * 1024

_BF = jnp.bfloat16
_F32 = jnp.float32
_HI = lax.Precision.HIGHEST
_NT = (((1,), (1,)), ((), ()))
_TN = (((0,), (0,)), ((), ()))


def _params(*sem):
    return pltpu.CompilerParams(dimension_semantics=sem, vmem_limit_bytes=VMEM_LIMIT)


def _rms(x, gain, eps):
    return x * lax.rsqrt(jnp.mean(x * x, axis=-1, keepdims=True) + eps) * gain


def _silu(x):
    return x * jax.nn.sigmoid(x)


def _const_spec(shape):
    nd = len(shape)
    return pl.BlockSpec(shape, lambda *_: (0,) * nd)


def _mm(a, b):
    return jnp.dot(a.astype(_BF), b.astype(_BF), preferred_element_type=_F32)


def _ffn_kernel(*refs, n_mix, final, n_chunks):
    refs = list(refs)
    x_ref = refs.pop(0)
    mix = [(refs.pop(0), refs.pop(0)) for _ in range(n_mix)]
    g_ref, wg_ref, wu_ref, wo_ref = refs[:4]
    refs = refs[4:]
    fg_ref = refs.pop(0) if final else None
    o_ref, acc_ref = refs

    x = x_ref[...]
    for a_ref, w_ref in mix:
        x = x + jnp.dot(a_ref[...], w_ref[...], preferred_element_type=_F32)
    o_ref[...] = x
    h = _rms(x, g_ref[...], NORM_EPS).astype(_BF)
    for c in range(n_chunks):
        gate = jnp.dot(h, wg_ref[c], preferred_element_type=_F32)
        up = jnp.dot(h, wu_ref[c], preferred_element_type=_F32)
        act = (_silu(gate) * up).astype(_BF)
        part = jnp.dot(act, wo_ref[c], preferred_element_type=_F32)
        if c == 0:
            acc_ref[...] = part
        else:
            acc_ref[...] += part
    y = o_ref[...] + 0.5 * acc_ref[...]
    if final:
        y = _rms(y, fg_ref[...], NORM_EPS)
    o_ref[...] = y


def _ffn(x, mix, gain, wg, wu, wo, final_gain=None):
    n, d = x.shape
    tm = min(TOKEN_TILE, n)
    n_chunks = wg.shape[0]
    row = lambda i: (i, 0)
    args, specs = [x], [pl.BlockSpec((tm, d), row)]
    for a, w in mix:
        args += [a, w]
        specs += [pl.BlockSpec((tm, a.shape[1]), row), _const_spec(w.shape)]
    args += [gain, wg, wu, wo]
    specs += [_const_spec(gain.shape), _const_spec(wg.shape), _const_spec(wu.shape), _const_spec(wo.shape)]
    if final_gain is not None:
        args.append(final_gain)
        specs.append(_const_spec(final_gain.shape))
    kern = functools.partial(_ffn_kernel, n_mix=len(mix), final=final_gain is not None, n_chunks=n_chunks)
    return pl.pallas_call(
        kern,
        grid=(n // tm,),
        in_specs=specs,
        out_specs=pl.BlockSpec((tm, d), row),
        out_shape=jax.ShapeDtypeStruct((n, d), _F32),
        scratch_shapes=[pltpu.VMEM((tm, d), _F32)],
        compiler_params=_params("parallel"),
        name="ffn",
    )(*args)


def _proj_kernel(x_ref, g_ref, w_ref, *out_refs, groups):
    h = _rms(x_ref[...], g_ref[...], NORM_EPS).astype(_BF)
    for o_ref, (c0, c1, scale) in zip(out_refs, groups):
        for s in range(c0, c1, 512):
            e = min(s + 512, c1)
            r = jnp.dot(h, w_ref[:, s:e], preferred_element_type=_F32)
            if scale != 1.0:
                r = r * scale
            o_ref[:, s - c0:e - c0] = r.astype(o_ref.dtype)


def _proj(x, gain, w, groups, dtypes):
    n, d = x.shape
    tm = min(TOKEN_TILE, n)
    row = lambda i: (i, 0)
    return pl.pallas_call(
        functools.partial(_proj_kernel, groups=groups),
        grid=(n // tm,),
        in_specs=[pl.BlockSpec((tm, d), row), _const_spec(gain.shape), _const_spec(w.shape)],
        out_specs=[pl.BlockSpec((tm, c1 - c0), row) for c0, c1, _ in groups],
        out_shape=[jax.ShapeDtypeStruct((n, c1 - c0), dt) for (c0, c1, _), dt in zip(groups, dtypes)],
        compiler_params=_params("parallel"),
        name="proj",
    )(x, gain, w)


def _lane_bcast(m, n):
    if n <= LANES:
        return m[:, :n]
    return jnp.concatenate([m] * (n // LANES), axis=1)


def _attn_kernel(scal_ref, q_ref, k_ref, v_ref, *rest, tq, tk, n_past, lam_init):
    if n_past:
        pk_ref, pv_ref, gain_ref, o_ref, m_ref, l_ref, acc_ref, ka_ref, va_ref, diag_ref, sa_ref, sb_ref = rest
    else:
        gain_ref, o_ref, m_ref, l_ref, acc_ref, ka_ref, va_ref, diag_ref, sa_ref, sb_ref = rest
    head = pl.program_id(1)
    qi = pl.program_id(2)
    lam = scal_ref[0]
    slope = scal_ref[1 + head]
    past_len = n_past * tk
    nk = tq if n_past else tk

    def stage(k_rows, v_rows, pos0, dst, n):
        pos = pos0 + lax.broadcasted_iota(jnp.int32, (n, LANES), 0)
        lane = lax.broadcasted_iota(jnp.int32, (n, LANES), 1)
        hi = (slope * CHUNK) * jnp.right_shift(pos, 6).astype(_F32)
        lo = slope * jnp.bitwise_and(pos, CHUNK - 1).astype(_F32)
        pair = jnp.right_shift(lane, 1)
        for mi in range(2):
            a0 = A_QK_DIM * (1 - mi) // 2
            keys = jnp.where(pair == a0, hi, jnp.where(pair == a0 + 1, lo, k_rows))
            ka_ref[mi, pl.ds(dst, n), :] = keys.astype(_BF)
        va_ref[pl.ds(dst, n), :] = jnp.concatenate([v_rows.astype(_BF), jnp.ones((n, LANES), _BF)], axis=1)

    @pl.when(qi == 0)
    def _():
        r = lax.broadcasted_iota(jnp.int32, (tq, nk), 0)
        c = lax.broadcasted_iota(jnp.int32, (tq, nk), 1)
        ahead = (-2.0 * LOG2E * slope) * jnp.maximum(c - r, 0).astype(_F32)
        diag_ref[...] = jnp.where((c // CHUNK) <= (r // CHUNK), ahead, -jnp.inf)

    if n_past:
        for j in range(n_past):
            stage(pk_ref[0, j * tk:(j + 1) * tk, :], pv_ref[0, j * tk:(j + 1) * tk, :], j * tk, j * tk, tk)
        stage(k_ref[0], v_ref[0], past_len, past_len, tq)
        own = past_len
    else:
        own = pl.multiple_of(qi * tk, tk)
        stage(k_ref[0, pl.ds(own, tk), :], v_ref[0, pl.ds(own, tk), :], own, own, tk)

    q = q_ref[0]
    lane = lax.broadcasted_iota(jnp.int32, q.shape, 1)
    log2e = jnp.where(jnp.bitwise_and(lane, 1) == 0, LOG2E_HI, LOG2E_LO).astype(_BF)
    zero = jnp.zeros_like(q)
    quad = jnp.right_shift(lane, 2)
    q_maps = (jnp.where(lane < A_QK_DIM, q, jnp.where(quad == A_QK_DIM // 4, log2e, zero)),
              jnp.where(lane >= A_QK_DIM, q, jnp.where(quad == 0, log2e, zero)))

    m_ref[...] = jnp.full(m_ref.shape, -jnp.inf, _F32)
    l_ref[...] = jnp.zeros(l_ref.shape, _F32)
    acc_ref[...] = jnp.zeros(acc_ref.shape, _F32)

    def scores(off, n, slot, extra):
        for mi in range(2):
            s = lax.dot_general(q_maps[mi], ka_ref[mi, pl.ds(off, n), :], _NT, preferred_element_type=_F32)
            (sb_ref if slot else sa_ref)[mi, :, 0:n] = s if extra is None else s + extra

    def consume(off, n, slot):
        v_aug = va_ref[pl.ds(off, n), :]
        p, alpha = [], []
        for mi in range(2):
            s = (sb_ref if slot else sa_ref)[mi, :, 0:n]
            m_prev = m_ref[mi]
            m_next = jnp.maximum(m_prev, jnp.max(s, axis=1, keepdims=True))
            p.append(jnp.exp2(s - _lane_bcast(m_next, n)).astype(_BF))
            alpha.append(jnp.exp2(m_prev - m_next))
            m_ref[mi] = m_next
        for mi in range(2):
            pv = jnp.dot(p[mi], v_aug, preferred_element_type=_F32)
            acc_ref[mi] = alpha[mi] * acc_ref[mi] + pv[:, :LANES]
            l_ref[mi] = alpha[mi] * l_ref[mi] + pv[:, LANES:]

    A, B = 0, 1
    if n_past:
        scores(own, nk, A, diag_ref[...])
        consume(own, nk, A)

        def past_tile(j, carry):
            off = pl.multiple_of(j * tk, tk)
            scores(off, tk, A, None)
            consume(off, tk, A)
            return carry

        lax.fori_loop(0, n_past, past_tile, 0)
    else:
        tile_off = lambda i: pl.multiple_of(jnp.where(i == 0, qi, i - 1) * tk, tk)
        scores(own, tk, A, diag_ref[...])

        def pair(k, carry):
            scores(tile_off(2 * k + 1), tk, B, None)
            consume(tile_off(2 * k), tk, A)
            scores(tile_off(2 * k + 2), tk, A, None)
            consume(tile_off(2 * k + 1), tk, B)
            return carry

        lax.fori_loop(0, qi // 2, pair, 0)

        @pl.when(qi % 2 == 0)
        def _():
            consume(tile_off(qi), tk, A)

        @pl.when(qi % 2 == 1)
        def _():
            scores(tile_off(qi), tk, B, None)
            consume(tile_off(qi - 1), tk, A)
            consume(tile_off(qi), tk, B)

    o = acc_ref[0] / l_ref[0] - lam * (acc_ref[1] / l_ref[1])
    o = _rms(o, gain_ref[...], DIFF_NORM_EPS) * (1.0 - lam_init)
    o_ref[0] = o.astype(o_ref.dtype)


def _attention(scal, q, k, v, past_k, past_v, gain, lam_init):
    b, t, width = q.shape
    heads = width // A_HEAD
    has_past = past_k is not None
    if has_past:
        tq, tk = t, ATTN_TILE
        assert t <= CHUNK and past_k.shape[1] % tk == 0 and tk % CHUNK == 0
        n_past = past_k.shape[1] // tk
    else:
        tq = tk = min(ATTN_TILE, t)
        assert t % tq == 0 and tq % CHUNK == 0
        n_past = 0
    n_keys = n_past * tk + t
    tile = lambda bi, hi, qi: (bi, qi, hi)
    whole = lambda bi, hi, qi: (bi, 0, hi)
    args = [scal, q, k, v]
    specs = [pl.BlockSpec(memory_space=pltpu.SMEM),
             pl.BlockSpec((1, tq, A_HEAD), tile),
             pl.BlockSpec((1, t, A_HEAD), whole),
             pl.BlockSpec((1, t, A_HEAD), whole)]
    if has_past:
        args += [past_k, past_v]
        specs += [pl.BlockSpec((1, past_k.shape[1], A_HEAD), whole)] * 2
    args.append(gain)
    specs.append(_const_spec(gain.shape))
    kern = functools.partial(_attn_kernel, tq=tq, tk=tk, n_past=n_past, lam_init=lam_init)
    return pl.pallas_call(
        kern,
        grid=(b, heads, t // tq),
        in_specs=specs,
        out_specs=pl.BlockSpec((1, tq, A_HEAD), tile),
        out_shape=jax.ShapeDtypeStruct((b, t, width), _BF),
        scratch_shapes=[pltpu.VMEM((2, tq, LANES), _F32)] * 3
        + [pltpu.VMEM((2, n_keys, LANES), _BF), pltpu.VMEM((n_keys, 2 * LANES), _BF),
           pltpu.VMEM((tq, tq if has_past else tk), _F32),
           pltpu.VMEM((2, tq, tk), _F32), pltpu.VMEM((2, tq, tk), _F32)],
        compiler_params=_params("parallel", "parallel", "arbitrary"),
        name="attn",
    )(*args)


def _hgrn_kernel(hg_ref, lb_ref, gain_ref, s0_ref, ob_ref, sout_ref, st_ref, row_ref, w_ref, r_ref, *, L, heads,
                 nc):
    t = pl.program_id(1)
    width = heads * B_DK
    hs = range(heads)
    us = range(nc * heads)

    @pl.when(t == 0)
    def _():
        for h in hs:
            st_ref[h] = s0_ref[0, h].T

    rr = lax.broadcasted_iota(jnp.int32, (L, L), 0)
    cc = lax.broadcasted_iota(jnp.int32, (L, L), 1)
    tri = (rr >= cc).astype(_F32)
    ones = jnp.ones((LANES, LANES), _BF)
    blk, half = HGRN_BLOCK, HGRN_BLOCK // 2
    nb = L // blk
    blk_row = lax.broadcasted_iota(jnp.int32, (blk, LANES), 0)
    half_row = lax.broadcasted_iota(jnp.int32, (half, LANES), 0)
    rows = lambda x, a, b: x[a:b]

    def operand(g, u):
        c, h = divmod(u, heads)
        return hg_ref[0, c * L:(c + 1) * L, g * width + h * B_DK:g * width + (h + 1) * B_DK]

    qb = [operand(0, u) for u in us]
    ib = [operand(2, u) for u in us]
    f = []
    for u in us:
        lb = lb_ref[:, (u % heads) * B_DK:(u % heads + 1) * B_DK]
        f.append(lb + (1.0 - lb) * jax.nn.sigmoid(operand(1, u)))
    kk = [1.0 - f[u] for u in us]
    cb = [jnp.dot(tri, jnp.log2(f[u]), precision=_HI, preferred_element_type=_F32) for u in us]
    st = [st_ref[h] for h in hs]
    o_acc = [None] * len(us)
    for c in range(nc):
        for h in hs:
            u = c * heads + h
            o_acc[u] = lax.dot_general((qb[u] * jnp.exp2(cb[u])).astype(_BF), st[h].astype(_BF), _NT,
                                       preferred_element_type=_F32)
        for h in hs:
            u = c * heads + h
            cl = cb[u][L - 1:L, :]
            k_tail = kk[u] * jnp.exp2(cl - cb[u])
            st[h] = st[h] * jnp.exp2(cl) + lax.dot_general(ib[u].astype(_BF), k_tail.astype(_BF), _TN,
                                                           preferred_element_type=_F32)
    for h in hs:
        st_ref[h] = st[h]
    for h in us:
        row_ref[h, 0] = cb[h]
        row_ref[h, 1] = kk[h]
        row_ref[h, 2] = ib[h]

    if nb > 1:
        seg_start = [blk * i * (i - 1) // 2 for i in range(1, nb + 1)]
        n_keys = seg_start[-1]
        qr = lax.broadcasted_iota(jnp.int32, (L - blk, n_keys), 0)
        kc = lax.broadcasted_iota(jnp.int32, (L - blk, n_keys), 1)
        seg = jnp.zeros((L - blk, n_keys), jnp.int32)
        for i in range(2, nb):
            seg = jnp.where(kc >= seg_start[i - 1], i - 1, seg)
        own = seg == qr // blk
        scores = []
        for h in us:
            ref = [row_ref[h, 0, blk * i - 1:blk * i, :] for i in range(1, nb)]
            q_t = jnp.concatenate([rows(qb[h], blk * i, blk * (i + 1))
                                   * jnp.exp2(rows(cb[h], blk * i, blk * (i + 1)) - ref[i - 1])
                                   for i in range(1, nb)], axis=0)
            k_t = jnp.concatenate([rows(kk[h], 0, blk * i) * jnp.exp2(ref[i - 1] - rows(cb[h], 0, blk * i))
                                   for i in range(1, nb)], axis=0)
            scores.append(lax.dot_general(q_t.astype(_BF), k_t.astype(_BF), _NT, preferred_element_type=_F32))
        for h in us:
            v_t = jnp.concatenate([rows(ib[h], 0, blk * i) for i in range(1, nb)], axis=0)
            o_off = _mm(jnp.where(own, scores[h], 0.0), v_t)
            o_acc[h] = jnp.concatenate([rows(o_acc[h], 0, blk), rows(o_acc[h], blk, L) + o_off], axis=0)

    for h in us:
        pieces = []
        for i in range(nb):
            cb_i, q_i = rows(cb[h], blk * i, blk * (i + 1)), rows(qb[h], blk * i, blk * (i + 1))
            for sl in range(blk):
                s = blk * i + sl
                cbs = row_ref[h, 0, s:s + 1, :]
                ks = row_ref[h, 1, s:s + 1, :]
                if sl < half:
                    d = jnp.where(blk_row >= sl, cb_i - cbs, -jnp.inf)
                    pieces.append(q_i * (ks * jnp.exp2(d)))
                else:
                    d = jnp.where(half_row >= sl - half, cb_i[half:] - cbs, -jnp.inf)
                    pieces.append(q_i[half:] * (ks * jnp.exp2(d)))
        w_ref[h] = jnp.concatenate(pieces, axis=0).astype(_BF)
        r_ref[h] = jnp.dot(w_ref[h], ones, preferred_element_type=_F32)

    for h in us:
        o8 = [rows(o_acc[h], half * j, half * (j + 1)) for j in range(L // half)]
        off = 0
        for i in range(nb):
            for sl in range(blk):
                vs = row_ref[h, 2, blk * i + sl:blk * i + sl + 1, :]
                if sl < half:
                    o8[2 * i] = o8[2 * i] + r_ref[h, off:off + half, :] * vs
                    off += half
                o8[2 * i + 1] = o8[2 * i + 1] + r_ref[h, off:off + half, :] * vs
                off += half
        o = jnp.concatenate(o8, axis=0)
        ob = _rms(o, gain_ref[...], NORM_EPS) * _silu(operand(3, h))
        c, hd = divmod(h, heads)
        ob_ref[0, c * L:(c + 1) * L, hd * B_DK:(hd + 1) * B_DK] = ob.astype(ob_ref.dtype)

    @pl.when(t == pl.num_programs(1) - 1)
    def _():
        for h in hs:
            sout_ref[0, h] = st_ref[h].T


def _hgrn(hg, lb, gain, s0):
    b, t, four_w = hg.shape
    width = four_w // 4
    heads = width // B_DK
    L = min(CHUNK, t)
    half = HGRN_BLOCK // 2
    n_rows = (L // HGRN_BLOCK) * (half * HGRN_BLOCK + half * half)
    nc = HGRN_CHUNKS if t % (HGRN_CHUNKS * L) == 0 else 1
    kern = functools.partial(_hgrn_kernel, L=L, heads=heads, nc=nc)
    state_spec = pl.BlockSpec((1, heads, B_DK, B_DK), lambda bi, ti: (bi, 0, 0, 0))
    return pl.pallas_call(
        kern,
        grid=(b, t // (nc * L)),
        in_specs=[pl.BlockSpec((1, nc * L, four_w), lambda bi, ti: (bi, ti, 0)),
                  _const_spec(lb.shape), _const_spec(gain.shape), state_spec],
        out_specs=[pl.BlockSpec((1, nc * L, width), lambda bi, ti: (bi, ti, 0)), state_spec],
        out_shape=[jax.ShapeDtypeStruct((b, t, width), _BF),
                   jax.ShapeDtypeStruct((b, heads, B_DK, B_DK), _F32)],
        scratch_shapes=[pltpu.VMEM((heads, B_DK, B_DK), _F32),
                        pltpu.VMEM((nc * heads, 3, L, LANES), _F32),
                        pltpu.VMEM((nc * heads, n_rows, LANES), _BF),
                        pltpu.VMEM((nc * heads, n_rows, LANES), _F32)],
        compiler_params=_params("parallel", "arbitrary"),
        name="hgrn",
    )(hg, lb, gain, s0)


def _softplus(x):
    return jnp.maximum(x, 0.0) + jnp.log1p(jnp.exp(-jnp.abs(x)))


def _delta_kernel(qkv_ref, gate_ref, ab_ref, cw_ref, alog_ref, dtb_ref, gain_ref, hist_ref, s0_ref,
                  o_ref, sout_ref, cout_ref, xp_ref, s_ref, *, L, heads, nc):
    t = pl.program_id(1)
    width = heads * C_DK
    rows = nc * L
    pad = 8
    lo = pad - (CONV_W - 1)
    hs = range(heads)
    pairs = [(c, h) for c in range(nc) for h in hs]

    @pl.when(t == 0)
    def _():
        xp_ref[lo:pad, :] = hist_ref[0]
        for h in hs:
            s_ref[h] = s0_ref[0, h]

    xp_ref[pad:pad + rows, :] = qkv_ref[0]
    tail = xp_ref[lo + rows:pad + rows, :]
    cout_ref[0] = tail

    def conv(c0):
        y = cw_ref[0:1, c0:c0 + C_DK] * xp_ref[lo:lo + rows, c0:c0 + C_DK]
        for j in range(1, CONV_W):
            y = y + cw_ref[j:j + 1, c0:c0 + C_DK] * xp_ref[lo + j:lo + j + rows, c0:c0 + C_DK]
        return _silu(y)

    rr = lax.broadcasted_iota(jnp.int32, (L, L), 0)
    cc = lax.broadcasted_iota(jnp.int32, (L, L), 1)
    lower = (rr >= cc).astype(_F32)
    upper = (rr <= cc).astype(_F32)

    ab = ab_ref[0]
    g_all = -jnp.exp(alog_ref[...]) * _softplus(ab + dtb_ref[...])
    beta_all = jax.nn.sigmoid(ab)
    chunk = lambda x, c: x[c * L:(c + 1) * L]
    gam_col = [jnp.dot(lower, chunk(g_all, c), precision=_HI, preferred_element_type=_F32) for c in range(nc)]
    gam_row = [lax.dot_general(chunk(g_all, c), upper, _TN, precision=_HI, preferred_element_type=_F32)
               for c in range(nc)]

    q_all, k_all, v_all = [], [], []
    for h in hs:
        q = conv(h * C_DK)
        k = conv(width + h * C_DK)
        q_all.append(q * lax.rsqrt(jnp.sum(q * q, axis=-1, keepdims=True) + L2_EPS) * (C_DK ** -0.5))
        k_all.append(k * lax.rsqrt(jnp.sum(k * k, axis=-1, keepdims=True) + L2_EPS))
        v_all.append(conv(2 * width + h * C_DK))
    xp_ref[lo:pad, :] = tail

    q, k, v, gc, bc, d_incl, d_strict = {}, {}, {}, {}, {}, {}, {}
    for c, h in pairs:
        p = (c, h)
        q[p], k[p], v[p] = chunk(q_all[h], c), chunk(k_all[h], c), chunk(v_all[h], c)
        gc[p] = gam_col[c][:, h:h + 1]
        bc[p] = chunk(beta_all, c)[:, heads + h:heads + h + 1]
        d_incl[p] = jnp.exp(jnp.where(rr >= cc, gc[p] - gam_row[c][h:h + 1, :], -jnp.inf))
        d_strict[p] = jnp.where(rr > cc, d_incl[p], 0.0)

    qkk = {p: lax.dot_general(jnp.concatenate([q[p], k[p]], axis=0).astype(_BF), k[p].astype(_BF), _NT,
                              preferred_element_type=_F32) for p in pairs}

    a = {p: bc[p] * qkk[p][L:] * d_strict[p] for p in pairs}
    n_inv = {p: -a[p] for p in pairs}
    pw = 2
    power = {p: _mm(a[p], a[p]) for p in pairs} if pw < L else None
    while pw < L:
        nxt = {p: _mm(power[p], power[p]) for p in pairs} if 2 * pw < L else None
        n_inv = {p: n_inv[p] + power[p] + _mm(n_inv[p], power[p]) for p in pairs}
        power = nxt
        pw *= 2

    eg = {p: jnp.exp(gc[p]) for p in pairs}
    rhs = {p: jnp.concatenate([k[p] * (bc[p] * eg[p]), v[p] * bc[p]], axis=1) for p in pairs}
    wu = {p: rhs[p] + _mm(n_inv[p], rhs[p]) for p in pairs}

    state = [s_ref[h] for h in hs]
    for c in range(nc):
        ps = [(c, h) for h in hs]
        ws = {p: _mm(jnp.concatenate([wu[p][:, :C_DK], q[p] * eg[p]], axis=0), state[p[1]]) for p in ps}
        v_new = {p: (wu[p][:, C_DK:] - ws[p][:L]).astype(_BF) for p in ps}
        o = {p: ws[p][L:] + _mm(qkk[p][:L] * d_incl[p], v_new[p]) for p in ps}
        for p in ps:
            h = p[1]
            gl = gc[p][L - 1:L, :]
            state[h] = jnp.exp(gl) * state[h] + lax.dot_general(
                (k[p] * jnp.exp(gl - gc[p])).astype(_BF), v_new[p], _TN, preferred_element_type=_F32)
            gate = gate_ref[0, c * L:(c + 1) * L, h * C_DK:(h + 1) * C_DK]
            on = _rms(o[p], gain_ref[...], NORM_EPS) * _silu(gate)
            o_ref[0, c * L:(c + 1) * L, h * C_DK:(h + 1) * C_DK] = on.astype(o_ref.dtype)
    for h in hs:
        s_ref[h] = state[h]

    @pl.when(t == pl.num_programs(1) - 1)
    def _():
        for h in hs:
            sout_ref[0, h] = s_ref[h]


def _delta(qkv, gate, ab, conv_w, alog, dtb, gain, hist, s0):
    b, t, three_w = qkv.shape
    width = three_w // 3
    heads = width // C_DK
    L = min(CHUNK, t)
    nc = DELTA_CHUNKS if t % (DELTA_CHUNKS * L) == 0 else 1
    rows = nc * L
    kern = functools.partial(_delta_kernel, L=L, heads=heads, nc=nc)
    step = lambda bi, ti: (bi, ti, 0)
    per_b3 = lambda bi, ti: (bi, 0, 0)
    state_spec = pl.BlockSpec((1, heads, C_DK, C_DK), lambda bi, ti: (bi, 0, 0, 0))
    hist_spec = pl.BlockSpec((1, CONV_W - 1, three_w), per_b3)
    return pl.pallas_call(
        kern,
        grid=(b, t // rows),
        in_specs=[pl.BlockSpec((1, rows, three_w), step), pl.BlockSpec((1, rows, width), step),
                  pl.BlockSpec((1, rows, LANES), step), _const_spec(conv_w.shape), _const_spec(alog.shape),
                  _const_spec(dtb.shape), _const_spec(gain.shape), hist_spec, state_spec],
        out_specs=[pl.BlockSpec((1, rows, width), step), state_spec, hist_spec],
        out_shape=[jax.ShapeDtypeStruct((b, t, width), _BF),
                   jax.ShapeDtypeStruct((b, heads, C_DK, C_DK), _F32),
                   jax.ShapeDtypeStruct((b, CONV_W - 1, three_w), _F32)],
        scratch_shapes=[pltpu.VMEM((8 + rows, three_w), _F32),
                        pltpu.VMEM((heads, C_DK, C_DK), _F32)],
        compiler_params=_params("parallel", "arbitrary"),
        name="delta",
    )(qkv, gate, ab, conv_w, alog, dtb, gain, hist, s0)


def _ffn_weights(w_in, w_out):
    d, two_ff = w_in.shape
    d_ff = two_ff // 2
    n_chunks = d_ff // FF_CHUNK
    split = lambda w: w.astype(_BF).reshape(d, n_chunks, FF_CHUNK).transpose(1, 0, 2)
    return split(w_in[:, :d_ff]), split(w_in[:, d_ff:]), w_out.astype(_BF).reshape(n_chunks, FF_CHUNK, d)


def _row(v):
    return v.astype(_F32).reshape(1, -1)


def _pad_lanes(v):
    return jnp.pad(_row(v), ((0, 0), (0, LANES - v.shape[-1])))


def kernel(x_prompt, x_sample, cache_attn_k, cache_attn_v, state_hgrn, state_delta, cache_conv, ffn1_norm, ffn1_w_in, ffn1_w_out, mix_norm, ffn2_norm, ffn2_w_in, ffn2_w_out, even_w_in, diff_lambda, diff_norm, hgrn_lb_logits, hgrn_norm, even_w_out, odd_w_in, conv_w, delta_A_log, delta_dt_bias, delta_norm, odd_w_out, final_norm):
    depth, d_model = ffn1_norm.shape
    a_width = cache_attn_k.shape[3] * cache_attn_k.shape[4] * cache_attn_k.shape[5]
    a_heads = cache_attn_k.shape[3]
    b_heads, b_width = state_hgrn.shape[2], state_hgrn.shape[2] * state_hgrn.shape[3]
    c_heads, c_width = state_delta.shape[2], state_delta.shape[2] * state_delta.shape[3]

    lower_bounds = jnp.cumsum(jax.nn.softmax(hgrn_lb_logits.astype(_F32), axis=0), axis=0)
    slopes = 2.0 ** (-8.0 * jnp.arange(1, a_heads + 1, dtype=_F32) / a_heads)
    ffn1 = [_ffn_weights(ffn1_w_in[l], ffn1_w_out[l]) for l in range(depth)]
    ffn2 = [_ffn_weights(ffn2_w_in[l], ffn2_w_out[l]) for l in range(depth)]

    def run(x, past_k, past_v, hgrn0, delta0, conv0):
        b, t, _ = x.shape
        x = x.reshape(b * t, d_model)
        ks, vs, hs, ds, cs = [], [], [], [], []
        mix = []
        for layer in range(depth):
            j = layer // 2
            x = _ffn(x, mix, _row(ffn1_norm[layer]), *ffn1[layer])
            if layer % 2 == 0:
                groups = ((0, a_width, A_QK_DIM ** -0.5 * LOG2E), (a_width, 2 * a_width, 1.0),
                          (2 * a_width, 3 * a_width, 1.0), (3 * a_width, 3 * a_width + 4 * b_width, 1.0))
                q, k, v, hg = _proj(x, _row(mix_norm[layer]), even_w_in[j].astype(_BF), groups,
                                    (_BF, _F32, _F32, _F32))
                lam_init = 0.8 - 0.6 * math.exp(-0.3 * layer)
                lv = diff_lambda[j].astype(_F32)
                lam = jnp.exp(jnp.sum(lv[0] * lv[1])) - jnp.exp(jnp.sum(lv[2] * lv[3])) + lam_init
                scal = jnp.concatenate([lam.reshape(1), slopes])
                seq = lambda a: a.reshape(b, t, -1)
                pk = None if past_k is None else past_k[j].reshape(b, -1, a_width)
                pv = None if past_v is None else past_v[j].reshape(b, -1, a_width)
                oa = _attention(scal, seq(q), seq(k), seq(v), pk, pv, _row(diff_norm[j]), lam_init)
                ob, s_new = _hgrn(seq(hg), _row(lower_bounds[j]), _row(hgrn_norm[j]), hgrn0[j])
                ks.append(k.reshape(b, t, a_heads, 2, A_QK_DIM))
                vs.append(v.reshape(b, t, a_heads, A_HEAD))
                hs.append(s_new)
                w_out = even_w_out[j].astype(_BF)
                mix = [(oa.reshape(b * t, a_width), w_out[:a_width]), (ob.reshape(b * t, b_width), w_out[a_width:])]
            else:
                w_in = odd_w_in[j]
                w_cat = jnp.concatenate([w_in[:, :4 * c_width],
                                         jnp.pad(w_in[:, 4 * c_width:], ((0, 0), (0, LANES - 2 * c_heads)))],
                                        axis=1).astype(_BF)
                groups = ((0, 3 * c_width, 1.0), (3 * c_width, 4 * c_width, 1.0),
                          (4 * c_width, 4 * c_width + LANES, 1.0))
                qkv, gate, ab = _proj(x, _row(mix_norm[layer]), w_cat, groups, (_F32, _F32, _F32))
                seq = lambda a: a.reshape(b, t, -1)
                o, s_new, c_new = _delta(seq(qkv), seq(gate), seq(ab), conv_w[j].astype(_F32),
                                         _pad_lanes(delta_A_log[j]), _pad_lanes(delta_dt_bias[j]),
                                         _row(delta_norm[j]), conv0[j], delta0[j])
                ds.append(s_new)
                cs.append(c_new)
                mix = [(o.reshape(b * t, c_width), odd_w_out[j].astype(_BF))]
            final = _row(final_norm) if layer == depth - 1 else None
            x = _ffn(x, mix, _row(ffn2_norm[layer]), *ffn2[layer], final_gain=final)
            mix = []
        return (x.reshape(b, t, d_model), jnp.stack(ks, axis=0), jnp.stack(vs, axis=0),
                jnp.stack(hs, axis=0), jnp.stack(ds, axis=0), jnp.stack(cs, axis=0))

    bp = x_prompt.shape[0]
    n_even, n_odd = state_hgrn.shape[0], state_delta.shape[0]
    zeros_h = jnp.zeros((n_even, bp) + state_hgrn.shape[2:], _F32)
    zeros_d = jnp.zeros((n_odd, bp) + state_delta.shape[2:], _F32)
    zeros_c = jnp.zeros((n_odd, bp) + cache_conv.shape[2:], x_prompt.dtype)
    y_p, k_p, v_p, h_p, d_p, c_p = run(x_prompt, None, None, zeros_h, zeros_d, zeros_c)
    y_s, k_s, v_s, h_s, d_s, c_s = run(x_sample, cache_attn_k, cache_attn_v, state_hgrn, state_delta, cache_conv)
    return (y_p, y_s, k_p, v_p, h_p, d_p, c_p, k_s, v_s, h_s, d_s, c_s)
```

```python
import functools
import math

import jax
import jax.numpy as jnp
import ml_dtypes
import numpy as np
from jax import lax
from jax.experimental import pallas as pl
from jax.experimental.pallas import tpu as pltpu

CHUNK = 64
A_QK_DIM = 64
A_HEAD = 2 * A_QK_DIM
B_DK = 128
C_DK = 128
CONV_W = 4
NORM_EPS = 1e-6
DIFF_NORM_EPS = 1e-5
L2_EPS = 1e-6
LOG2E = math.log2(math.e)
LOG2E_HI = float(np.float32(LOG2E).astype(ml_dtypes.bfloat16))
LOG2E_LO = float(np.float32(LOG2E - LOG2E_HI).astype(ml_dtypes.bfloat16))

LANES = 128
HGRN_BLOCK = 16
FF_CHUNK = 256
TOKEN_TILE = 512
ATTN_TILE = 512
DELTA_CHUNKS = 2
HGRN_CHUNKS = 4
VMEM_LIMIT = 56 * 1024 * 1024

_BF = jnp.bfloat16
_F32 = jnp.float32
_HI = lax.Precision.HIGHEST
_NT = (((1,), (1,)), ((), ()))
_TN = (((0,), (0,)), ((), ()))


def _params(*sem):
    return pltpu.CompilerParams(dimension_semantics=sem, vmem_limit_bytes=VMEM_LIMIT)


def _rms(x, gain, eps):
    return x * lax.rsqrt(jnp.mean(x * x, axis=-1, keepdims=True) + eps) * gain


def _silu(x):
    return x * jax.nn.sigmoid(x)


def _const_spec(shape):
    nd = len(shape)
    return pl.BlockSpec(shape, lambda *_: (0,) * nd)


def _mm(a, b):
    return jnp.dot(a.astype(_BF), b.astype(_BF), preferred_element_type=_F32)


def _ffn_kernel(*refs, n_mix, final, n_chunks):
    refs = list(refs)
    x_ref = refs.pop(0)
    mix = [(refs.pop(0), refs.pop(0)) for _ in range(n_mix)]
    g_ref, wg_ref, wu_ref, wo_ref = refs[:4]
    refs = refs[4:]
    fg_ref = refs.pop(0) if final else None
    o_ref, acc_ref = refs

    x = x_ref[...]
    for a_ref, w_ref in mix:
        x = x + jnp.dot(a_ref[...], w_ref[...], preferred_element_type=_F32)
    o_ref[...] = x
    h = _rms(x, g_ref[...], NORM_EPS).astype(_BF)
    for c in range(n_chunks):
        gate = jnp.dot(h, wg_ref[c], preferred_element_type=_F32)
        up = jnp.dot(h, wu_ref[c], preferred_element_type=_F32)
        act = (_silu(gate) * up).astype(_BF)
        part = jnp.dot(act, wo_ref[c], preferred_element_type=_F32)
        if c == 0:
            acc_ref[...] = part
        else:
            acc_ref[...] += part
    y = o_ref[...] + 0.5 * acc_ref[...]
    if final:
        y = _rms(y, fg_ref[...], NORM_EPS)
    o_ref[...] = y


def _ffn(x, mix, gain, wg, wu, wo, final_gain=None):
    n, d = x.shape
    tm = min(TOKEN_TILE, n)
    n_chunks = wg.shape[0]
    row = lambda i: (i, 0)
    args, specs = [x], [pl.BlockSpec((tm, d), row)]
    for a, w in mix:
        args += [a, w]
        specs += [pl.BlockSpec((tm, a.shape[1]), row), _const_spec(w.shape)]
    args += [gain, wg, wu, wo]
    specs += [_const_spec(gain.shape), _const_spec(wg.shape), _const_spec(wu.shape), _const_spec(wo.shape)]
    if final_gain is not None:
        args.append(final_gain)
        specs.append(_const_spec(final_gain.shape))
    kern = functools.partial(_ffn_kernel, n_mix=len(mix), final=final_gain is not None, n_chunks=n_chunks)
    return pl.pallas_call(
        kern,
        grid=(n // tm,),
        in_specs=specs,
        out_specs=pl.BlockSpec((tm, d), row),
        out_shape=jax.ShapeDtypeStruct((n, d), _F32),
        scratch_shapes=[pltpu.VMEM((tm, d), _F32)],
        compiler_params=_params("parallel"),
        name="ffn",
    )(*args)


def _proj_kernel(x_ref, g_ref, w_ref, *out_refs, groups, views):
    h = _rms(x_ref[...], g_ref[...], NORM_EPS).astype(_BF)
    view_refs = dict(zip(views, out_refs[len(groups):]))
    for gi, (o_ref, (c0, c1, scale)) in enumerate(zip(out_refs, groups)):
        for s in range(c0, c1, 512):
            e = min(s + 512, c1)
            r = jnp.dot(h, w_ref[:, s:e], preferred_element_type=_F32)
            if scale != 1.0:
                r = r * scale
            o_ref[:, s - c0:e - c0] = r.astype(o_ref.dtype)
            if gi in views:
                dims = views[gi]
                width = dims[-1]
                for j in range((e - s) // width):
                    idx = np.unravel_index((s - c0) // width + j, dims[:-1])
                    view_refs[gi][(slice(None),) + tuple(int(i) for i in idx) + (slice(None),)] = (
                        r[:, j * width:(j + 1) * width])


def _proj(x, gain, w, groups, dtypes, views=None):
    n, d = x.shape
    tm = min(TOKEN_TILE, n)
    views = views or {}
    row = lambda i: (i, 0)
    lead = lambda nd: (lambda i: (i,) + (0,) * nd)
    return pl.pallas_call(
        functools.partial(_proj_kernel, groups=groups, views=views),
        grid=(n // tm,),
        in_specs=[pl.BlockSpec((tm, d), row), _const_spec(gain.shape), _const_spec(w.shape)],
        out_specs=[pl.BlockSpec((tm, c1 - c0), row) for c0, c1, _ in groups]
        + [pl.BlockSpec((tm,) + dims, lead(len(dims))) for dims in views.values()],
        out_shape=[jax.ShapeDtypeStruct((n, c1 - c0), dt) for (c0, c1, _), dt in zip(groups, dtypes)]
        + [jax.ShapeDtypeStruct((n,) + dims, _F32) for dims in views.values()],
        compiler_params=_params("parallel"),
        name="proj",
    )(x, gain, w)


def _lane_bcast(m, n):
    if n <= LANES:
        return m[:, :n]
    return jnp.concatenate([m] * (n // LANES), axis=1)


def _attn_kernel(scal_ref, q_ref, k_ref, v_ref, *rest, tq, tk, n_past, lam_init):
    if n_past:
        pk_ref, pv_ref, gain_ref, o_ref, m_ref, l_ref, acc_ref, ka_ref, va_ref, diag_ref, sa_ref, sb_ref = rest
    else:
        gain_ref, o_ref, m_ref, l_ref, acc_ref, ka_ref, va_ref, diag_ref, sa_ref, sb_ref = rest
    head = pl.program_id(1)
    qi = pl.program_id(2)
    lam = scal_ref[0]
    slope = scal_ref[1 + head]
    past_len = n_past * tk
    nk = tq if n_past else tk

    def stage(k_rows, v_rows, pos0, dst, n):
        pos = pos0 + lax.broadcasted_iota(jnp.int32, (n, LANES), 0)
        lane = lax.broadcasted_iota(jnp.int32, (n, LANES), 1)
        hi = (slope * CHUNK) * jnp.right_shift(pos, 6).astype(_F32)
        lo = slope * jnp.bitwise_and(pos, CHUNK - 1).astype(_F32)
        pair = jnp.right_shift(lane, 1)
        for mi in range(2):
            a0 = A_QK_DIM * (1 - mi) // 2
            keys = jnp.where(pair == a0, hi, jnp.where(pair == a0 + 1, lo, k_rows))
            ka_ref[mi, pl.ds(dst, n), :] = keys.astype(_BF)
        va_ref[pl.ds(dst, n), :] = jnp.concatenate([v_rows.astype(_BF), jnp.ones((n, LANES), _BF)], axis=1)

    @pl.when(qi == 0)
    def _():
        r = lax.broadcasted_iota(jnp.int32, (tq, nk), 0)
        c = lax.broadcasted_iota(jnp.int32, (tq, nk), 1)
        ahead = (-2.0 * LOG2E * slope) * jnp.maximum(c - r, 0).astype(_F32)
        diag_ref[...] = jnp.where((c // CHUNK) <= (r // CHUNK), ahead, -jnp.inf)

    if n_past:
        for j in range(n_past):
            stage(pk_ref[0, j * tk:(j + 1) * tk, :], pv_ref[0, j * tk:(j + 1) * tk, :], j * tk, j * tk, tk)
        stage(k_ref[0], v_ref[0], past_len, past_len, tq)
        own = past_len
    else:
        own = pl.multiple_of(qi * tk, tk)
        stage(k_ref[0, pl.ds(own, tk), :], v_ref[0, pl.ds(own, tk), :], own, own, tk)

    q = q_ref[0]
    lane = lax.broadcasted_iota(jnp.int32, q.shape, 1)
    log2e = jnp.where(jnp.bitwise_and(lane, 1) == 0, LOG2E_HI, LOG2E_LO).astype(_BF)
    zero = jnp.zeros_like(q)
    quad = jnp.right_shift(lane, 2)
    q_maps = (jnp.where(lane < A_QK_DIM, q, jnp.where(quad == A_QK_DIM // 4, log2e, zero)),
              jnp.where(lane >= A_QK_DIM, q, jnp.where(quad == 0, log2e, zero)))

    m_ref[...] = jnp.full(m_ref.shape, -jnp.inf, _F32)
    l_ref[...] = jnp.zeros(l_ref.shape, _F32)
    acc_ref[...] = jnp.zeros(acc_ref.shape, _F32)

    def scores(off, n, slot, extra):
        for mi in range(2):
            s = lax.dot_general(q_maps[mi], ka_ref[mi, pl.ds(off, n), :], _NT, preferred_element_type=_F32)
            (sb_ref if slot else sa_ref)[mi, :, 0:n] = s if extra is None else s + extra

    def consume(off, n, slot):
        v_aug = va_ref[pl.ds(off, n), :]
        p, alpha = [], []
        for mi in range(2):
            s = (sb_ref if slot else sa_ref)[mi, :, 0:n]
            m_prev = m_ref[mi]
            m_next = jnp.maximum(m_prev, jnp.max(s, axis=1, keepdims=True))
            p.append(jnp.exp2(s - _lane_bcast(m_next, n)).astype(_BF))
            alpha.append(jnp.exp2(m_prev - m_next))
            m_ref[mi] = m_next
        for mi in range(2):
            pv = jnp.dot(p[mi], v_aug, preferred_element_type=_F32)
            acc_ref[mi] = alpha[mi] * acc_ref[mi] + pv[:, :LANES]
            l_ref[mi] = alpha[mi] * l_ref[mi] + pv[:, LANES:]

    A, B = 0, 1
    if n_past:
        scores(own, nk, A, diag_ref[...])
        consume(own, nk, A)

        def past_tile(j, carry):
            off = pl.multiple_of(j * tk, tk)
            scores(off, tk, A, None)
            consume(off, tk, A)
            return carry

        lax.fori_loop(0, n_past, past_tile, 0)
    else:
        tile_off = lambda i: pl.multiple_of(jnp.where(i == 0, qi, i - 1) * tk, tk)
        scores(own, tk, A, diag_ref[...])

        def pair(k, carry):
            scores(tile_off(2 * k + 1), tk, B, None)
            consume(tile_off(2 * k), tk, A)
            scores(tile_off(2 * k + 2), tk, A, None)
            consume(tile_off(2 * k + 1), tk, B)
            return carry

        lax.fori_loop(0, qi // 2, pair, 0)

        @pl.when(qi % 2 == 0)
        def _():
            consume(tile_off(qi), tk, A)

        @pl.when(qi % 2 == 1)
        def _():
            scores(tile_off(qi), tk, B, None)
            consume(tile_off(qi - 1), tk, A)
            consume(tile_off(qi), tk, B)

    o = acc_ref[0] / l_ref[0] - lam * (acc_ref[1] / l_ref[1])
    o = _rms(o, gain_ref[...], DIFF_NORM_EPS) * (1.0 - lam_init)
    o_ref[0] = o.astype(o_ref.dtype)


def _attention(scal, q, k, v, past_k, past_v, gain, lam_init):
    b, t, width = q.shape
    heads = width // A_HEAD
    has_past = past_k is not None
    if has_past:
        tq, tk = t, ATTN_TILE
        assert t <= CHUNK and past_k.shape[1] % tk == 0 and tk % CHUNK == 0
        n_past = past_k.shape[1] // tk
    else:
        tq = tk = min(ATTN_TILE, t)
        assert t % tq == 0 and tq % CHUNK == 0
        n_past = 0
    n_keys = n_past * tk + t
    tile = lambda bi, hi, qi: (bi, qi, hi)
    whole = lambda bi, hi, qi: (bi, 0, hi)
    args = [scal, q, k, v]
    specs = [pl.BlockSpec(memory_space=pltpu.SMEM),
             pl.BlockSpec((1, tq, A_HEAD), tile),
             pl.BlockSpec((1, t, A_HEAD), whole),
             pl.BlockSpec((1, t, A_HEAD), whole)]
    if has_past:
        args += [past_k, past_v]
        specs += [pl.BlockSpec((1, past_k.shape[1], A_HEAD), whole)] * 2
    args.append(gain)
    specs.append(_const_spec(gain.shape))
    kern = functools.partial(_attn_kernel, tq=tq, tk=tk, n_past=n_past, lam_init=lam_init)
    return pl.pallas_call(
        kern,
        grid=(b, heads, t // tq),
        in_specs=specs,
        out_specs=pl.BlockSpec((1, tq, A_HEAD), tile),
        out_shape=jax.ShapeDtypeStruct((b, t, width), _BF),
        scratch_shapes=[pltpu.VMEM((2, tq, LANES), _F32)] * 3
        + [pltpu.VMEM((2, n_keys, LANES), _BF), pltpu.VMEM((n_keys, 2 * LANES), _BF),
           pltpu.VMEM((tq, tq if has_past else tk), _F32),
           pltpu.VMEM((2, tq, tk), _F32), pltpu.VMEM((2, tq, tk), _F32)],
        compiler_params=_params("parallel", "parallel", "arbitrary"),
        name="attn",
    )(*args)


def _hgrn_kernel(hg_ref, lb_ref, gain_ref, s0_ref, ob_ref, sout_ref, st_ref, row_ref, w_ref, r_ref, *, L, heads,
                 nc):
    t = pl.program_id(1)
    width = heads * B_DK
    hs = range(heads)
    us = range(nc * heads)

    @pl.when(t == 0)
    def _():
        for h in hs:
            st_ref[h] = s0_ref[0, h].T

    rr = lax.broadcasted_iota(jnp.int32, (L, L), 0)
    cc = lax.broadcasted_iota(jnp.int32, (L, L), 1)
    tri = (rr >= cc).astype(_F32)
    ones = jnp.ones((LANES, LANES), _BF)
    blk, half = HGRN_BLOCK, HGRN_BLOCK // 2
    nb = L // blk
    blk_row = lax.broadcasted_iota(jnp.int32, (blk, LANES), 0)
    half_row = lax.broadcasted_iota(jnp.int32, (half, LANES), 0)
    rows = lambda x, a, b: x[a:b]

    def operand(g, u):
        c, h = divmod(u, heads)
        return hg_ref[0, c * L:(c + 1) * L, g * width + h * B_DK:g * width + (h + 1) * B_DK]

    qb = [operand(0, u) for u in us]
    ib = [operand(2, u) for u in us]
    f = []
    for u in us:
        lb = lb_ref[:, (u % heads) * B_DK:(u % heads + 1) * B_DK]
        f.append(lb + (1.0 - lb) * jax.nn.sigmoid(operand(1, u)))
    kk = [1.0 - f[u] for u in us]
    cb = [jnp.dot(tri, jnp.log2(f[u]), precision=_HI, preferred_element_type=_F32) for u in us]
    st = [st_ref[h] for h in hs]
    o_acc = [None] * len(us)
    for c in range(nc):
        for h in hs:
            u = c * heads + h
            o_acc[u] = lax.dot_general((qb[u] * jnp.exp2(cb[u])).astype(_BF), st[h].astype(_BF), _NT,
                                       preferred_element_type=_F32)
        for h in hs:
            u = c * heads + h
            cl = cb[u][L - 1:L, :]
            k_tail = kk[u] * jnp.exp2(cl - cb[u])
            st[h] = st[h] * jnp.exp2(cl) + lax.dot_general(ib[u].astype(_BF), k_tail.astype(_BF), _TN,
                                                           preferred_element_type=_F32)
    for h in hs:
        st_ref[h] = st[h]
    for h in us:
        row_ref[h, 0] = cb[h]
        row_ref[h, 1] = kk[h]
        row_ref[h, 2] = ib[h]

    if nb > 1:
        seg_start = [blk * i * (i - 1) // 2 for i in range(1, nb + 1)]
        n_keys = seg_start[-1]
        qr = lax.broadcasted_iota(jnp.int32, (L - blk, n_keys), 0)
        kc = lax.broadcasted_iota(jnp.int32, (L - blk, n_keys), 1)
        seg = jnp.zeros((L - blk, n_keys), jnp.int32)
        for i in range(2, nb):
            seg = jnp.where(kc >= seg_start[i - 1], i - 1, seg)
        own = seg == qr // blk
        scores = []
        for h in us:
            ref = [row_ref[h, 0, blk * i - 1:blk * i, :] for i in range(1, nb)]
            q_t = jnp.concatenate([rows(qb[h], blk * i, blk * (i + 1))
                                   * jnp.exp2(rows(cb[h], blk * i, blk * (i + 1)) - ref[i - 1])
                                   for i in range(1, nb)], axis=0)
            k_t = jnp.concatenate([rows(kk[h], 0, blk * i) * jnp.exp2(ref[i - 1] - rows(cb[h], 0, blk * i))
                                   for i in range(1, nb)], axis=0)
            scores.append(lax.dot_general(q_t.astype(_BF), k_t.astype(_BF), _NT, preferred_element_type=_F32))
        for h in us:
            v_t = jnp.concatenate([rows(ib[h], 0, blk * i) for i in range(1, nb)], axis=0)
            o_off = _mm(jnp.where(own, scores[h], 0.0), v_t)
            o_acc[h] = jnp.concatenate([rows(o_acc[h], 0, blk), rows(o_acc[h], blk, L) + o_off], axis=0)

    for h in us:
        pieces = []
        for i in range(nb):
            cb_i, q_i = rows(cb[h], blk * i, blk * (i + 1)), rows(qb[h], blk * i, blk * (i + 1))
            for sl in range(blk):
                s = blk * i + sl
                cbs = row_ref[h, 0, s:s + 1, :]
                ks = row_ref[h, 1, s:s + 1, :]
                if sl < half:
                    d = jnp.where(blk_row >= sl, cb_i - cbs, -jnp.inf)
                    pieces.append(q_i * (ks * jnp.exp2(d)))
                else:
                    d = jnp.where(half_row >= sl - half, cb_i[half:] - cbs, -jnp.inf)
                    pieces.append(q_i[half:] * (ks * jnp.exp2(d)))
        w_ref[h] = jnp.concatenate(pieces, axis=0).astype(_BF)
        r_ref[h] = jnp.dot(w_ref[h], ones, preferred_element_type=_F32)

    for h in us:
        o8 = [rows(o_acc[h], half * j, half * (j + 1)) for j in range(L // half)]
        off = 0
        for i in range(nb):
            for sl in range(blk):
                vs = row_ref[h, 2, blk * i + sl:blk * i + sl + 1, :]
                if sl < half:
                    o8[2 * i] = o8[2 * i] + r_ref[h, off:off + half, :] * vs
                    off += half
                o8[2 * i + 1] = o8[2 * i + 1] + r_ref[h, off:off + half, :] * vs
                off += half
        o = jnp.concatenate(o8, axis=0)
        ob = _rms(o, gain_ref[...], NORM_EPS) * _silu(operand(3, h))
        c, hd = divmod(h, heads)
        ob_ref[0, c * L:(c + 1) * L, hd * B_DK:(hd + 1) * B_DK] = ob.astype(ob_ref.dtype)

    @pl.when(t == pl.num_programs(1) - 1)
    def _():
        for h in hs:
            sout_ref[0, h] = st_ref[h].T


def _hgrn(hg, lb, gain, s0):
    b, t, four_w = hg.shape
    width = four_w // 4
    heads = width // B_DK
    L = min(CHUNK, t)
    half = HGRN_BLOCK // 2
    n_rows = (L // HGRN_BLOCK) * (half * HGRN_BLOCK + half * half)
    nc = HGRN_CHUNKS if t % (HGRN_CHUNKS * L) == 0 else 1
    kern = functools.partial(_hgrn_kernel, L=L, heads=heads, nc=nc)
    state_spec = pl.BlockSpec((1, heads, B_DK, B_DK), lambda bi, ti: (bi, 0, 0, 0))
    return pl.pallas_call(
        kern,
        grid=(b, t // (nc * L)),
        in_specs=[pl.BlockSpec((1, nc * L, four_w), lambda bi, ti: (bi, ti, 0)),
                  _const_spec(lb.shape), _const_spec(gain.shape), state_spec],
        out_specs=[pl.BlockSpec((1, nc * L, width), lambda bi, ti: (bi, ti, 0)), state_spec],
        out_shape=[jax.ShapeDtypeStruct((b, t, width), _BF),
                   jax.ShapeDtypeStruct((b, heads, B_DK, B_DK), _F32)],
        scratch_shapes=[pltpu.VMEM((heads, B_DK, B_DK), _F32),
                        pltpu.VMEM((nc * heads, 3, L, LANES), _F32),
                        pltpu.VMEM((nc * heads, n_rows, LANES), _BF),
                        pltpu.VMEM((nc * heads, n_rows, LANES), _F32)],
        compiler_params=_params("parallel", "arbitrary"),
        name="hgrn",
    )(hg, lb, gain, s0)


def _softplus(x):
    return jnp.maximum(x, 0.0) + jnp.log1p(jnp.exp(-jnp.abs(x)))


def _delta_kernel(q_ref, k_ref, v_ref, gate_ref, ab_ref, alog_ref, dtb_ref, gain_ref, s0_ref,
                  o_ref, sout_ref, s_ref, *, L, heads, nc):
    t = pl.program_id(1)
    hs = range(heads)
    pairs = [(c, h) for c in range(nc) for h in hs]

    @pl.when(t == 0)
    def _():
        for h in hs:
            s_ref[h] = s0_ref[0, h]

    rr = lax.broadcasted_iota(jnp.int32, (L, L), 0)
    cc = lax.broadcasted_iota(jnp.int32, (L, L), 1)
    lower = (rr >= cc).astype(_F32)
    upper = (rr <= cc).astype(_F32)

    ab = ab_ref[0]
    g_all = -jnp.exp(alog_ref[...]) * _softplus(ab + dtb_ref[...])
    beta_all = jax.nn.sigmoid(ab)
    chunk = lambda x, c: x[c * L:(c + 1) * L]
    gam_col = [jnp.dot(lower, chunk(g_all, c), precision=_HI, preferred_element_type=_F32) for c in range(nc)]
    gam_row = [lax.dot_general(chunk(g_all, c), upper, _TN, precision=_HI, preferred_element_type=_F32)
               for c in range(nc)]

    q, k, v, gc, bc, d_incl, d_strict = {}, {}, {}, {}, {}, {}, {}
    for c, h in pairs:
        p = (c, h)
        q[p], k[p], v[p] = (r[0, c * L:(c + 1) * L, h * C_DK:(h + 1) * C_DK] for r in (q_ref, k_ref, v_ref))
        gc[p] = gam_col[c][:, h:h + 1]
        bc[p] = chunk(beta_all, c)[:, heads + h:heads + h + 1]
        d_incl[p] = jnp.exp(jnp.where(rr >= cc, gc[p] - gam_row[c][h:h + 1, :], -jnp.inf))
        d_strict[p] = jnp.where(rr > cc, d_incl[p], 0.0)

    qkk = {p: lax.dot_general(jnp.concatenate([q[p], k[p]], axis=0).astype(_BF), k[p].astype(_BF), _NT,
                              preferred_element_type=_F32) for p in pairs}

    a = {p: bc[p] * qkk[p][L:] * d_strict[p] for p in pairs}
    n_inv = {p: -a[p] for p in pairs}
    pw = 2
    power = {p: _mm(a[p], a[p]) for p in pairs} if pw < L else None
    while pw < L:
        nxt = {p: _mm(power[p], power[p]) for p in pairs} if 2 * pw < L else None
        n_inv = {p: n_inv[p] + power[p] + _mm(n_inv[p], power[p]) for p in pairs}
        power = nxt
        pw *= 2

    eg = {p: jnp.exp(gc[p]) for p in pairs}
    rhs = {p: jnp.concatenate([k[p] * (bc[p] * eg[p]), v[p] * bc[p]], axis=1) for p in pairs}
    wu = {p: rhs[p] + _mm(n_inv[p], rhs[p]) for p in pairs}

    state = [s_ref[h] for h in hs]
    for c in range(nc):
        ps = [(c, h) for h in hs]
        ws = {p: _mm(jnp.concatenate([wu[p][:, :C_DK], q[p] * eg[p]], axis=0), state[p[1]]) for p in ps}
        v_new = {p: (wu[p][:, C_DK:] - ws[p][:L]).astype(_BF) for p in ps}
        o = {p: ws[p][L:] + _mm(qkk[p][:L] * d_incl[p], v_new[p]) for p in ps}
        for p in ps:
            h = p[1]
            gl = gc[p][L - 1:L, :]
            state[h] = jnp.exp(gl) * state[h] + lax.dot_general(
                (k[p] * jnp.exp(gl - gc[p])).astype(_BF), v_new[p], _TN, preferred_element_type=_F32)
            gate = gate_ref[0, c * L:(c + 1) * L, h * C_DK:(h + 1) * C_DK]
            on = _rms(o[p], gain_ref[...], NORM_EPS) * _silu(gate)
            o_ref[0, c * L:(c + 1) * L, h * C_DK:(h + 1) * C_DK] = on.astype(o_ref.dtype)
    for h in hs:
        s_ref[h] = state[h]

    @pl.when(t == pl.num_programs(1) - 1)
    def _():
        for h in hs:
            sout_ref[0, h] = s_ref[h]


def _delta(q, k, v, gate, ab, alog, dtb, gain, s0):
    b, t, width = q.shape
    heads = width // C_DK
    L = min(CHUNK, t)
    nc = DELTA_CHUNKS if t % (DELTA_CHUNKS * L) == 0 else 1
    rows = nc * L
    kern = functools.partial(_delta_kernel, L=L, heads=heads, nc=nc)
    step = lambda bi, ti: (bi, ti, 0)
    wide = pl.BlockSpec((1, rows, width), step)
    state_spec = pl.BlockSpec((1, heads, C_DK, C_DK), lambda bi, ti: (bi, 0, 0, 0))
    return pl.pallas_call(
        kern,
        grid=(b, t // rows),
        in_specs=[wide, wide, wide, wide, pl.BlockSpec((1, rows, LANES), step), _const_spec(alog.shape),
                  _const_spec(dtb.shape), _const_spec(gain.shape), state_spec],
        out_specs=[wide, state_spec],
        out_shape=[jax.ShapeDtypeStruct((b, t, width), _BF),
                   jax.ShapeDtypeStruct((b, heads, C_DK, C_DK), _F32)],
        scratch_shapes=[pltpu.VMEM((heads, C_DK, C_DK), _F32)],
        compiler_params=_params("parallel", "arbitrary"),
        name="delta",
    )(q, k, v, gate, ab, alog, dtb, gain, s0)


def _conv_proj_kernel(x_ref, halo_ref, g_ref, w_ref, cw_ref, hist_ref,
                      q_ref, k_ref, v_ref, gate_ref, ab_ref, cout_ref, pre_a, pre_b, *, width, tiles_per_seq):
    i = pl.program_id(0)
    tm = x_ref.shape[0]
    pad = halo_ref.shape[0]
    lo = pad - (CONV_W - 1)
    first = (i % tiles_per_seq) == 0
    h_all = _rms(jnp.concatenate([halo_ref[...], x_ref[...]], axis=0), g_ref[...], NORM_EPS).astype(_BF)
    step = 4 * C_DK
    n_chunk = 0
    for g, o_ref in enumerate((q_ref, k_ref, v_ref)):
        for s in range(0, width, step):
            c0 = g * width + s
            pre = pre_b if n_chunk % 2 else pre_a
            n_chunk += 1
            pre[...] = jnp.dot(h_all, w_ref[:, c0:c0 + step], preferred_element_type=_F32)
            pre[lo:pad, :] = jnp.where(first, hist_ref[0, :, c0:c0 + step], pre[lo:pad, :])
            cout_ref[0, :, c0:c0 + step] = pre[tm + lo:tm + pad, :]
            rows = pre[...]
            y = cw_ref[0:1, c0:c0 + step] * rows
            for j in range(1, CONV_W):
                y = cw_ref[j:j + 1, c0:c0 + step] * rows + pltpu.roll(y, 1, axis=0)
            y = _silu(y[pad:])
            for hh in range(step // C_DK):
                yh = y[:, hh * C_DK:(hh + 1) * C_DK]
                if g < 2:
                    yh = yh * lax.rsqrt(jnp.sum(yh * yh, axis=-1, keepdims=True) + L2_EPS)
                    if g == 0:
                        yh = yh * (C_DK ** -0.5)
                o_ref[:, s + hh * C_DK:s + (hh + 1) * C_DK] = yh
    h_main = h_all[pad:]
    for s in range(0, width, step):
        gate_ref[:, s:s + step] = jnp.dot(h_main, w_ref[:, 3 * width + s:3 * width + s + step],
                                          preferred_element_type=_F32)
    ab_ref[...] = jnp.dot(h_main, w_ref[:, 4 * width:4 * width + LANES], preferred_element_type=_F32)


def _conv_proj(x, gain, w, conv_w, hist, seq_len):
    n, d = x.shape
    width = conv_w.shape[1] // 3
    tm = min(TOKEN_TILE, seq_len)
    tiles_per_seq = seq_len // tm
    halo = 8
    row = lambda i: (i, 0)
    wide = pl.BlockSpec((tm, width), row)
    hist_spec = pl.BlockSpec((1, CONV_W - 1, 3 * width), lambda i: (i // tiles_per_seq, 0, 0))
    kern = functools.partial(_conv_proj_kernel, width=width, tiles_per_seq=tiles_per_seq)
    return pl.pallas_call(
        kern,
        grid=(n // tm,),
        in_specs=[pl.BlockSpec((tm, d), row),
                  pl.BlockSpec((halo, d), lambda i: (jnp.maximum(i * (tm // halo) - 1, 0), 0)),
                  _const_spec(gain.shape), _const_spec(w.shape), _const_spec(conv_w.shape), hist_spec],
        out_specs=[wide, wide, wide, wide, pl.BlockSpec((tm, LANES), row), hist_spec],
        out_shape=[jax.ShapeDtypeStruct((n, width), _F32)] * 4
        + [jax.ShapeDtypeStruct((n, LANES), _F32), jax.ShapeDtypeStruct(hist.shape, _F32)],
        scratch_shapes=[pltpu.VMEM((tm + halo, 4 * C_DK), _F32)] * 2,
        compiler_params=_params("arbitrary"),
        name="conv_proj",
    )(x, x, gain, w, conv_w, hist)


def _ffn_weights(w_in, w_out):
    d, two_ff = w_in.shape
    d_ff = two_ff // 2
    n_chunks = d_ff // FF_CHUNK
    split = lambda w: w.astype(_BF).reshape(d, n_chunks, FF_CHUNK).transpose(1, 0, 2)
    return split(w_in[:, :d_ff]), split(w_in[:, d_ff:]), w_out.astype(_BF).reshape(n_chunks, FF_CHUNK, d)


def _row(v):
    return v.astype(_F32).reshape(1, -1)


def _pad_lanes(v):
    return jnp.pad(_row(v), ((0, 0), (0, LANES - v.shape[-1])))


def kernel(x_prompt, x_sample, cache_attn_k, cache_attn_v, state_hgrn, state_delta, cache_conv, ffn1_norm, ffn1_w_in, ffn1_w_out, mix_norm, ffn2_norm, ffn2_w_in, ffn2_w_out, even_w_in, diff_lambda, diff_norm, hgrn_lb_logits, hgrn_norm, even_w_out, odd_w_in, conv_w, delta_A_log, delta_dt_bias, delta_norm, odd_w_out, final_norm):
    depth, d_model = ffn1_norm.shape
    a_width = cache_attn_k.shape[3] * cache_attn_k.shape[4] * cache_attn_k.shape[5]
    a_heads = cache_attn_k.shape[3]
    b_heads, b_width = state_hgrn.shape[2], state_hgrn.shape[2] * state_hgrn.shape[3]
    c_heads, c_width = state_delta.shape[2], state_delta.shape[2] * state_delta.shape[3]

    lower_bounds = jnp.cumsum(jax.nn.softmax(hgrn_lb_logits.astype(_F32), axis=0), axis=0)
    slopes = 2.0 ** (-8.0 * jnp.arange(1, a_heads + 1, dtype=_F32) / a_heads)
    ffn1 = [_ffn_weights(ffn1_w_in[l], ffn1_w_out[l]) for l in range(depth)]
    ffn2 = [_ffn_weights(ffn2_w_in[l], ffn2_w_out[l]) for l in range(depth)]

    def run(x, past_k, past_v, hgrn0, delta0, conv0):
        b, t, _ = x.shape
        x = x.reshape(b * t, d_model)
        ks, vs, hs, ds, cs = [], [], [], [], []
        mix = []
        for layer in range(depth):
            j = layer // 2
            x = _ffn(x, mix, _row(ffn1_norm[layer]), *ffn1[layer])
            if layer % 2 == 0:
                groups = ((0, a_width, A_QK_DIM ** -0.5 * LOG2E), (a_width, 2 * a_width, 1.0),
                          (2 * a_width, 3 * a_width, 1.0), (3 * a_width, 3 * a_width + 4 * b_width, 1.0))
                q, k, v, hg, k_out, v_out = _proj(
                    x, _row(mix_norm[layer]), even_w_in[j].astype(_BF), groups, (_BF, _F32, _F32, _F32),
                    views={1: (a_heads, 2, A_QK_DIM), 2: (a_heads, A_HEAD)})
                lam_init = 0.8 - 0.6 * math.exp(-0.3 * layer)
                lv = diff_lambda[j].astype(_F32)
                lam = jnp.exp(jnp.sum(lv[0] * lv[1])) - jnp.exp(jnp.sum(lv[2] * lv[3])) + lam_init
                scal = jnp.concatenate([lam.reshape(1), slopes])
                seq = lambda a: a.reshape(b, t, -1)
                pk = None if past_k is None else past_k[j].reshape(b, -1, a_width)
                pv = None if past_v is None else past_v[j].reshape(b, -1, a_width)
                oa = _attention(scal, seq(q), seq(k), seq(v), pk, pv, _row(diff_norm[j]), lam_init)
                ob, s_new = _hgrn(seq(hg), _row(lower_bounds[j]), _row(hgrn_norm[j]), hgrn0[j])
                ks.append(k_out.reshape(b, t, a_heads, 2, A_QK_DIM))
                vs.append(v_out.reshape(b, t, a_heads, A_HEAD))
                hs.append(s_new)
                w_out = even_w_out[j].astype(_BF)
                mix = [(oa.reshape(b * t, a_width), w_out[:a_width]), (ob.reshape(b * t, b_width), w_out[a_width:])]
            else:
                w_in = odd_w_in[j]
                w_cat = jnp.concatenate([w_in[:, :4 * c_width],
                                         jnp.pad(w_in[:, 4 * c_width:], ((0, 0), (0, LANES - 2 * c_heads)))],
                                        axis=1).astype(_BF)
                q, k, v, gate, ab, c_new = _conv_proj(x, _row(mix_norm[layer]), w_cat, conv_w[j].astype(_F32),
                                                      conv0[j].astype(_F32), t)
                seq = lambda a: a.reshape(b, t, -1)
                o, s_new = _delta(seq(q), seq(k), seq(v), seq(gate), seq(ab), _pad_lanes(delta_A_log[j]),
                                  _pad_lanes(delta_dt_bias[j]), _row(delta_norm[j]), delta0[j])
                ds.append(s_new)
                cs.append(c_new)
                mix = [(o.reshape(b * t, c_width), odd_w_out[j].astype(_BF))]
            final = _row(final_norm) if layer == depth - 1 else None
            x = _ffn(x, mix, _row(ffn2_norm[layer]), *ffn2[layer], final_gain=final)
            mix = []
        return (x.reshape(b, t, d_model), jnp.stack(ks, axis=0), jnp.stack(vs, axis=0),
                jnp.stack(hs, axis=0), jnp.stack(ds, axis=0), jnp.stack(cs, axis=0))

    bp = x_prompt.shape[0]
    n_even, n_odd = state_hgrn.shape[0], state_delta.shape[0]
    zeros_h = jnp.zeros((n_even, bp) + state_hgrn.shape[2:], _F32)
    zeros_d = jnp.zeros((n_odd, bp) + state_delta.shape[2:], _F32)
    zeros_c = jnp.zeros((n_odd, bp) + cache_conv.shape[2:], x_prompt.dtype)
    y_p, k_p, v_p, h_p, d_p, c_p = run(x_prompt, None, None, zeros_h, zeros_d, zeros_c)
    y_s, k_s, v_s, h_s, d_s, c_s = run(x_sample, cache_attn_k, cache_attn_v, state_hgrn, state_delta, cache_conv)
    return (y_p, y_s, k_p, v_p, h_p, d_p, c_p, k_s, v_s, h_s, d_s, c_s)
```

```python
import functools
import math

import jax
import jax.numpy as jnp
import ml_dtypes
import numpy as np
from jax import lax
from jax.experimental import pallas as pl
from jax.experimental.pallas import tpu as pltpu

CHUNK = 64
A_QK_DIM = 64
A_HEAD = 2 * A_QK_DIM
B_DK = 128
C_DK = 128
CONV_W = 4
NORM_EPS = 1e-6
DIFF_NORM_EPS = 1e-5
L2_EPS = 1e-6
LOG2E = math.log2(math.e)
LOG2E_HI = float(np.float32(LOG2E).astype(ml_dtypes.bfloat16))
LOG2E_LO = float(np.float32(LOG2E - LOG2E_HI).astype(ml_dtypes.bfloat16))

LANES = 128
HGRN_BLOCK = 16
FF_CHUNK = 256
TOKEN_TILE = 512
ATTN_TILE = 512
DELTA_CHUNKS = 4
HGRN_CHUNKS = 4
VMEM_LIMIT = 56 * 1024 * 1024

_BF = jnp.bfloat16
_F32 = jnp.float32
_HI = lax.Precision.HIGHEST
_NT = (((1,), (1,)), ((), ()))
_TN = (((0,), (0,)), ((), ()))


def _params(*sem):
    return pltpu.CompilerParams(dimension_semantics=sem, vmem_limit_bytes=VMEM_LIMIT)


def _rms(x, gain, eps):
    return x * lax.rsqrt(jnp.mean(x * x, axis=-1, keepdims=True) + eps) * gain


def _silu(x):
    return x * jax.nn.sigmoid(x)


def _const_spec(shape):
    nd = len(shape)
    return pl.BlockSpec(shape, lambda *_: (0,) * nd)


def _mm(a, b):
    return jnp.dot(a.astype(_BF), b.astype(_BF), preferred_element_type=_F32)


def _ffn_kernel(*refs, n_mix, final, n_chunks):
    refs = list(refs)
    x_ref = refs.pop(0)
    mix = [(refs.pop(0), refs.pop(0)) for _ in range(n_mix)]
    g_ref, wg_ref, wu_ref, wo_ref = refs[:4]
    refs = refs[4:]
    fg_ref = refs.pop(0) if final else None
    o_ref, acc_ref = refs

    x = x_ref[...]
    for a_ref, w_ref in mix:
        x = x + jnp.dot(a_ref[...], w_ref[...], preferred_element_type=_F32)
    o_ref[...] = x
    h = _rms(x, g_ref[...], NORM_EPS).astype(_BF)
    for c in range(n_chunks):
        gate = jnp.dot(h, wg_ref[c], preferred_element_type=_F32)
        up = jnp.dot(h, wu_ref[c], preferred_element_type=_F32)
        act = (_silu(gate) * up).astype(_BF)
        part = jnp.dot(act, wo_ref[c], preferred_element_type=_F32)
        if c == 0:
            acc_ref[...] = part
        else:
            acc_ref[...] += part
    y = o_ref[...] + 0.5 * acc_ref[...]
    if final:
        y = _rms(y, fg_ref[...], NORM_EPS)
    o_ref[...] = y


def _ffn(x, mix, gain, wg, wu, wo, final_gain=None):
    n, d = x.shape
    tm = min(TOKEN_TILE, n)
    n_chunks = wg.shape[0]
    row = lambda i: (i, 0)
    args, specs = [x], [pl.BlockSpec((tm, d), row)]
    for a, w in mix:
        args += [a, w]
        specs += [pl.BlockSpec((tm, a.shape[1]), row), _const_spec(w.shape)]
    args += [gain, wg, wu, wo]
    specs += [_const_spec(gain.shape), _const_spec(wg.shape), _const_spec(wu.shape), _const_spec(wo.shape)]
    if final_gain is not None:
        args.append(final_gain)
        specs.append(_const_spec(final_gain.shape))
    kern = functools.partial(_ffn_kernel, n_mix=len(mix), final=final_gain is not None, n_chunks=n_chunks)
    return pl.pallas_call(
        kern,
        grid=(n // tm,),
        in_specs=specs,
        out_specs=pl.BlockSpec((tm, d), row),
        out_shape=jax.ShapeDtypeStruct((n, d), _F32),
        scratch_shapes=[pltpu.VMEM((tm, d), _F32)],
        compiler_params=_params("parallel"),
        name="ffn",
    )(*args)


def _proj_kernel(x_ref, g_ref, w_ref, *out_refs, groups, views):
    h = _rms(x_ref[...], g_ref[...], NORM_EPS).astype(_BF)
    view_refs = dict(zip(views, out_refs[len(groups):]))
    for gi, (o_ref, (c0, c1, scale)) in enumerate(zip(out_refs, groups)):
        for s in range(c0, c1, 512):
            e = min(s + 512, c1)
            r = jnp.dot(h, w_ref[:, s:e], preferred_element_type=_F32)
            if scale != 1.0:
                r = r * scale
            o_ref[:, s - c0:e - c0] = r.astype(o_ref.dtype)
            if gi in views:
                dims = views[gi]
                width = dims[-1]
                for j in range((e - s) // width):
                    idx = np.unravel_index((s - c0) // width + j, dims[:-1])
                    view_refs[gi][(slice(None),) + tuple(int(i) for i in idx) + (slice(None),)] = (
                        r[:, j * width:(j + 1) * width])


def _proj(x, gain, w, groups, dtypes, views=None):
    n, d = x.shape
    tm = min(TOKEN_TILE, n)
    views = views or {}
    row = lambda i: (i, 0)
    lead = lambda nd: (lambda i: (i,) + (0,) * nd)
    return pl.pallas_call(
        functools.partial(_proj_kernel, groups=groups, views=views),
        grid=(n // tm,),
        in_specs=[pl.BlockSpec((tm, d), row), _const_spec(gain.shape), _const_spec(w.shape)],
        out_specs=[pl.BlockSpec((tm, c1 - c0), row) for c0, c1, _ in groups]
        + [pl.BlockSpec((tm,) + dims, lead(len(dims))) for dims in views.values()],
        out_shape=[jax.ShapeDtypeStruct((n, c1 - c0), dt) for (c0, c1, _), dt in zip(groups, dtypes)]
        + [jax.ShapeDtypeStruct((n,) + dims, _F32) for dims in views.values()],
        compiler_params=_params("parallel"),
        name="proj",
    )(x, gain, w)


def _lane_bcast(m, n):
    if n <= LANES:
        return m[:, :n]
    return jnp.concatenate([m] * (n // LANES), axis=1)


def _attn_kernel(scal_ref, q_ref, k_ref, v_ref, gain_ref, o_ref, m_ref, l_ref, acc_ref, ka_ref, va_ref, diag_ref,
                 sa_ref, sb_ref, *, tk, lam_init):
    head = pl.program_id(1)
    qi = pl.program_id(2)
    lam = scal_ref[0]
    slope = scal_ref[1 + head]
    tq = nk = tk

    def stage(k_rows, v_rows, pos0, dst, n):
        pos = pos0 + lax.broadcasted_iota(jnp.int32, (n, LANES), 0)
        lane = lax.broadcasted_iota(jnp.int32, (n, LANES), 1)
        hi = (slope * CHUNK) * jnp.right_shift(pos, 6).astype(_F32)
        lo = slope * jnp.bitwise_and(pos, CHUNK - 1).astype(_F32)
        pair = jnp.right_shift(lane, 1)
        for mi in range(2):
            a0 = A_QK_DIM * (1 - mi) // 2
            keys = jnp.where(pair == a0, hi, jnp.where(pair == a0 + 1, lo, k_rows))
            ka_ref[mi, pl.ds(dst, n), :] = keys.astype(_BF)
        va_ref[pl.ds(dst, n), :] = jnp.concatenate([v_rows.astype(_BF), jnp.ones((n, LANES), _BF)], axis=1)

    @pl.when(qi == 0)
    def _():
        r = lax.broadcasted_iota(jnp.int32, (tq, nk), 0)
        c = lax.broadcasted_iota(jnp.int32, (tq, nk), 1)
        ahead = (-2.0 * LOG2E * slope) * jnp.maximum(c - r, 0).astype(_F32)
        diag_ref[...] = jnp.where((c // CHUNK) <= (r // CHUNK), ahead, -jnp.inf)

    own = pl.multiple_of(qi * tk, tk)
    stage(k_ref[0, pl.ds(own, tk), :], v_ref[0, pl.ds(own, tk), :], own, own, tk)

    q = q_ref[0]
    lane = lax.broadcasted_iota(jnp.int32, q.shape, 1)
    log2e = jnp.where(jnp.bitwise_and(lane, 1) == 0, LOG2E_HI, LOG2E_LO).astype(_BF)
    zero = jnp.zeros_like(q)
    quad = jnp.right_shift(lane, 2)
    q_maps = (jnp.where(lane < A_QK_DIM, q, jnp.where(quad == A_QK_DIM // 4, log2e, zero)),
              jnp.where(lane >= A_QK_DIM, q, jnp.where(quad == 0, log2e, zero)))

    m_ref[...] = jnp.full(m_ref.shape, -jnp.inf, _F32)
    l_ref[...] = jnp.zeros(l_ref.shape, _F32)
    acc_ref[...] = jnp.zeros(acc_ref.shape, _F32)

    def scores(off, n, slot, extra):
        for mi in range(2):
            s = lax.dot_general(q_maps[mi], ka_ref[mi, pl.ds(off, n), :], _NT, preferred_element_type=_F32)
            (sb_ref if slot else sa_ref)[mi, :, 0:n] = s if extra is None else s + extra

    def consume(off, n, slot):
        v_aug = va_ref[pl.ds(off, n), :]
        p, alpha = [], []
        for mi in range(2):
            s = (sb_ref if slot else sa_ref)[mi, :, 0:n]
            m_prev = m_ref[mi]
            m_next = jnp.maximum(m_prev, jnp.max(s, axis=1, keepdims=True))
            p.append(jnp.exp2(s - _lane_bcast(m_next, n)).astype(_BF))
            alpha.append(jnp.exp2(m_prev - m_next))
            m_ref[mi] = m_next
        for mi in range(2):
            pv = jnp.dot(p[mi], v_aug, preferred_element_type=_F32)
            acc_ref[mi] = alpha[mi] * acc_ref[mi] + pv[:, :LANES]
            l_ref[mi] = alpha[mi] * l_ref[mi] + pv[:, LANES:]

    A, B = 0, 1
    tile_off = lambda i: pl.multiple_of(jnp.where(i == 0, qi, i - 1) * tk, tk)
    scores(own, tk, A, diag_ref[...])

    def pair(k, carry):
        scores(tile_off(2 * k + 1), tk, B, None)
        consume(tile_off(2 * k), tk, A)
        scores(tile_off(2 * k + 2), tk, A, None)
        consume(tile_off(2 * k + 1), tk, B)
        return carry

    lax.fori_loop(0, qi // 2, pair, 0)

    @pl.when(qi % 2 == 0)
    def _():
        consume(tile_off(qi), tk, A)

    @pl.when(qi % 2 == 1)
    def _():
        scores(tile_off(qi), tk, B, None)
        consume(tile_off(qi - 1), tk, A)
        consume(tile_off(qi), tk, B)

    o = acc_ref[0] / l_ref[0] - lam * (acc_ref[1] / l_ref[1])
    o = _rms(o, gain_ref[...], DIFF_NORM_EPS) * (1.0 - lam_init)
    o_ref[0] = o.astype(o_ref.dtype)


def _attention(scal, q, k, v, gain, lam_init):
    b, t, width = q.shape
    heads = width // A_HEAD
    tk = min(ATTN_TILE, t)
    assert t % tk == 0 and tk % CHUNK == 0
    tile = lambda bi, hi, qi: (bi, qi, hi)
    whole = lambda bi, hi, qi: (bi, 0, hi)
    kern = functools.partial(_attn_kernel, tk=tk, lam_init=lam_init)
    return pl.pallas_call(
        kern,
        grid=(b, heads, t // tk),
        in_specs=[pl.BlockSpec(memory_space=pltpu.SMEM),
                  pl.BlockSpec((1, tk, A_HEAD), tile),
                  pl.BlockSpec((1, t, A_HEAD), whole),
                  pl.BlockSpec((1, t, A_HEAD), whole),
                  _const_spec(gain.shape)],
        out_specs=pl.BlockSpec((1, tk, A_HEAD), tile),
        out_shape=jax.ShapeDtypeStruct((b, t, width), _BF),
        scratch_shapes=[pltpu.VMEM((2, tk, LANES), _F32)] * 3
        + [pltpu.VMEM((2, t, LANES), _BF), pltpu.VMEM((t, 2 * LANES), _BF), pltpu.VMEM((tk, tk), _F32),
           pltpu.VMEM((2, tk, tk), _F32), pltpu.VMEM((2, tk, tk), _F32)],
        compiler_params=_params("parallel", "parallel", "arbitrary"),
        name="attn",
    )(scal, q, k, v, gain)


def _attn_step_kernel(scal_ref, q_ref, k_ref, v_ref, pk_ref, pv_ref, gain_ref, o_ref, m_ref, l_ref, acc_ref,
                      *, heads, tk, past_len, lam_init):
    j = pl.program_id(1)
    lam = scal_ref[0]
    tq = q_ref.shape[1]
    maps = [(h, mi) for h in range(heads) for mi in range(2)]

    @pl.when(j == 0)
    def _():
        m_ref[...] = jnp.full(m_ref.shape, -jnp.inf, _F32)
        l_ref[...] = jnp.zeros(l_ref.shape, _F32)
        acc_ref[...] = jnp.zeros(acc_ref.shape, _F32)

    q = q_ref[0]
    q_maps = [q[:, (2 * h + mi) * A_QK_DIM:(2 * h + mi + 1) * A_QK_DIM] for h, mi in maps]

    def update(keys, values, bias):
        n = keys[0].shape[0]
        s = [lax.dot_general(q_maps[i], keys[i].astype(_BF), _NT, preferred_element_type=_F32) + bias[maps[i][0]]
             for i in range(len(maps))]
        v_aug = [jnp.concatenate([v.astype(_BF), jnp.ones((n, LANES), _BF)], axis=1) for v in values]
        p, alpha = [], []
        for i in range(len(maps)):
            m_prev = m_ref[i]
            m_next = jnp.maximum(m_prev, jnp.max(s[i], axis=1, keepdims=True))
            p.append(jnp.exp2(s[i] - _lane_bcast(m_next, n)).astype(_BF))
            alpha.append(jnp.exp2(m_prev - m_next))
            m_ref[i] = m_next
        for i, (h, _) in enumerate(maps):
            pv = jnp.dot(p[i], v_aug[h], preferred_element_type=_F32)
            acc_ref[i] = alpha[i] * acc_ref[i] + pv[:, :LANES]
            l_ref[i] = alpha[i] * l_ref[i] + pv[:, LANES:]

    r = lax.broadcasted_iota(jnp.int32, (tq, tk), 0)
    c = lax.broadcasted_iota(jnp.int32, (tq, tk), 1)
    ahead = (r - c + (past_len - j * tk)).astype(_F32)
    update([pk_ref[0, pl.ds(2 * h + mi, tk, stride=2 * heads), :] for h, mi in maps],
           [pv_ref[0, pl.ds(h, tk, stride=heads), :] for h in range(heads)],
           [(-LOG2E * scal_ref[1 + h]) * ahead for h in range(heads)])

    @pl.when(j == pl.num_programs(1) - 1)
    def _():
        r = lax.broadcasted_iota(jnp.int32, (tq, tq), 0)
        c = lax.broadcasted_iota(jnp.int32, (tq, tq), 1)
        dist = jnp.abs(r - c).astype(_F32)
        visible = ((past_len + c) // CHUNK) <= ((past_len + r) // CHUNK)
        k_cur, v_cur = k_ref[0], v_ref[0]
        update([k_cur[:, (2 * h + mi) * A_QK_DIM:(2 * h + mi + 1) * A_QK_DIM] for h, mi in maps],
               [v_cur[:, h * A_HEAD:(h + 1) * A_HEAD] for h in range(heads)],
               [jnp.where(visible, (-LOG2E * scal_ref[1 + h]) * dist, -jnp.inf) for h in range(heads)])
        for h in range(heads):
            o = acc_ref[2 * h] / l_ref[2 * h] - lam * (acc_ref[2 * h + 1] / l_ref[2 * h + 1])
            o = _rms(o, gain_ref[...], DIFF_NORM_EPS) * (1.0 - lam_init)
            o_ref[0, :, h * A_HEAD:(h + 1) * A_HEAD] = o.astype(o_ref.dtype)


def _attention_step(scal, q, k, v, past_k, past_v, gain, lam_init):
    b, t, width = q.shape
    heads = width // A_HEAD
    past_len = past_k.shape[1]
    tk = min(ATTN_TILE, past_len)
    assert past_len % tk == 0 and t <= CHUNK
    pk = past_k.reshape(b, past_len * heads * 2, A_QK_DIM)
    pv = past_v.reshape(b, past_len * heads, A_HEAD)
    cur = pl.BlockSpec((1, t, width), lambda bi, j: (bi, 0, 0))
    kern = functools.partial(_attn_step_kernel, heads=heads, tk=tk, past_len=past_len, lam_init=lam_init)
    return pl.pallas_call(
        kern,
        grid=(b, past_len // tk),
        in_specs=[pl.BlockSpec(memory_space=pltpu.SMEM), cur, cur, cur,
                  pl.BlockSpec((1, tk * heads * 2, A_QK_DIM), lambda bi, j: (bi, j, 0)),
                  pl.BlockSpec((1, tk * heads, A_HEAD), lambda bi, j: (bi, j, 0)),
                  _const_spec(gain.shape)],
        out_specs=cur,
        out_shape=jax.ShapeDtypeStruct((b, t, width), _BF),
        scratch_shapes=[pltpu.VMEM((2 * heads, t, LANES), _F32)] * 3,
        compiler_params=_params("parallel", "arbitrary"),
        name="attn_step",
    )(scal, q, k, v, pk, pv, gain)


def _hgrn_kernel(hg_ref, lb_ref, gain_ref, s0_ref, ob_ref, sout_ref, st_ref, row_ref, w_ref, r_ref, *, L, heads,
                 nc):
    t = pl.program_id(1)
    width = heads * B_DK
    hs = range(heads)
    us = range(nc * heads)

    @pl.when(t == 0)
    def _():
        for h in hs:
            st_ref[h] = s0_ref[0, h].T

    rr = lax.broadcasted_iota(jnp.int32, (L, L), 0)
    cc = lax.broadcasted_iota(jnp.int32, (L, L), 1)
    tri = (rr >= cc).astype(_F32)
    ones = jnp.ones((LANES, LANES), _BF)
    blk, half = HGRN_BLOCK, HGRN_BLOCK // 2
    nb = L // blk
    blk_row = lax.broadcasted_iota(jnp.int32, (blk, LANES), 0)
    half_row = lax.broadcasted_iota(jnp.int32, (half, LANES), 0)
    rows = lambda x, a, b: x[a:b]

    def operand(g, u):
        c, h = divmod(u, heads)
        return hg_ref[0, c * L:(c + 1) * L, g * width + h * B_DK:g * width + (h + 1) * B_DK]

    qb = [operand(0, u) for u in us]
    ib = [operand(2, u) for u in us]
    f = []
    for u in us:
        lb = lb_ref[:, (u % heads) * B_DK:(u % heads + 1) * B_DK]
        f.append(lb + (1.0 - lb) * jax.nn.sigmoid(operand(1, u)))
    kk = [1.0 - f[u] for u in us]
    cb = [jnp.dot(tri, jnp.log2(f[u]), precision=_HI, preferred_element_type=_F32) for u in us]
    st = [st_ref[h] for h in hs]
    o_acc = [None] * len(us)
    for c in range(nc):
        for h in hs:
            u = c * heads + h
            o_acc[u] = lax.dot_general((qb[u] * jnp.exp2(cb[u])).astype(_BF), st[h].astype(_BF), _NT,
                                       preferred_element_type=_F32)
        for h in hs:
            u = c * heads + h
            cl = cb[u][L - 1:L, :]
            k_tail = kk[u] * jnp.exp2(cl - cb[u])
            st[h] = st[h] * jnp.exp2(cl) + lax.dot_general(ib[u].astype(_BF), k_tail.astype(_BF), _TN,
                                                           preferred_element_type=_F32)
    for h in hs:
        st_ref[h] = st[h]
    for h in us:
        row_ref[h, 0] = cb[h]
        row_ref[h, 1] = kk[h]
        row_ref[h, 2] = ib[h]

    if nb > 1:
        seg_start = [blk * i * (i - 1) // 2 for i in range(1, nb + 1)]
        n_keys = seg_start[-1]
        qr = lax.broadcasted_iota(jnp.int32, (L - blk, n_keys), 0)
        kc = lax.broadcasted_iota(jnp.int32, (L - blk, n_keys), 1)
        seg = jnp.zeros((L - blk, n_keys), jnp.int32)
        for i in range(2, nb):
            seg = jnp.where(kc >= seg_start[i - 1], i - 1, seg)
        own = seg == qr // blk
        scores = []
        for h in us:
            ref = [row_ref[h, 0, blk * i - 1:blk * i, :] for i in range(1, nb)]
            q_t = jnp.concatenate([rows(qb[h], blk * i, blk * (i + 1))
                                   * jnp.exp2(rows(cb[h], blk * i, blk * (i + 1)) - ref[i - 1])
                                   for i in range(1, nb)], axis=0)
            k_t = jnp.concatenate([rows(kk[h], 0, blk * i) * jnp.exp2(ref[i - 1] - rows(cb[h], 0, blk * i))
                                   for i in range(1, nb)], axis=0)
            scores.append(lax.dot_general(q_t.astype(_BF), k_t.astype(_BF), _NT, preferred_element_type=_F32))
        for h in us:
            v_t = jnp.concatenate([rows(ib[h], 0, blk * i) for i in range(1, nb)], axis=0)
            o_off = _mm(jnp.where(own, scores[h], 0.0), v_t)
            o_acc[h] = jnp.concatenate([rows(o_acc[h], 0, blk), rows(o_acc[h], blk, L) + o_off], axis=0)

    for h in us:
        pieces = []
        for i in range(nb):
            cb_i, q_i = rows(cb[h], blk * i, blk * (i + 1)), rows(qb[h], blk * i, blk * (i + 1))
            for sl in range(blk):
                s = blk * i + sl
                cbs = row_ref[h, 0, s:s + 1, :]
                ks = row_ref[h, 1, s:s + 1, :]
                if sl < half:
                    d = jnp.where(blk_row >= sl, cb_i - cbs, -jnp.inf)
                    pieces.append(q_i * (ks * jnp.exp2(d)))
                else:
                    d = jnp.where(half_row >= sl - half, cb_i[half:] - cbs, -jnp.inf)
                    pieces.append(q_i[half:] * (ks * jnp.exp2(d)))
        w_ref[h] = jnp.concatenate(pieces, axis=0).astype(_BF)
        r_ref[h] = jnp.dot(w_ref[h], ones, preferred_element_type=_F32)

    for h in us:
        o8 = [rows(o_acc[h], half * j, half * (j + 1)) for j in range(L // half)]
        off = 0
        for i in range(nb):
            for sl in range(blk):
                vs = row_ref[h, 2, blk * i + sl:blk * i + sl + 1, :]
                if sl < half:
                    o8[2 * i] = o8[2 * i] + r_ref[h, off:off + half, :] * vs
                    off += half
                o8[2 * i + 1] = o8[2 * i + 1] + r_ref[h, off:off + half, :] * vs
                off += half
        o = jnp.concatenate(o8, axis=0)
        ob = _rms(o, gain_ref[...], NORM_EPS) * _silu(operand(3, h))
        c, hd = divmod(h, heads)
        ob_ref[0, c * L:(c + 1) * L, hd * B_DK:(hd + 1) * B_DK] = ob.astype(ob_ref.dtype)

    @pl.when(t == pl.num_programs(1) - 1)
    def _():
        for h in hs:
            sout_ref[0, h] = st_ref[h].T


def _hgrn(hg, lb, gain, s0):
    b, t, four_w = hg.shape
    width = four_w // 4
    heads = width // B_DK
    L = min(CHUNK, t)
    half = HGRN_BLOCK // 2
    n_rows = (L // HGRN_BLOCK) * (half * HGRN_BLOCK + half * half)
    nc = HGRN_CHUNKS if t % (HGRN_CHUNKS * L) == 0 else 1
    kern = functools.partial(_hgrn_kernel, L=L, heads=heads, nc=nc)
    state_spec = pl.BlockSpec((1, heads, B_DK, B_DK), lambda bi, ti: (bi, 0, 0, 0))
    return pl.pallas_call(
        kern,
        grid=(b, t // (nc * L)),
        in_specs=[pl.BlockSpec((1, nc * L, four_w), lambda bi, ti: (bi, ti, 0)),
                  _const_spec(lb.shape), _const_spec(gain.shape), state_spec],
        out_specs=[pl.BlockSpec((1, nc * L, width), lambda bi, ti: (bi, ti, 0)), state_spec],
        out_shape=[jax.ShapeDtypeStruct((b, t, width), _BF),
                   jax.ShapeDtypeStruct((b, heads, B_DK, B_DK), _F32)],
        scratch_shapes=[pltpu.VMEM((heads, B_DK, B_DK), _F32),
                        pltpu.VMEM((nc * heads, 3, L, LANES), _F32),
                        pltpu.VMEM((nc * heads, n_rows, LANES), _BF),
                        pltpu.VMEM((nc * heads, n_rows, LANES), _F32)],
        compiler_params=_params("parallel", "arbitrary"),
        name="hgrn",
    )(hg, lb, gain, s0)


def _softplus(x):
    return jnp.maximum(x, 0.0) + jnp.log1p(jnp.exp(-jnp.abs(x)))


def _delta_kernel(q_ref, k_ref, v_ref, gate_ref, ab_ref, alog_ref, dtb_ref, gain_ref, s0_ref,
                  o_ref, sout_ref, s_ref, *, L, heads, nc):
    t = pl.program_id(1)
    hs = range(heads)
    pairs = [(c, h) for c in range(nc) for h in hs]

    @pl.when(t == 0)
    def _():
        for h in hs:
            s_ref[h] = s0_ref[0, h]

    rr = lax.broadcasted_iota(jnp.int32, (L, L), 0)
    cc = lax.broadcasted_iota(jnp.int32, (L, L), 1)
    lower = (rr >= cc).astype(_F32)
    upper = (rr <= cc).astype(_F32)

    ab = ab_ref[0]
    g_all = -jnp.exp(alog_ref[...]) * _softplus(ab + dtb_ref[...])
    beta_all = jax.nn.sigmoid(ab)
    chunk = lambda x, c: x[c * L:(c + 1) * L]
    gam_col = [jnp.dot(lower, chunk(g_all, c), precision=_HI, preferred_element_type=_F32) for c in range(nc)]
    gam_row = [lax.dot_general(chunk(g_all, c), upper, _TN, precision=_HI, preferred_element_type=_F32)
               for c in range(nc)]

    q, k, v, gc, bc, d_incl, d_strict = {}, {}, {}, {}, {}, {}, {}
    for c, h in pairs:
        p = (c, h)
        q[p], k[p], v[p] = (r[0, c * L:(c + 1) * L, h * C_DK:(h + 1) * C_DK] for r in (q_ref, k_ref, v_ref))
        gc[p] = gam_col[c][:, h:h + 1]
        bc[p] = chunk(beta_all, c)[:, heads + h:heads + h + 1]
        d_incl[p] = jnp.exp(jnp.where(rr >= cc, gc[p] - gam_row[c][h:h + 1, :], -jnp.inf))
        d_strict[p] = jnp.where(rr > cc, d_incl[p], 0.0)

    qkk = {p: lax.dot_general(jnp.concatenate([q[p], k[p]], axis=0).astype(_BF), k[p].astype(_BF), _NT,
                              preferred_element_type=_F32) for p in pairs}

    a = {p: bc[p] * qkk[p][L:] * d_strict[p] for p in pairs}
    n_inv = {p: -a[p] for p in pairs}
    pw = 2
    power = {p: _mm(a[p], a[p]) for p in pairs} if pw < L else None
    while pw < L:
        nxt = {p: _mm(power[p], power[p]) for p in pairs} if 2 * pw < L else None
        n_inv = {p: n_inv[p] + power[p] + _mm(n_inv[p], power[p]) for p in pairs}
        power = nxt
        pw *= 2

    eg = {p: jnp.exp(gc[p]) for p in pairs}
    rhs = {p: jnp.concatenate([k[p] * (bc[p] * eg[p]), v[p] * bc[p]], axis=1) for p in pairs}
    wu = {p: rhs[p] + _mm(n_inv[p], rhs[p]) for p in pairs}

    state = [s_ref[h] for h in hs]
    for c in range(nc):
        ps = [(c, h) for h in hs]
        ws = {p: _mm(jnp.concatenate([wu[p][:, :C_DK], q[p] * eg[p]], axis=0), state[p[1]]) for p in ps}
        v_new = {p: (wu[p][:, C_DK:] - ws[p][:L]).astype(_BF) for p in ps}
        o = {p: ws[p][L:] + _mm(qkk[p][:L] * d_incl[p], v_new[p]) for p in ps}
        for p in ps:
            h = p[1]
            gl = gc[p][L - 1:L, :]
            state[h] = jnp.exp(gl) * state[h] + lax.dot_general(
                (k[p] * jnp.exp(gl - gc[p])).astype(_BF), v_new[p], _TN, preferred_element_type=_F32)
            gate = gate_ref[0, c * L:(c + 1) * L, h * C_DK:(h + 1) * C_DK]
            on = _rms(o[p], gain_ref[...], NORM_EPS) * _silu(gate)
            o_ref[0, c * L:(c + 1) * L, h * C_DK:(h + 1) * C_DK] = on.astype(o_ref.dtype)
    for h in hs:
        s_ref[h] = state[h]

    @pl.when(t == pl.num_programs(1) - 1)
    def _():
        for h in hs:
            sout_ref[0, h] = s_ref[h]


def _delta(q, k, v, gate, ab, alog, dtb, gain, s0):
    b, t, width = q.shape
    heads = width // C_DK
    L = min(CHUNK, t)
    nc = DELTA_CHUNKS if t % (DELTA_CHUNKS * L) == 0 else 1
    rows = nc * L
    kern = functools.partial(_delta_kernel, L=L, heads=heads, nc=nc)
    step = lambda bi, ti: (bi, ti, 0)
    wide = pl.BlockSpec((1, rows, width), step)
    state_spec = pl.BlockSpec((1, heads, C_DK, C_DK), lambda bi, ti: (bi, 0, 0, 0))
    return pl.pallas_call(
        kern,
        grid=(b, t // rows),
        in_specs=[wide, wide, wide, wide, pl.BlockSpec((1, rows, LANES), step), _const_spec(alog.shape),
                  _const_spec(dtb.shape), _const_spec(gain.shape), state_spec],
        out_specs=[wide, state_spec],
        out_shape=[jax.ShapeDtypeStruct((b, t, width), _BF),
                   jax.ShapeDtypeStruct((b, heads, C_DK, C_DK), _F32)],
        scratch_shapes=[pltpu.VMEM((heads, C_DK, C_DK), _F32)],
        compiler_params=_params("parallel", "arbitrary"),
        name="delta",
    )(q, k, v, gate, ab, alog, dtb, gain, s0)


def _conv_proj_kernel(x_ref, halo_ref, g_ref, w_ref, cw_ref, hist_ref,
                      q_ref, k_ref, v_ref, gate_ref, ab_ref, cout_ref, pre_a, pre_b, *, width, tiles_per_seq):
    i = pl.program_id(0)
    tm = x_ref.shape[0]
    pad = halo_ref.shape[0]
    lo = pad - (CONV_W - 1)
    first = (i % tiles_per_seq) == 0
    h_all = _rms(jnp.concatenate([halo_ref[...], x_ref[...]], axis=0), g_ref[...], NORM_EPS).astype(_BF)
    step = 4 * C_DK
    n_chunk = 0
    for g, o_ref in enumerate((q_ref, k_ref, v_ref)):
        for s in range(0, width, step):
            c0 = g * width + s
            pre = pre_b if n_chunk % 2 else pre_a
            n_chunk += 1
            pre[...] = jnp.dot(h_all, w_ref[:, c0:c0 + step], preferred_element_type=_F32)
            pre[lo:pad, :] = jnp.where(first, hist_ref[0, :, c0:c0 + step], pre[lo:pad, :])
            cout_ref[0, :, c0:c0 + step] = pre[tm + lo:tm + pad, :]
            rows = pre[...]
            y = cw_ref[0:1, c0:c0 + step] * rows
            for j in range(1, CONV_W):
                y = cw_ref[j:j + 1, c0:c0 + step] * rows + pltpu.roll(y, 1, axis=0)
            y = _silu(y[pad:])
            for hh in range(step // C_DK):
                yh = y[:, hh * C_DK:(hh + 1) * C_DK]
                if g < 2:
                    yh = yh * lax.rsqrt(jnp.sum(yh * yh, axis=-1, keepdims=True) + L2_EPS)
                    if g == 0:
                        yh = yh * (C_DK ** -0.5)
                o_ref[:, s + hh * C_DK:s + (hh + 1) * C_DK] = yh
    h_main = h_all[pad:]
    for s in range(0, width, step):
        gate_ref[:, s:s + step] = jnp.dot(h_main, w_ref[:, 3 * width + s:3 * width + s + step],
                                          preferred_element_type=_F32)
    ab_ref[...] = jnp.dot(h_main, w_ref[:, 4 * width:4 * width + LANES], preferred_element_type=_F32)


def _conv_proj(x, gain, w, conv_w, hist, seq_len):
    n, d = x.shape
    width = conv_w.shape[1] // 3
    tm = min(TOKEN_TILE, seq_len)
    tiles_per_seq = seq_len // tm
    halo = 8
    row = lambda i: (i, 0)
    wide = pl.BlockSpec((tm, width), row)
    hist_spec = pl.BlockSpec((1, CONV_W - 1, 3 * width), lambda i: (i // tiles_per_seq, 0, 0))
    kern = functools.partial(_conv_proj_kernel, width=width, tiles_per_seq=tiles_per_seq)
    return pl.pallas_call(
        kern,
        grid=(n // tm,),
        in_specs=[pl.BlockSpec((tm, d), row),
                  pl.BlockSpec((halo, d), lambda i: (jnp.maximum(i * (tm // halo) - 1, 0), 0)),
                  _const_spec(gain.shape), _const_spec(w.shape), _const_spec(conv_w.shape), hist_spec],
        out_specs=[wide, wide, wide, wide, pl.BlockSpec((tm, LANES), row), hist_spec],
        out_shape=[jax.ShapeDtypeStruct((n, width), _F32)] * 4
        + [jax.ShapeDtypeStruct((n, LANES), _F32), jax.ShapeDtypeStruct(hist.shape, _F32)],
        scratch_shapes=[pltpu.VMEM((tm + halo, 4 * C_DK), _F32)] * 2,
        compiler_params=_params("arbitrary"),
        name="conv_proj",
    )(x, x, gain, w, conv_w, hist)


def _ffn_weights(w_in, w_out):
    d, two_ff = w_in.shape
    d_ff = two_ff // 2
    n_chunks = d_ff // FF_CHUNK
    split = lambda w: w.astype(_BF).reshape(d, n_chunks, FF_CHUNK).transpose(1, 0, 2)
    return split(w_in[:, :d_ff]), split(w_in[:, d_ff:]), w_out.astype(_BF).reshape(n_chunks, FF_CHUNK, d)


def _row(v):
    return v.astype(_F32).reshape(1, -1)


def _pad_lanes(v):
    return jnp.pad(_row(v), ((0, 0), (0, LANES - v.shape[-1])))


def kernel(x_prompt, x_sample, cache_attn_k, cache_attn_v, state_hgrn, state_delta, cache_conv, ffn1_norm, ffn1_w_in, ffn1_w_out, mix_norm, ffn2_norm, ffn2_w_in, ffn2_w_out, even_w_in, diff_lambda, diff_norm, hgrn_lb_logits, hgrn_norm, even_w_out, odd_w_in, conv_w, delta_A_log, delta_dt_bias, delta_norm, odd_w_out, final_norm):
    depth, d_model = ffn1_norm.shape
    a_width = cache_attn_k.shape[3] * cache_attn_k.shape[4] * cache_attn_k.shape[5]
    a_heads = cache_attn_k.shape[3]
    b_heads, b_width = state_hgrn.shape[2], state_hgrn.shape[2] * state_hgrn.shape[3]
    c_heads, c_width = state_delta.shape[2], state_delta.shape[2] * state_delta.shape[3]

    lower_bounds = jnp.cumsum(jax.nn.softmax(hgrn_lb_logits.astype(_F32), axis=0), axis=0)
    slopes = 2.0 ** (-8.0 * jnp.arange(1, a_heads + 1, dtype=_F32) / a_heads)
    ffn1 = [_ffn_weights(ffn1_w_in[l], ffn1_w_out[l]) for l in range(depth)]
    ffn2 = [_ffn_weights(ffn2_w_in[l], ffn2_w_out[l]) for l in range(depth)]

    def run(x, past_k, past_v, hgrn0, delta0, conv0):
        b, t, _ = x.shape
        x = x.reshape(b * t, d_model)
        ks, vs, hs, ds, cs = [], [], [], [], []
        mix = []
        for layer in range(depth):
            j = layer // 2
            x = _ffn(x, mix, _row(ffn1_norm[layer]), *ffn1[layer])
            if layer % 2 == 0:
                groups = ((0, a_width, A_QK_DIM ** -0.5 * LOG2E), (a_width, 2 * a_width, 1.0),
                          (2 * a_width, 3 * a_width, 1.0), (3 * a_width, 3 * a_width + 4 * b_width, 1.0))
                q, k, v, hg, k_out, v_out = _proj(
                    x, _row(mix_norm[layer]), even_w_in[j].astype(_BF), groups, (_BF, _F32, _F32, _F32),
                    views={1: (a_heads, 2, A_QK_DIM), 2: (a_heads, A_HEAD)})
                lam_init = 0.8 - 0.6 * math.exp(-0.3 * layer)
                lv = diff_lambda[j].astype(_F32)
                lam = jnp.exp(jnp.sum(lv[0] * lv[1])) - jnp.exp(jnp.sum(lv[2] * lv[3])) + lam_init
                scal = jnp.concatenate([lam.reshape(1), slopes])
                seq = lambda a: a.reshape(b, t, -1)
                if past_k is None:
                    oa = _attention(scal, seq(q), seq(k), seq(v), _row(diff_norm[j]), lam_init)
                else:
                    oa = _attention_step(scal, seq(q), seq(k), seq(v), past_k[j], past_v[j], _row(diff_norm[j]),
                                         lam_init)
                ob, s_new = _hgrn(seq(hg), _row(lower_bounds[j]), _row(hgrn_norm[j]), hgrn0[j])
                ks.append(k_out.reshape(b, t, a_heads, 2, A_QK_DIM))
                vs.append(v_out.reshape(b, t, a_heads, A_HEAD))
                hs.append(s_new)
                w_out = even_w_out[j].astype(_BF)
                mix = [(oa.reshape(b * t, a_width), w_out[:a_width]), (ob.reshape(b * t, b_width), w_out[a_width:])]
            else:
                w_in = odd_w_in[j]
                w_cat = jnp.concatenate([w_in[:, :4 * c_width],
                                         jnp.pad(w_in[:, 4 * c_width:], ((0, 0), (0, LANES - 2 * c_heads)))],
                                        axis=1).astype(_BF)
                q, k, v, gate, ab, c_new = _conv_proj(x, _row(mix_norm[layer]), w_cat, conv_w[j].astype(_F32),
                                                      conv0[j].astype(_F32), t)
                seq = lambda a: a.reshape(b, t, -1)
                o, s_new = _delta(seq(q), seq(k), seq(v), seq(gate), seq(ab), _pad_lanes(delta_A_log[j]),
                                  _pad_lanes(delta_dt_bias[j]), _row(delta_norm[j]), delta0[j])
                ds.append(s_new)
                cs.append(c_new)
                mix = [(o.reshape(b * t, c_width), odd_w_out[j].astype(_BF))]
            final = _row(final_norm) if layer == depth - 1 else None
            x = _ffn(x, mix, _row(ffn2_norm[layer]), *ffn2[layer], final_gain=final)
            mix = []
        return (x.reshape(b, t, d_model), jnp.stack(ks, axis=0), jnp.stack(vs, axis=0),
                jnp.stack(hs, axis=0), jnp.stack(ds, axis=0), jnp.stack(cs, axis=0))

    bp = x_prompt.shape[0]
    n_even, n_odd = state_hgrn.shape[0], state_delta.shape[0]
    zeros_h = jnp.zeros((n_even, bp) + state_hgrn.shape[2:], _F32)
    zeros_d = jnp.zeros((n_odd, bp) + state_delta.shape[2:], _F32)
    zeros_c = jnp.zeros((n_odd, bp) + cache_conv.shape[2:], x_prompt.dtype)
    y_p, k_p, v_p, h_p, d_p, c_p = run(x_prompt, None, None, zeros_h, zeros_d, zeros_c)
    y_s, k_s, v_s, h_s, d_s, c_s = run(x_sample, cache_attn_k, cache_attn_v, state_hgrn, state_delta, cache_conv)
    return (y_p, y_s, k_p, v_p, h_p, d_p, c_p, k_s, v_s, h_s, d_s, c_s)
```

```python
import functools
import math

import jax
import jax.numpy as jnp
import ml_dtypes
import numpy as np
from jax import lax
from jax.experimental import pallas as pl
from jax.experimental.pallas import tpu as pltpu

CHUNK = 64
A_QK_DIM = 64
A_HEAD = 2 * A_QK_DIM
B_DK = 128
C_DK = 128
CONV_W = 4
NORM_EPS = 1e-6
DIFF_NORM_EPS = 1e-5
L2_EPS = 1e-6
LOG2E = math.log2(math.e)
LOG2E_HI = float(np.float32(LOG2E).astype(ml_dtypes.bfloat16))
LOG2E_LO = float(np.float32(LOG2E - LOG2E_HI).astype(ml_dtypes.bfloat16))

LANES = 128
HGRN_BLOCK = 16
FF_CHUNK = 256
TOKEN_TILE = 512
ATTN_TILE = 512
DELTA_CHUNKS = 4
HGRN_CHUNKS = 4
VMEM_LIMIT = 56 * 1024 * 1024

_BF = jnp.bfloat16
_F32 = jnp.float32
_HI = lax.Precision.HIGHEST
_NT = (((1,), (1,)), ((), ()))
_TN = (((0,), (0,)), ((), ()))


def _params(*sem):
    return pltpu.CompilerParams(dimension_semantics=sem, vmem_limit_bytes=VMEM_LIMIT)


def _rms(x, gain, eps):
    return x * lax.rsqrt(jnp.mean(x * x, axis=-1, keepdims=True) + eps) * gain


def _silu(x):
    return x * jax.nn.sigmoid(x)


def _const_spec(shape):
    nd = len(shape)
    return pl.BlockSpec(shape, lambda *_: (0,) * nd)


def _mm(a, b):
    return jnp.dot(a.astype(_BF), b.astype(_BF), preferred_element_type=_F32)


def _ffn_kernel(*refs, n_mix, final, n_chunks):
    refs = list(refs)
    x_ref = refs.pop(0)
    mix = [(refs.pop(0), refs.pop(0)) for _ in range(n_mix)]
    g_ref, wg_ref, wu_ref, wo_ref = refs[:4]
    refs = refs[4:]
    fg_ref = refs.pop(0) if final else None
    o_ref, acc_ref = refs

    x = x_ref[...]
    for a_ref, w_ref in mix:
        x = x + jnp.dot(a_ref[...], w_ref[...], preferred_element_type=_F32)
    o_ref[...] = x
    h = _rms(x, g_ref[...], NORM_EPS).astype(_BF)
    for c in range(n_chunks):
        gate = jnp.dot(h, wg_ref[c], preferred_element_type=_F32)
        up = jnp.dot(h, wu_ref[c], preferred_element_type=_F32)
        act = (_silu(gate) * up).astype(_BF)
        part = jnp.dot(act, wo_ref[c], preferred_element_type=_F32)
        if c == 0:
            acc_ref[...] = part
        else:
            acc_ref[...] += part
    y = o_ref[...] + 0.5 * acc_ref[...]
    if final:
        y = _rms(y, fg_ref[...], NORM_EPS)
    o_ref[...] = y


def _ffn(x, mix, gain, wg, wu, wo, final_gain=None):
    n, d = x.shape
    tm = min(TOKEN_TILE, n)
    n_chunks = wg.shape[0]
    row = lambda i: (i, 0)
    args, specs = [x], [pl.BlockSpec((tm, d), row)]
    for a, w in mix:
        args += [a, w]
        specs += [pl.BlockSpec((tm, a.shape[1]), row), _const_spec(w.shape)]
    args += [gain, wg, wu, wo]
    specs += [_const_spec(gain.shape), _const_spec(wg.shape), _const_spec(wu.shape), _const_spec(wo.shape)]
    if final_gain is not None:
        args.append(final_gain)
        specs.append(_const_spec(final_gain.shape))
    kern = functools.partial(_ffn_kernel, n_mix=len(mix), final=final_gain is not None, n_chunks=n_chunks)
    return pl.pallas_call(
        kern,
        grid=(n // tm,),
        in_specs=specs,
        out_specs=pl.BlockSpec((tm, d), row),
        out_shape=jax.ShapeDtypeStruct((n, d), _F32),
        scratch_shapes=[pltpu.VMEM((tm, d), _F32)],
        compiler_params=_params("parallel"),
        name="ffn",
    )(*args)


def _proj_kernel(x_ref, g_ref, w_ref, *out_refs, groups, views):
    h = _rms(x_ref[...], g_ref[...], NORM_EPS).astype(_BF)
    view_refs = dict(zip(views, out_refs[len(groups):]))
    for gi, (o_ref, (c0, c1, scale)) in enumerate(zip(out_refs, groups)):
        for s in range(c0, c1, 512):
            e = min(s + 512, c1)
            r = jnp.dot(h, w_ref[:, s:e], preferred_element_type=_F32)
            if scale != 1.0:
                r = r * scale
            o_ref[:, s - c0:e - c0] = r.astype(o_ref.dtype)
            if gi in views:
                dims = views[gi]
                width = dims[-1]
                for j in range((e - s) // width):
                    idx = np.unravel_index((s - c0) // width + j, dims[:-1])
                    view_refs[gi][(slice(None),) + tuple(int(i) for i in idx) + (slice(None),)] = (
                        r[:, j * width:(j + 1) * width])


def _proj(x, gain, w, groups, dtypes, views=None):
    n, d = x.shape
    tm = min(TOKEN_TILE, n)
    views = views or {}
    row = lambda i: (i, 0)
    lead = lambda nd: (lambda i: (i,) + (0,) * nd)
    return pl.pallas_call(
        functools.partial(_proj_kernel, groups=groups, views=views),
        grid=(n // tm,),
        in_specs=[pl.BlockSpec((tm, d), row), _const_spec(gain.shape), _const_spec(w.shape)],
        out_specs=[pl.BlockSpec((tm, c1 - c0), row) for c0, c1, _ in groups]
        + [pl.BlockSpec((tm,) + dims, lead(len(dims))) for dims in views.values()],
        out_shape=[jax.ShapeDtypeStruct((n, c1 - c0), dt) for (c0, c1, _), dt in zip(groups, dtypes)]
        + [jax.ShapeDtypeStruct((n,) + dims, _F32) for dims in views.values()],
        compiler_params=_params("parallel"),
        name="proj",
    )(x, gain, w)


def _lane_bcast(m, n):
    if n <= LANES:
        return m[:, :n]
    return jnp.concatenate([m] * (n // LANES), axis=1)


def _attn_kernel(scal_ref, q_ref, k_ref, v_ref, gain_ref, o_ref, m_ref, l_ref, acc_ref, ka_ref, va_ref, diag_ref,
                 sa_ref, sb_ref, *, tk, lam_init):
    head = pl.program_id(1)
    qi = pl.program_id(2)
    lam = scal_ref[0]
    slope = scal_ref[1 + head]
    tq = nk = tk

    def stage(k_rows, v_rows, pos0, dst, n):
        pos = pos0 + lax.broadcasted_iota(jnp.int32, (n, LANES), 0)
        lane = lax.broadcasted_iota(jnp.int32, (n, LANES), 1)
        hi = (slope * CHUNK) * jnp.right_shift(pos, 6).astype(_F32)
        lo = slope * jnp.bitwise_and(pos, CHUNK - 1).astype(_F32)
        pair = jnp.right_shift(lane, 1)
        for mi in range(2):
            a0 = A_QK_DIM * (1 - mi) // 2
            keys = jnp.where(pair == a0, hi, jnp.where(pair == a0 + 1, lo, k_rows))
            ka_ref[mi, pl.ds(dst, n), :] = keys.astype(_BF)
        va_ref[pl.ds(dst, n), :] = jnp.concatenate([v_rows.astype(_BF), jnp.ones((n, LANES), _BF)], axis=1)

    @pl.when(qi == 0)
    def _():
        r = lax.broadcasted_iota(jnp.int32, (tq, nk), 0)
        c = lax.broadcasted_iota(jnp.int32, (tq, nk), 1)
        ahead = (-2.0 * LOG2E * slope) * jnp.maximum(c - r, 0).astype(_F32)
        diag_ref[...] = jnp.where((c // CHUNK) <= (r // CHUNK), ahead, -jnp.inf)

    own = pl.multiple_of(qi * tk, tk)
    stage(k_ref[0, pl.ds(own, tk), :], v_ref[0, pl.ds(own, tk), :], own, own, tk)

    q = q_ref[0]
    lane = lax.broadcasted_iota(jnp.int32, q.shape, 1)
    log2e = jnp.where(jnp.bitwise_and(lane, 1) == 0, LOG2E_HI, LOG2E_LO).astype(_BF)
    zero = jnp.zeros_like(q)
    quad = jnp.right_shift(lane, 2)
    q_maps = (jnp.where(lane < A_QK_DIM, q, jnp.where(quad == A_QK_DIM // 4, log2e, zero)),
              jnp.where(lane >= A_QK_DIM, q, jnp.where(quad == 0, log2e, zero)))

    m_ref[...] = jnp.full(m_ref.shape, -jnp.inf, _F32)
    l_ref[...] = jnp.zeros(l_ref.shape, _F32)
    acc_ref[...] = jnp.zeros(acc_ref.shape, _F32)

    def scores(off, n, slot, extra):
        for mi in range(2):
            s = lax.dot_general(q_maps[mi], ka_ref[mi, pl.ds(off, n), :], _NT, preferred_element_type=_F32)
            (sb_ref if slot else sa_ref)[mi, :, 0:n] = s if extra is None else s + extra

    def consume(off, n, slot):
        v_aug = va_ref[pl.ds(off, n), :]
        p, alpha = [], []
        for mi in range(2):
            s = (sb_ref if slot else sa_ref)[mi, :, 0:n]
            m_prev = m_ref[mi]
            m_next = jnp.maximum(m_prev, jnp.max(s, axis=1, keepdims=True))
            p.append(jnp.exp2(s - _lane_bcast(m_next, n)).astype(_BF))
            alpha.append(jnp.exp2(m_prev - m_next))
            m_ref[mi] = m_next
        for mi in range(2):
            pv = jnp.dot(p[mi], v_aug, preferred_element_type=_F32)
            acc_ref[mi] = alpha[mi] * acc_ref[mi] + pv[:, :LANES]
            l_ref[mi] = alpha[mi] * l_ref[mi] + pv[:, LANES:]

    A, B = 0, 1
    tile_off = lambda i: pl.multiple_of(jnp.where(i == 0, qi, i - 1) * tk, tk)
    scores(own, tk, A, diag_ref[...])

    def pair(k, carry):
        scores(tile_off(2 * k + 1), tk, B, None)
        consume(tile_off(2 * k), tk, A)
        scores(tile_off(2 * k + 2), tk, A, None)
        consume(tile_off(2 * k + 1), tk, B)
        return carry

    lax.fori_loop(0, qi // 2, pair, 0)

    @pl.when(qi % 2 == 0)
    def _():
        consume(tile_off(qi), tk, A)

    @pl.when(qi % 2 == 1)
    def _():
        scores(tile_off(qi), tk, B, None)
        consume(tile_off(qi - 1), tk, A)
        consume(tile_off(qi), tk, B)

    o = acc_ref[0] / l_ref[0] - lam * (acc_ref[1] / l_ref[1])
    o = _rms(o, gain_ref[...], DIFF_NORM_EPS) * (1.0 - lam_init)
    o_ref[0] = o.astype(o_ref.dtype)


def _attention(scal, q, k, v, gain, lam_init):
    b, t, width = q.shape
    heads = width // A_HEAD
    tk = min(ATTN_TILE, t)
    assert t % tk == 0 and tk % CHUNK == 0
    tile = lambda bi, hi, qi: (bi, qi, hi)
    whole = lambda bi, hi, qi: (bi, 0, hi)
    kern = functools.partial(_attn_kernel, tk=tk, lam_init=lam_init)
    return pl.pallas_call(
        kern,
        grid=(b, heads, t // tk),
        in_specs=[pl.BlockSpec(memory_space=pltpu.SMEM),
                  pl.BlockSpec((1, tk, A_HEAD), tile),
                  pl.BlockSpec((1, t, A_HEAD), whole),
                  pl.BlockSpec((1, t, A_HEAD), whole),
                  _const_spec(gain.shape)],
        out_specs=pl.BlockSpec((1, tk, A_HEAD), tile),
        out_shape=jax.ShapeDtypeStruct((b, t, width), _BF),
        scratch_shapes=[pltpu.VMEM((2, tk, LANES), _F32)] * 3
        + [pltpu.VMEM((2, t, LANES), _BF), pltpu.VMEM((t, 2 * LANES), _BF), pltpu.VMEM((tk, tk), _F32),
           pltpu.VMEM((2, tk, tk), _F32), pltpu.VMEM((2, tk, tk), _F32)],
        compiler_params=_params("parallel", "parallel", "arbitrary"),
        name="attn",
    )(scal, q, k, v, gain)


def _attn_step_kernel(scal_ref, q_ref, k_ref, v_ref, pk_ref, pv_ref, gain_ref, o_ref, m_ref, l_ref, acc_ref,
                      *, heads, tk, past_len, lam_init):
    j = pl.program_id(1)
    lam = scal_ref[0]
    tq = q_ref.shape[1]
    maps = [(h, mi) for h in range(heads) for mi in range(2)]

    @pl.when(j == 0)
    def _():
        m_ref[...] = jnp.full(m_ref.shape, -jnp.inf, _F32)
        l_ref[...] = jnp.zeros(l_ref.shape, _F32)
        acc_ref[...] = jnp.zeros(acc_ref.shape, _F32)

    q = q_ref[0]
    q_maps = [q[:, (2 * h + mi) * A_QK_DIM:(2 * h + mi + 1) * A_QK_DIM] for h, mi in maps]

    def update(keys, values, bias):
        n = keys[0].shape[1]
        s = [jnp.dot(q_maps[i], keys[i].astype(_BF), preferred_element_type=_F32) + bias[maps[i][0]]
             for i in range(len(maps))]
        v_aug = [jnp.concatenate([v.astype(_BF), jnp.ones((n, LANES), _BF)], axis=1) for v in values]
        p, alpha = [], []
        for i in range(len(maps)):
            m_prev = m_ref[i]
            m_next = jnp.maximum(m_prev, jnp.max(s[i], axis=1, keepdims=True))
            p.append(jnp.exp2(s[i] - _lane_bcast(m_next, n)).astype(_BF))
            alpha.append(jnp.exp2(m_prev - m_next))
            m_ref[i] = m_next
        for i, (h, _) in enumerate(maps):
            pv = jnp.dot(p[i], v_aug[h], preferred_element_type=_F32)
            acc_ref[i] = alpha[i] * acc_ref[i] + pv[:, :LANES]
            l_ref[i] = alpha[i] * l_ref[i] + pv[:, LANES:]

    r = lax.broadcasted_iota(jnp.int32, (tq, tk), 0)
    c = lax.broadcasted_iota(jnp.int32, (tq, tk), 1)
    ahead = (r - c + (past_len - j * tk)).astype(_F32)
    update([pk_ref[0, h, mi] for h, mi in maps],
           [pv_ref[0, pl.ds(h, tk, stride=heads), :] for h in range(heads)],
           [(-LOG2E * scal_ref[1 + h]) * ahead for h in range(heads)])

    @pl.when(j == pl.num_programs(1) - 1)
    def _():
        r = lax.broadcasted_iota(jnp.int32, (tq, tq), 0)
        c = lax.broadcasted_iota(jnp.int32, (tq, tq), 1)
        dist = jnp.abs(r - c).astype(_F32)
        visible = ((past_len + c) // CHUNK) <= ((past_len + r) // CHUNK)
        k_cur, v_cur = k_ref[0], v_ref[0]
        update([k_cur[:, (2 * h + mi) * A_QK_DIM:(2 * h + mi + 1) * A_QK_DIM].T for h, mi in maps],
               [v_cur[:, h * A_HEAD:(h + 1) * A_HEAD] for h in range(heads)],
               [jnp.where(visible, (-LOG2E * scal_ref[1 + h]) * dist, -jnp.inf) for h in range(heads)])
        for h in range(heads):
            o = acc_ref[2 * h] / l_ref[2 * h] - lam * (acc_ref[2 * h + 1] / l_ref[2 * h + 1])
            o = _rms(o, gain_ref[...], DIFF_NORM_EPS) * (1.0 - lam_init)
            o_ref[0, :, h * A_HEAD:(h + 1) * A_HEAD] = o.astype(o_ref.dtype)


def _attention_step(scal, q, k, v, past_k, past_v, gain, lam_init):
    b, t, width = q.shape
    heads = width // A_HEAD
    past_len = past_k.shape[1]
    tk = min(ATTN_TILE, past_len)
    assert past_len % tk == 0 and t <= CHUNK
    pk = jnp.transpose(past_k, (0, 2, 3, 4, 1))
    pv = past_v.reshape(b, past_len * heads, A_HEAD)
    cur = pl.BlockSpec((1, t, width), lambda bi, j: (bi, 0, 0))
    kern = functools.partial(_attn_step_kernel, heads=heads, tk=tk, past_len=past_len, lam_init=lam_init)
    return pl.pallas_call(
        kern,
        grid=(b, past_len // tk),
        in_specs=[pl.BlockSpec(memory_space=pltpu.SMEM), cur, cur, cur,
                  pl.BlockSpec((1, heads, 2, A_QK_DIM, tk), lambda bi, j: (bi, 0, 0, 0, j)),
                  pl.BlockSpec((1, tk * heads, A_HEAD), lambda bi, j: (bi, j, 0)),
                  _const_spec(gain.shape)],
        out_specs=cur,
        out_shape=jax.ShapeDtypeStruct((b, t, width), _BF),
        scratch_shapes=[pltpu.VMEM((2 * heads, t, LANES), _F32)] * 3,
        compiler_params=_params("parallel", "arbitrary"),
        name="attn_step",
    )(scal, q, k, v, pk, pv, gain)


def _hgrn_kernel(hg_ref, lb_ref, gain_ref, s0_ref, ob_ref, sout_ref, st_ref, row_ref, w_ref, r_ref, *, L, heads,
                 nc):
    t = pl.program_id(1)
    width = heads * B_DK
    hs = range(heads)
    us = range(nc * heads)

    @pl.when(t == 0)
    def _():
        for h in hs:
            st_ref[h] = s0_ref[0, h].T

    rr = lax.broadcasted_iota(jnp.int32, (L, L), 0)
    cc = lax.broadcasted_iota(jnp.int32, (L, L), 1)
    tri = (rr >= cc).astype(_F32)
    ones = jnp.ones((LANES, LANES), _BF)
    blk, half = HGRN_BLOCK, HGRN_BLOCK // 2
    nb = L // blk
    blk_row = lax.broadcasted_iota(jnp.int32, (blk, LANES), 0)
    half_row = lax.broadcasted_iota(jnp.int32, (half, LANES), 0)
    rows = lambda x, a, b: x[a:b]

    def operand(g, u):
        c, h = divmod(u, heads)
        return hg_ref[0, c * L:(c + 1) * L, g * width + h * B_DK:g * width + (h + 1) * B_DK]

    qb = [operand(0, u) for u in us]
    ib = [operand(2, u) for u in us]
    f = []
    for u in us:
        lb = lb_ref[:, (u % heads) * B_DK:(u % heads + 1) * B_DK]
        f.append(lb + (1.0 - lb) * jax.nn.sigmoid(operand(1, u)))
    kk = [1.0 - f[u] for u in us]
    cb = [jnp.dot(tri, jnp.log2(f[u]), precision=_HI, preferred_element_type=_F32) for u in us]
    st = [st_ref[h] for h in hs]
    o_acc = [None] * len(us)
    for c in range(nc):
        for h in hs:
            u = c * heads + h
            o_acc[u] = lax.dot_general((qb[u] * jnp.exp2(cb[u])).astype(_BF), st[h].astype(_BF), _NT,
                                       preferred_element_type=_F32)
        for h in hs:
            u = c * heads + h
            cl = cb[u][L - 1:L, :]
            k_tail = kk[u] * jnp.exp2(cl - cb[u])
            st[h] = st[h] * jnp.exp2(cl) + lax.dot_general(ib[u].astype(_BF), k_tail.astype(_BF), _TN,
                                                           preferred_element_type=_F32)
    for h in hs:
        st_ref[h] = st[h]
    for h in us:
        row_ref[h, 0] = cb[h]
        row_ref[h, 1] = kk[h]
        row_ref[h, 2] = ib[h]

    if nb > 1:
        seg_start = [blk * i * (i - 1) // 2 for i in range(1, nb + 1)]
        n_keys = seg_start[-1]
        qr = lax.broadcasted_iota(jnp.int32, (L - blk, n_keys), 0)
        kc = lax.broadcasted_iota(jnp.int32, (L - blk, n_keys), 1)
        seg = jnp.zeros((L - blk, n_keys), jnp.int32)
        for i in range(2, nb):
            seg = jnp.where(kc >= seg_start[i - 1], i - 1, seg)
        own = seg == qr // blk
        scores = []
        for h in us:
            ref = [row_ref[h, 0, blk * i - 1:blk * i, :] for i in range(1, nb)]
            q_t = jnp.concatenate([rows(qb[h], blk * i, blk * (i + 1))
                                   * jnp.exp2(rows(cb[h], blk * i, blk * (i + 1)) - ref[i - 1])
                                   for i in range(1, nb)], axis=0)
            k_t = jnp.concatenate([rows(kk[h], 0, blk * i) * jnp.exp2(ref[i - 1] - rows(cb[h], 0, blk * i))
                                   for i in range(1, nb)], axis=0)
            scores.append(lax.dot_general(q_t.astype(_BF), k_t.astype(_BF), _NT, preferred_element_type=_F32))
        for h in us:
            v_t = jnp.concatenate([rows(ib[h], 0, blk * i) for i in range(1, nb)], axis=0)
            o_off = _mm(jnp.where(own, scores[h], 0.0), v_t)
            o_acc[h] = jnp.concatenate([rows(o_acc[h], 0, blk), rows(o_acc[h], blk, L) + o_off], axis=0)

    for h in us:
        pieces = []
        for i in range(nb):
            cb_i, q_i = rows(cb[h], blk * i, blk * (i + 1)), rows(qb[h], blk * i, blk * (i + 1))
            for sl in range(blk):
                s = blk * i + sl
                cbs = row_ref[h, 0, s:s + 1, :]
                ks = row_ref[h, 1, s:s + 1, :]
                if sl < half:
                    d = jnp.where(blk_row >= sl, cb_i - cbs, -jnp.inf)
                    pieces.append(q_i * (ks * jnp.exp2(d)))
                else:
                    d = jnp.where(half_row >= sl - half, cb_i[half:] - cbs, -jnp.inf)
                    pieces.append(q_i[half:] * (ks * jnp.exp2(d)))
        w_ref[h] = jnp.concatenate(pieces, axis=0).astype(_BF)
        r_ref[h] = jnp.dot(w_ref[h], ones, preferred_element_type=_F32)

    for h in us:
        o8 = [rows(o_acc[h], half * j, half * (j + 1)) for j in range(L // half)]
        off = 0
        for i in range(nb):
            for sl in range(blk):
                vs = row_ref[h, 2, blk * i + sl:blk * i + sl + 1, :]
                if sl < half:
                    o8[2 * i] = o8[2 * i] + r_ref[h, off:off + half, :] * vs
                    off += half
                o8[2 * i + 1] = o8[2 * i + 1] + r_ref[h, off:off + half, :] * vs
                off += half
        o = jnp.concatenate(o8, axis=0)
        ob = _rms(o, gain_ref[...], NORM_EPS) * _silu(operand(3, h))
        c, hd = divmod(h, heads)
        ob_ref[0, c * L:(c + 1) * L, hd * B_DK:(hd + 1) * B_DK] = ob.astype(ob_ref.dtype)

    @pl.when(t == pl.num_programs(1) - 1)
    def _():
        for h in hs:
            sout_ref[0, h] = st_ref[h].T


def _hgrn(hg, lb, gain, s0):
    b, t, four_w = hg.shape
    width = four_w // 4
    heads = width // B_DK
    L = min(CHUNK, t)
    half = HGRN_BLOCK // 2
    n_rows = (L // HGRN_BLOCK) * (half * HGRN_BLOCK + half * half)
    nc = HGRN_CHUNKS if t % (HGRN_CHUNKS * L) == 0 else 1
    kern = functools.partial(_hgrn_kernel, L=L, heads=heads, nc=nc)
    state_spec = pl.BlockSpec((1, heads, B_DK, B_DK), lambda bi, ti: (bi, 0, 0, 0))
    return pl.pallas_call(
        kern,
        grid=(b, t // (nc * L)),
        in_specs=[pl.BlockSpec((1, nc * L, four_w), lambda bi, ti: (bi, ti, 0)),
                  _const_spec(lb.shape), _const_spec(gain.shape), state_spec],
        out_specs=[pl.BlockSpec((1, nc * L, width), lambda bi, ti: (bi, ti, 0)), state_spec],
        out_shape=[jax.ShapeDtypeStruct((b, t, width), _BF),
                   jax.ShapeDtypeStruct((b, heads, B_DK, B_DK), _F32)],
        scratch_shapes=[pltpu.VMEM((heads, B_DK, B_DK), _F32),
                        pltpu.VMEM((nc * heads, 3, L, LANES), _F32),
                        pltpu.VMEM((nc * heads, n_rows, LANES), _BF),
                        pltpu.VMEM((nc * heads, n_rows, LANES), _F32)],
        compiler_params=_params("parallel", "arbitrary"),
        name="hgrn",
    )(hg, lb, gain, s0)


def _softplus(x):
    return jnp.maximum(x, 0.0) + jnp.log1p(jnp.exp(-jnp.abs(x)))


def _delta_kernel(q_ref, k_ref, v_ref, gate_ref, ab_ref, alog_ref, dtb_ref, gain_ref, s0_ref,
                  o_ref, sout_ref, s_ref, *, L, heads, nc):
    t = pl.program_id(1)
    hs = range(heads)
    pairs = [(c, h) for c in range(nc) for h in hs]

    @pl.when(t == 0)
    def _():
        for h in hs:
            s_ref[h] = s0_ref[0, h]

    rr = lax.broadcasted_iota(jnp.int32, (L, L), 0)
    cc = lax.broadcasted_iota(jnp.int32, (L, L), 1)
    lower = (rr >= cc).astype(_F32)
    upper = (rr <= cc).astype(_F32)

    ab = ab_ref[0]
    g_all = -jnp.exp(alog_ref[...]) * _softplus(ab + dtb_ref[...])
    beta_all = jax.nn.sigmoid(ab)
    chunk = lambda x, c: x[c * L:(c + 1) * L]
    gam_col = [jnp.dot(lower, chunk(g_all, c), precision=_HI, preferred_element_type=_F32) for c in range(nc)]
    gam_row = [lax.dot_general(chunk(g_all, c), upper, _TN, precision=_HI, preferred_element_type=_F32)
               for c in range(nc)]

    q, k, v, gc, bc, d_incl, d_strict = {}, {}, {}, {}, {}, {}, {}
    for c, h in pairs:
        p = (c, h)
        q[p], k[p], v[p] = (r[0, c * L:(c + 1) * L, h * C_DK:(h + 1) * C_DK] for r in (q_ref, k_ref, v_ref))
        gc[p] = gam_col[c][:, h:h + 1]
        bc[p] = chunk(beta_all, c)[:, heads + h:heads + h + 1]
        d_incl[p] = jnp.exp(jnp.where(rr >= cc, gc[p] - gam_row[c][h:h + 1, :], -jnp.inf))
        d_strict[p] = jnp.where(rr > cc, d_incl[p], 0.0)

    qkk = {p: lax.dot_general(jnp.concatenate([q[p], k[p]], axis=0).astype(_BF), k[p].astype(_BF), _NT,
                              preferred_element_type=_F32) for p in pairs}

    a = {p: bc[p] * qkk[p][L:] * d_strict[p] for p in pairs}
    n_inv = {p: -a[p] for p in pairs}
    pw = 2
    power = {p: _mm(a[p], a[p]) for p in pairs} if pw < L else None
    while pw < L:
        nxt = {p: _mm(power[p], power[p]) for p in pairs} if 2 * pw < L else None
        n_inv = {p: n_inv[p] + power[p] + _mm(n_inv[p], power[p]) for p in pairs}
        power = nxt
        pw *= 2

    eg = {p: jnp.exp(gc[p]) for p in pairs}
    rhs = {p: jnp.concatenate([k[p] * (bc[p] * eg[p]), v[p] * bc[p]], axis=1) for p in pairs}
    wu = {p: rhs[p] + _mm(n_inv[p], rhs[p]) for p in pairs}

    state = [s_ref[h] for h in hs]
    for c in range(nc):
        ps = [(c, h) for h in hs]
        ws = {p: _mm(jnp.concatenate([wu[p][:, :C_DK], q[p] * eg[p]], axis=0), state[p[1]]) for p in ps}
        v_new = {p: (wu[p][:, C_DK:] - ws[p][:L]).astype(_BF) for p in ps}
        o = {p: ws[p][L:] + _mm(qkk[p][:L] * d_incl[p], v_new[p]) for p in ps}
        for p in ps:
            h = p[1]
            gl = gc[p][L - 1:L, :]
            state[h] = jnp.exp(gl) * state[h] + lax.dot_general(
                (k[p] * jnp.exp(gl - gc[p])).astype(_BF), v_new[p], _TN, preferred_element_type=_F32)
            gate = gate_ref[0, c * L:(c + 1) * L, h * C_DK:(h + 1) * C_DK]
            on = _rms(o[p], gain_ref[...], NORM_EPS) * _silu(gate)
            o_ref[0, c * L:(c + 1) * L, h * C_DK:(h + 1) * C_DK] = on.astype(o_ref.dtype)
    for h in hs:
        s_ref[h] = state[h]

    @pl.when(t == pl.num_programs(1) - 1)
    def _():
        for h in hs:
            sout_ref[0, h] = s_ref[h]


def _delta(q, k, v, gate, ab, alog, dtb, gain, s0):
    b, t, width = q.shape
    heads = width // C_DK
    L = min(CHUNK, t)
    nc = DELTA_CHUNKS if t % (DELTA_CHUNKS * L) == 0 else 1
    rows = nc * L
    kern = functools.partial(_delta_kernel, L=L, heads=heads, nc=nc)
    step = lambda bi, ti: (bi, ti, 0)
    wide = pl.BlockSpec((1, rows, width), step)
    state_spec = pl.BlockSpec((1, heads, C_DK, C_DK), lambda bi, ti: (bi, 0, 0, 0))
    return pl.pallas_call(
        kern,
        grid=(b, t // rows),
        in_specs=[wide, wide, wide, wide, pl.BlockSpec((1, rows, LANES), step), _const_spec(alog.shape),
                  _const_spec(dtb.shape), _const_spec(gain.shape), state_spec],
        out_specs=[wide, state_spec],
        out_shape=[jax.ShapeDtypeStruct((b, t, width), _BF),
                   jax.ShapeDtypeStruct((b, heads, C_DK, C_DK), _F32)],
        scratch_shapes=[pltpu.VMEM((heads, C_DK, C_DK), _F32)],
        compiler_params=_params("parallel", "arbitrary"),
        name="delta",
    )(q, k, v, gate, ab, alog, dtb, gain, s0)


def _conv_proj_kernel(x_ref, halo_ref, g_ref, w_ref, cw_ref, hist_ref,
                      q_ref, k_ref, v_ref, gate_ref, ab_ref, cout_ref, pre_a, pre_b, *, width, tiles_per_seq):
    i = pl.program_id(0)
    tm = x_ref.shape[0]
    pad = halo_ref.shape[0]
    lo = pad - (CONV_W - 1)
    first = (i % tiles_per_seq) == 0
    h_all = _rms(jnp.concatenate([halo_ref[...], x_ref[...]], axis=0), g_ref[...], NORM_EPS).astype(_BF)
    step = 4 * C_DK
    n_chunk = 0
    for g, o_ref in enumerate((q_ref, k_ref, v_ref)):
        for s in range(0, width, step):
            c0 = g * width + s
            pre = pre_b if n_chunk % 2 else pre_a
            n_chunk += 1
            pre[...] = jnp.dot(h_all, w_ref[:, c0:c0 + step], preferred_element_type=_F32)
            pre[lo:pad, :] = jnp.where(first, hist_ref[0, :, c0:c0 + step], pre[lo:pad, :])
            cout_ref[0, :, c0:c0 + step] = pre[tm + lo:tm + pad, :]
            rows = pre[...]
            y = cw_ref[0:1, c0:c0 + step] * rows
            for j in range(1, CONV_W):
                y = cw_ref[j:j + 1, c0:c0 + step] * rows + pltpu.roll(y, 1, axis=0)
            y = _silu(y[pad:])
            for hh in range(step // C_DK):
                yh = y[:, hh * C_DK:(hh + 1) * C_DK]
                if g < 2:
                    yh = yh * lax.rsqrt(jnp.sum(yh * yh, axis=-1, keepdims=True) + L2_EPS)
                    if g == 0:
                        yh = yh * (C_DK ** -0.5)
                o_ref[:, s + hh * C_DK:s + (hh + 1) * C_DK] = yh
    h_main = h_all[pad:]
    for s in range(0, width, step):
        gate_ref[:, s:s + step] = jnp.dot(h_main, w_ref[:, 3 * width + s:3 * width + s + step],
                                          preferred_element_type=_F32)
    ab_ref[...] = jnp.dot(h_main, w_ref[:, 4 * width:4 * width + LANES], preferred_element_type=_F32)


def _conv_proj(x, gain, w, conv_w, hist, seq_len):
    n, d = x.shape
    width = conv_w.shape[1] // 3
    tm = min(TOKEN_TILE, seq_len)
    tiles_per_seq = seq_len // tm
    halo = 8
    row = lambda i: (i, 0)
    wide = pl.BlockSpec((tm, width), row)
    hist_spec = pl.BlockSpec((1, CONV_W - 1, 3 * width), lambda i: (i // tiles_per_seq, 0, 0))
    kern = functools.partial(_conv_proj_kernel, width=width, tiles_per_seq=tiles_per_seq)
    return pl.pallas_call(
        kern,
        grid=(n // tm,),
        in_specs=[pl.BlockSpec((tm, d), row),
                  pl.BlockSpec((halo, d), lambda i: (jnp.maximum(i * (tm // halo) - 1, 0), 0)),
                  _const_spec(gain.shape), _const_spec(w.shape), _const_spec(conv_w.shape), hist_spec],
        out_specs=[wide, wide, wide, wide, pl.BlockSpec((tm, LANES), row), hist_spec],
        out_shape=[jax.ShapeDtypeStruct((n, width), _F32)] * 4
        + [jax.ShapeDtypeStruct((n, LANES), _F32), jax.ShapeDtypeStruct(hist.shape, _F32)],
        scratch_shapes=[pltpu.VMEM((tm + halo, 4 * C_DK), _F32)] * 2,
        compiler_params=_params("arbitrary"),
        name="conv_proj",
    )(x, x, gain, w, conv_w, hist)


def _ffn_weights(w_in, w_out):
    d, two_ff = w_in.shape
    d_ff = two_ff // 2
    n_chunks = d_ff // FF_CHUNK
    split = lambda w: w.astype(_BF).reshape(d, n_chunks, FF_CHUNK).transpose(1, 0, 2)
    return split(w_in[:, :d_ff]), split(w_in[:, d_ff:]), w_out.astype(_BF).reshape(n_chunks, FF_CHUNK, d)


def _row(v):
    return v.astype(_F32).reshape(1, -1)


def _pad_lanes(v):
    return jnp.pad(_row(v), ((0, 0), (0, LANES - v.shape[-1])))


def kernel(x_prompt, x_sample, cache_attn_k, cache_attn_v, state_hgrn, state_delta, cache_conv, ffn1_norm, ffn1_w_in, ffn1_w_out, mix_norm, ffn2_norm, ffn2_w_in, ffn2_w_out, even_w_in, diff_lambda, diff_norm, hgrn_lb_logits, hgrn_norm, even_w_out, odd_w_in, conv_w, delta_A_log, delta_dt_bias, delta_norm, odd_w_out, final_norm):
    depth, d_model = ffn1_norm.shape
    a_width = cache_attn_k.shape[3] * cache_attn_k.shape[4] * cache_attn_k.shape[5]
    a_heads = cache_attn_k.shape[3]
    b_heads, b_width = state_hgrn.shape[2], state_hgrn.shape[2] * state_hgrn.shape[3]
    c_heads, c_width = state_delta.shape[2], state_delta.shape[2] * state_delta.shape[3]

    lower_bounds = jnp.cumsum(jax.nn.softmax(hgrn_lb_logits.astype(_F32), axis=0), axis=0)
    slopes = 2.0 ** (-8.0 * jnp.arange(1, a_heads + 1, dtype=_F32) / a_heads)
    ffn1 = [_ffn_weights(ffn1_w_in[l], ffn1_w_out[l]) for l in range(depth)]
    ffn2 = [_ffn_weights(ffn2_w_in[l], ffn2_w_out[l]) for l in range(depth)]

    def run(x, past_k, past_v, hgrn0, delta0, conv0):
        b, t, _ = x.shape
        x = x.reshape(b * t, d_model)
        ks, vs, hs, ds, cs = [], [], [], [], []
        mix = []
        for layer in range(depth):
            j = layer // 2
            x = _ffn(x, mix, _row(ffn1_norm[layer]), *ffn1[layer])
            if layer % 2 == 0:
                groups = ((0, a_width, A_QK_DIM ** -0.5 * LOG2E), (a_width, 2 * a_width, 1.0),
                          (2 * a_width, 3 * a_width, 1.0), (3 * a_width, 3 * a_width + 4 * b_width, 1.0))
                q, k, v, hg, k_out, v_out = _proj(
                    x, _row(mix_norm[layer]), even_w_in[j].astype(_BF), groups, (_BF, _F32, _F32, _F32),
                    views={1: (a_heads, 2, A_QK_DIM), 2: (a_heads, A_HEAD)})
                lam_init = 0.8 - 0.6 * math.exp(-0.3 * layer)
                lv = diff_lambda[j].astype(_F32)
                lam = jnp.exp(jnp.sum(lv[0] * lv[1])) - jnp.exp(jnp.sum(lv[2] * lv[3])) + lam_init
                scal = jnp.concatenate([lam.reshape(1), slopes])
                seq = lambda a: a.reshape(b, t, -1)
                if past_k is None:
                    oa = _attention(scal, seq(q), seq(k), seq(v), _row(diff_norm[j]), lam_init)
                else:
                    oa = _attention_step(scal, seq(q), seq(k), seq(v), past_k[j], past_v[j], _row(diff_norm[j]),
                                         lam_init)
                ob, s_new = _hgrn(seq(hg), _row(lower_bounds[j]), _row(hgrn_norm[j]), hgrn0[j])
                ks.append(k_out.reshape(b, t, a_heads, 2, A_QK_DIM))
                vs.append(v_out.reshape(b, t, a_heads, A_HEAD))
                hs.append(s_new)
                w_out = even_w_out[j].astype(_BF)
                mix = [(oa.reshape(b * t, a_width), w_out[:a_width]), (ob.reshape(b * t, b_width), w_out[a_width:])]
            else:
                w_in = odd_w_in[j]
                w_cat = jnp.concatenate([w_in[:, :4 * c_width],
                                         jnp.pad(w_in[:, 4 * c_width:], ((0, 0), (0, LANES - 2 * c_heads)))],
                                        axis=1).astype(_BF)
                q, k, v, gate, ab, c_new = _conv_proj(x, _row(mix_norm[layer]), w_cat, conv_w[j].astype(_F32),
                                                      conv0[j].astype(_F32), t)
                seq = lambda a: a.reshape(b, t, -1)
                o, s_new = _delta(seq(q), seq(k), seq(v), seq(gate), seq(ab), _pad_lanes(delta_A_log[j]),
                                  _pad_lanes(delta_dt_bias[j]), _row(delta_norm[j]), delta0[j])
                ds.append(s_new)
                cs.append(c_new)
                mix = [(o.reshape(b * t, c_width), odd_w_out[j].astype(_BF))]
            final = _row(final_norm) if layer == depth - 1 else None
            x = _ffn(x, mix, _row(ffn2_norm[layer]), *ffn2[layer], final_gain=final)
            mix = []
        return (x.reshape(b, t, d_model), jnp.stack(ks, axis=0), jnp.stack(vs, axis=0),
                jnp.stack(hs, axis=0), jnp.stack(ds, axis=0), jnp.stack(cs, axis=0))

    bp = x_prompt.shape[0]
    n_even, n_odd = state_hgrn.shape[0], state_delta.shape[0]
    zeros_h = jnp.zeros((n_even, bp) + state_hgrn.shape[2:], _F32)
    zeros_d = jnp.zeros((n_odd, bp) + state_delta.shape[2:], _F32)
    zeros_c = jnp.zeros((n_odd, bp) + cache_conv.shape[2:], x_prompt.dtype)
    y_p, k_p, v_p, h_p, d_p, c_p = run(x_prompt, None, None, zeros_h, zeros_d, zeros_c)
    y_s, k_s, v_s, h_s, d_s, c_s = run(x_sample, cache_attn_k, cache_attn_v, state_hgrn, state_delta, cache_conv)
    return (y_p, y_s, k_p, v_p, h_p, d_p, c_p, k_s, v_s, h_s, d_s, c_s)
```

```python
import functools
import math

import jax
import jax.numpy as jnp
import ml_dtypes
import numpy as np
from jax import lax
from jax.experimental import pallas as pl
from jax.experimental.pallas import tpu as pltpu

CHUNK = 64
A_QK_DIM = 64
A_HEAD = 2 * A_QK_DIM
B_DK = 128
C_DK = 128
CONV_W = 4
NORM_EPS = 1e-6
DIFF_NORM_EPS = 1e-5
L2_EPS = 1e-6
LOG2E = math.log2(math.e)
LOG2E_HI = float(np.float32(LOG2E).astype(ml_dtypes.bfloat16))
LOG2E_LO = float(np.float32(LOG2E - LOG2E_HI).astype(ml_dtypes.bfloat16))

LANES = 128
HGRN_BLOCK = 16
FF_CHUNK = 256
TOKEN_TILE = 512
ATTN_TILE = 512
DELTA_CHUNKS = 4
HGRN_CHUNKS = 4
VMEM_LIMIT = 56 * 1024 * 1024

_BF = jnp.bfloat16
_F32 = jnp.float32
_HI = lax.Precision.HIGHEST
_NT = (((1,), (1,)), ((), ()))
_TN = (((0,), (0,)), ((), ()))


def _params(*sem):
    return pltpu.CompilerParams(dimension_semantics=sem, vmem_limit_bytes=VMEM_LIMIT)


def _rms(x, gain, eps):
    return x * lax.rsqrt(jnp.mean(x * x, axis=-1, keepdims=True) + eps) * gain


def _silu(x):
    return x * jax.nn.sigmoid(x)


def _const_spec(shape):
    nd = len(shape)
    return pl.BlockSpec(shape, lambda *_: (0,) * nd)


def _mm(a, b):
    return jnp.dot(a.astype(_BF), b.astype(_BF), preferred_element_type=_F32)


def _ffn_kernel(*refs, n_mix, final):
    refs = list(refs)
    x_ref = refs.pop(0)
    mix = [(refs.pop(0), refs.pop(0)) for _ in range(n_mix)]
    g_ref, wi_ref, wo_ref = refs[:3]
    refs = refs[3:]
    fg_ref = refs.pop(0) if final else None
    o_ref, acc_ref = refs
    d_ff = wo_ref.shape[0]

    x = x_ref[...]
    for a_ref, w_ref in mix:
        x = x + jnp.dot(a_ref[...], w_ref[...], preferred_element_type=_F32)
    o_ref[...] = x
    h = _rms(x, g_ref[...], NORM_EPS).astype(_BF)
    for c in range(0, d_ff, FF_CHUNK):
        gate = jnp.dot(h, wi_ref[:, c:c + FF_CHUNK], preferred_element_type=_F32)
        up = jnp.dot(h, wi_ref[:, d_ff + c:d_ff + c + FF_CHUNK], preferred_element_type=_F32)
        act = (_silu(gate) * up).astype(_BF)
        part = jnp.dot(act, wo_ref[c:c + FF_CHUNK, :], preferred_element_type=_F32)
        if c == 0:
            acc_ref[...] = part
        else:
            acc_ref[...] += part
    y = o_ref[...] + 0.5 * acc_ref[...]
    if final:
        y = _rms(y, fg_ref[...], NORM_EPS)
    o_ref[...] = y


def _ffn(x, mix, gain, wi, wo, final_gain=None):
    n, d = x.shape
    tm = min(TOKEN_TILE, n)
    assert wo.shape[0] % FF_CHUNK == 0
    row = lambda i: (i, 0)
    args, specs = [x], [pl.BlockSpec((tm, d), row)]
    for a, w in mix:
        args += [a, w]
        specs += [pl.BlockSpec((tm, a.shape[1]), row), _const_spec(w.shape)]
    args += [gain, wi, wo]
    specs += [_const_spec(gain.shape), _const_spec(wi.shape), _const_spec(wo.shape)]
    if final_gain is not None:
        args.append(final_gain)
        specs.append(_const_spec(final_gain.shape))
    kern = functools.partial(_ffn_kernel, n_mix=len(mix), final=final_gain is not None)
    return pl.pallas_call(
        kern,
        grid=(n // tm,),
        in_specs=specs,
        out_specs=pl.BlockSpec((tm, d), row),
        out_shape=jax.ShapeDtypeStruct((n, d), _F32),
        scratch_shapes=[pltpu.VMEM((tm, d), _F32)],
        compiler_params=_params("parallel"),
        name="ffn",
    )(*args)


def _proj_kernel(x_ref, g_ref, w_ref, *out_refs, groups, views):
    h = _rms(x_ref[...], g_ref[...], NORM_EPS).astype(_BF)
    view_refs = dict(zip(views, out_refs[len(groups):]))
    for gi, (o_ref, (c0, c1, scale)) in enumerate(zip(out_refs, groups)):
        for s in range(c0, c1, 512):
            e = min(s + 512, c1)
            r = jnp.dot(h, w_ref[:, s:e], preferred_element_type=_F32)
            if scale != 1.0:
                r = r * scale
            o_ref[:, s - c0:e - c0] = r.astype(o_ref.dtype)
            if gi in views:
                dims = views[gi]
                width = dims[-1]
                for j in range((e - s) // width):
                    idx = np.unravel_index((s - c0) // width + j, dims[:-1])
                    view_refs[gi][(slice(None),) + tuple(int(i) for i in idx) + (slice(None),)] = (
                        r[:, j * width:(j + 1) * width])


def _proj(x, gain, w, groups, dtypes, views=None):
    n, d = x.shape
    tm = min(TOKEN_TILE, n)
    views = views or {}
    row = lambda i: (i, 0)
    lead = lambda nd: (lambda i: (i,) + (0,) * nd)
    return pl.pallas_call(
        functools.partial(_proj_kernel, groups=groups, views=views),
        grid=(n // tm,),
        in_specs=[pl.BlockSpec((tm, d), row), _const_spec(gain.shape), _const_spec(w.shape)],
        out_specs=[pl.BlockSpec((tm, c1 - c0), row) for c0, c1, _ in groups]
        + [pl.BlockSpec((tm,) + dims, lead(len(dims))) for dims in views.values()],
        out_shape=[jax.ShapeDtypeStruct((n, c1 - c0), dt) for (c0, c1, _), dt in zip(groups, dtypes)]
        + [jax.ShapeDtypeStruct((n,) + dims, _F32) for dims in views.values()],
        compiler_params=_params("parallel"),
        name="proj",
    )(x, gain, w)


def _lane_bcast(m, n):
    if n <= LANES:
        return m[:, :n]
    return jnp.concatenate([m] * (n // LANES), axis=1)


def _attn_kernel(scal_ref, q_ref, k_ref, v_ref, gain_ref, o_ref, m_ref, l_ref, acc_ref, ka_ref, va_ref, diag_ref,
                 sa_ref, sb_ref, *, tk, lam_init):
    head = pl.program_id(1)
    qi = pl.program_id(2)
    lam = scal_ref[0]
    slope = scal_ref[1 + head]
    tq = nk = tk

    def stage(k_rows, v_rows, pos0, dst, n):
        pos = pos0 + lax.broadcasted_iota(jnp.int32, (n, LANES), 0)
        lane = lax.broadcasted_iota(jnp.int32, (n, LANES), 1)
        hi = (slope * CHUNK) * jnp.right_shift(pos, 6).astype(_F32)
        lo = slope * jnp.bitwise_and(pos, CHUNK - 1).astype(_F32)
        pair = jnp.right_shift(lane, 1)
        for mi in range(2):
            a0 = A_QK_DIM * (1 - mi) // 2
            keys = jnp.where(pair == a0, hi, jnp.where(pair == a0 + 1, lo, k_rows))
            ka_ref[mi, pl.ds(dst, n), :] = keys.astype(_BF)
        va_ref[pl.ds(dst, n), :] = jnp.concatenate([v_rows.astype(_BF), jnp.ones((n, LANES), _BF)], axis=1)

    @pl.when(qi == 0)
    def _():
        r = lax.broadcasted_iota(jnp.int32, (tq, nk), 0)
        c = lax.broadcasted_iota(jnp.int32, (tq, nk), 1)
        ahead = (-2.0 * LOG2E * slope) * jnp.maximum(c - r, 0).astype(_F32)
        diag_ref[...] = jnp.where((c // CHUNK) <= (r // CHUNK), ahead, -jnp.inf)

    own = pl.multiple_of(qi * tk, tk)
    stage(k_ref[0, pl.ds(own, tk), :], v_ref[0, pl.ds(own, tk), :], own, own, tk)

    q = q_ref[0]
    lane = lax.broadcasted_iota(jnp.int32, q.shape, 1)
    log2e = jnp.where(jnp.bitwise_and(lane, 1) == 0, LOG2E_HI, LOG2E_LO).astype(_BF)
    zero = jnp.zeros_like(q)
    quad = jnp.right_shift(lane, 2)
    q_maps = (jnp.where(lane < A_QK_DIM, q, jnp.where(quad == A_QK_DIM // 4, log2e, zero)),
              jnp.where(lane >= A_QK_DIM, q, jnp.where(quad == 0, log2e, zero)))

    m_ref[...] = jnp.full(m_ref.shape, -jnp.inf, _F32)
    l_ref[...] = jnp.zeros(l_ref.shape, _F32)
    acc_ref[...] = jnp.zeros(acc_ref.shape, _F32)

    def scores(off, n, slot, extra):
        for mi in range(2):
            s = lax.dot_general(q_maps[mi], ka_ref[mi, pl.ds(off, n), :], _NT, preferred_element_type=_F32)
            (sb_ref if slot else sa_ref)[mi, :, 0:n] = s if extra is None else s + extra

    def consume(off, n, slot):
        v_aug = va_ref[pl.ds(off, n), :]
        p, alpha = [], []
        for mi in range(2):
            s = (sb_ref if slot else sa_ref)[mi, :, 0:n]
            m_prev = m_ref[mi]
            m_next = jnp.maximum(m_prev, jnp.max(s, axis=1, keepdims=True))
            p.append(jnp.exp2(s - _lane_bcast(m_next, n)).astype(_BF))
            alpha.append(jnp.exp2(m_prev - m_next))
            m_ref[mi] = m_next
        for mi in range(2):
            pv = jnp.dot(p[mi], v_aug, preferred_element_type=_F32)
            acc_ref[mi] = alpha[mi] * acc_ref[mi] + pv[:, :LANES]
            l_ref[mi] = alpha[mi] * l_ref[mi] + pv[:, LANES:]

    A, B = 0, 1
    tile_off = lambda i: pl.multiple_of(jnp.where(i == 0, qi, i - 1) * tk, tk)
    scores(own, tk, A, diag_ref[...])

    def pair(k, carry):
        scores(tile_off(2 * k + 1), tk, B, None)
        consume(tile_off(2 * k), tk, A)
        scores(tile_off(2 * k + 2), tk, A, None)
        consume(tile_off(2 * k + 1), tk, B)
        return carry

    lax.fori_loop(0, qi // 2, pair, 0)

    @pl.when(qi % 2 == 0)
    def _():
        consume(tile_off(qi), tk, A)

    @pl.when(qi % 2 == 1)
    def _():
        scores(tile_off(qi), tk, B, None)
        consume(tile_off(qi - 1), tk, A)
        consume(tile_off(qi), tk, B)

    o = acc_ref[0] / l_ref[0] - lam * (acc_ref[1] / l_ref[1])
    o = _rms(o, gain_ref[...], DIFF_NORM_EPS) * (1.0 - lam_init)
    o_ref[0] = o.astype(o_ref.dtype)


def _attention(scal, q, k, v, gain, lam_init):
    b, t, width = q.shape
    heads = width // A_HEAD
    tk = min(ATTN_TILE, t)
    assert t % tk == 0 and tk % CHUNK == 0
    tile = lambda bi, hi, qi: (bi, qi, hi)
    whole = lambda bi, hi, qi: (bi, 0, hi)
    kern = functools.partial(_attn_kernel, tk=tk, lam_init=lam_init)
    return pl.pallas_call(
        kern,
        grid=(b, heads, t // tk),
        in_specs=[pl.BlockSpec(memory_space=pltpu.SMEM),
                  pl.BlockSpec((1, tk, A_HEAD), tile),
                  pl.BlockSpec((1, t, A_HEAD), whole),
                  pl.BlockSpec((1, t, A_HEAD), whole),
                  _const_spec(gain.shape)],
        out_specs=pl.BlockSpec((1, tk, A_HEAD), tile),
        out_shape=jax.ShapeDtypeStruct((b, t, width), _BF),
        scratch_shapes=[pltpu.VMEM((2, tk, LANES), _F32)] * 3
        + [pltpu.VMEM((2, t, LANES), _BF), pltpu.VMEM((t, 2 * LANES), _BF), pltpu.VMEM((tk, tk), _F32),
           pltpu.VMEM((2, tk, tk), _F32), pltpu.VMEM((2, tk, tk), _F32)],
        compiler_params=_params("parallel", "parallel", "arbitrary"),
        name="attn",
    )(scal, q, k, v, gain)


def _attn_step_kernel(scal_ref, q_ref, k_ref, v_ref, pk_ref, pv_ref, gain_ref, o_ref, m_ref, l_ref, acc_ref,
                      *, heads, tk, past_len, lam_init):
    j = pl.program_id(1)
    lam = scal_ref[0]
    tq = q_ref.shape[1]
    maps = [(h, mi) for h in range(heads) for mi in range(2)]

    @pl.when(j == 0)
    def _():
        m_ref[...] = jnp.full(m_ref.shape, -jnp.inf, _F32)
        l_ref[...] = jnp.zeros(l_ref.shape, _F32)
        acc_ref[...] = jnp.zeros(acc_ref.shape, _F32)

    q = q_ref[0]
    q_maps = [q[:, (2 * h + mi) * A_QK_DIM:(2 * h + mi + 1) * A_QK_DIM] for h, mi in maps]

    def update(keys, values, bias):
        n = keys[0].shape[1]
        s = [jnp.dot(q_maps[i], keys[i].astype(_BF), preferred_element_type=_F32) + bias[maps[i][0]]
             for i in range(len(maps))]
        v_aug = [jnp.concatenate([v.astype(_BF), jnp.ones((n, LANES), _BF)], axis=1) for v in values]
        p, alpha = [], []
        for i in range(len(maps)):
            m_prev = m_ref[i]
            m_next = jnp.maximum(m_prev, jnp.max(s[i], axis=1, keepdims=True))
            p.append(jnp.exp2(s[i] - _lane_bcast(m_next, n)).astype(_BF))
            alpha.append(jnp.exp2(m_prev - m_next))
            m_ref[i] = m_next
        for i, (h, _) in enumerate(maps):
            pv = jnp.dot(p[i], v_aug[h], preferred_element_type=_F32)
            acc_ref[i] = alpha[i] * acc_ref[i] + pv[:, :LANES]
            l_ref[i] = alpha[i] * l_ref[i] + pv[:, LANES:]

    r = lax.broadcasted_iota(jnp.int32, (tq, tk), 0)
    c = lax.broadcasted_iota(jnp.int32, (tq, tk), 1)
    ahead = (r - c + (past_len - j * tk)).astype(_F32)
    update([pk_ref[0, h, mi] for h, mi in maps],
           [pv_ref[0, pl.ds(h, tk, stride=heads), :] for h in range(heads)],
           [(-LOG2E * scal_ref[1 + h]) * ahead for h in range(heads)])

    @pl.when(j == pl.num_programs(1) - 1)
    def _():
        r = lax.broadcasted_iota(jnp.int32, (tq, tq), 0)
        c = lax.broadcasted_iota(jnp.int32, (tq, tq), 1)
        dist = jnp.abs(r - c).astype(_F32)
        visible = ((past_len + c) // CHUNK) <= ((past_len + r) // CHUNK)
        k_cur, v_cur = k_ref[0], v_ref[0]
        update([k_cur[:, (2 * h + mi) * A_QK_DIM:(2 * h + mi + 1) * A_QK_DIM].T for h, mi in maps],
               [v_cur[:, h * A_HEAD:(h + 1) * A_HEAD] for h in range(heads)],
               [jnp.where(visible, (-LOG2E * scal_ref[1 + h]) * dist, -jnp.inf) for h in range(heads)])
        for h in range(heads):
            o = acc_ref[2 * h] / l_ref[2 * h] - lam * (acc_ref[2 * h + 1] / l_ref[2 * h + 1])
            o = _rms(o, gain_ref[...], DIFF_NORM_EPS) * (1.0 - lam_init)
            o_ref[0, :, h * A_HEAD:(h + 1) * A_HEAD] = o.astype(o_ref.dtype)


def _attention_step(scal, q, k, v, past_k, past_v, gain, lam_init):
    b, t, width = q.shape
    heads = width // A_HEAD
    past_len = past_k.shape[1]
    tk = min(ATTN_TILE, past_len)
    assert past_len % tk == 0 and t <= CHUNK
    pk = jnp.transpose(past_k, (0, 2, 3, 4, 1))
    pv = past_v.reshape(b, past_len * heads, A_HEAD)
    cur = pl.BlockSpec((1, t, width), lambda bi, j: (bi, 0, 0))
    kern = functools.partial(_attn_step_kernel, heads=heads, tk=tk, past_len=past_len, lam_init=lam_init)
    return pl.pallas_call(
        kern,
        grid=(b, past_len // tk),
        in_specs=[pl.BlockSpec(memory_space=pltpu.SMEM), cur, cur, cur,
                  pl.BlockSpec((1, heads, 2, A_QK_DIM, tk), lambda bi, j: (bi, 0, 0, 0, j)),
                  pl.BlockSpec((1, tk * heads, A_HEAD), lambda bi, j: (bi, j, 0)),
                  _const_spec(gain.shape)],
        out_specs=cur,
        out_shape=jax.ShapeDtypeStruct((b, t, width), _BF),
        scratch_shapes=[pltpu.VMEM((2 * heads, t, LANES), _F32)] * 3,
        compiler_params=_params("parallel", "arbitrary"),
        name="attn_step",
    )(scal, q, k, v, pk, pv, gain)


def _hgrn_kernel(hg_ref, lb_ref, gain_ref, s0_ref, ob_ref, sout_ref, st_ref, row_ref, w_ref, r_ref, *, L, heads,
                 nc):
    t = pl.program_id(1)
    width = heads * B_DK
    hs = range(heads)
    us = range(nc * heads)

    @pl.when(t == 0)
    def _():
        for h in hs:
            st_ref[h] = s0_ref[0, h].T

    rr = lax.broadcasted_iota(jnp.int32, (L, L), 0)
    cc = lax.broadcasted_iota(jnp.int32, (L, L), 1)
    tri = (rr >= cc).astype(_F32)
    ones = jnp.ones((LANES, LANES), _BF)
    blk, half = HGRN_BLOCK, HGRN_BLOCK // 2
    nb = L // blk
    blk_row = lax.broadcasted_iota(jnp.int32, (blk, LANES), 0)
    half_row = lax.broadcasted_iota(jnp.int32, (half, LANES), 0)
    rows = lambda x, a, b: x[a:b]

    def operand(g, u):
        c, h = divmod(u, heads)
        return hg_ref[0, c * L:(c + 1) * L, g * width + h * B_DK:g * width + (h + 1) * B_DK]

    qb = [operand(0, u) for u in us]
    ib = [operand(2, u) for u in us]
    f = []
    for u in us:
        lb = lb_ref[:, (u % heads) * B_DK:(u % heads + 1) * B_DK]
        f.append(lb + (1.0 - lb) * jax.nn.sigmoid(operand(1, u)))
    kk = [1.0 - f[u] for u in us]
    cb = [jnp.dot(tri, jnp.log2(f[u]), precision=_HI, preferred_element_type=_F32) for u in us]
    st = [st_ref[h] for h in hs]
    o_acc = [None] * len(us)
    for c in range(nc):
        for h in hs:
            u = c * heads + h
            o_acc[u] = lax.dot_general((qb[u] * jnp.exp2(cb[u])).astype(_BF), st[h].astype(_BF), _NT,
                                       preferred_element_type=_F32)
        for h in hs:
            u = c * heads + h
            cl = cb[u][L - 1:L, :]
            k_tail = kk[u] * jnp.exp2(cl - cb[u])
            st[h] = st[h] * jnp.exp2(cl) + lax.dot_general(ib[u].astype(_BF), k_tail.astype(_BF), _TN,
                                                           preferred_element_type=_F32)
    for h in hs:
        st_ref[h] = st[h]
    for h in us:
        row_ref[h, 0] = cb[h]
        row_ref[h, 1] = cb[h] - jnp.log2(jnp.maximum(kk[h], 0.0))
        row_ref[h, 2] = ib[h]

    if nb > 1:
        seg_start = [blk * i * (i - 1) // 2 for i in range(1, nb + 1)]
        n_keys = seg_start[-1]
        qr = lax.broadcasted_iota(jnp.int32, (L - blk, n_keys), 0)
        kc = lax.broadcasted_iota(jnp.int32, (L - blk, n_keys), 1)
        seg = jnp.zeros((L - blk, n_keys), jnp.int32)
        for i in range(2, nb):
            seg = jnp.where(kc >= seg_start[i - 1], i - 1, seg)
        own = seg == qr // blk
        scores = []
        for h in us:
            ref = [row_ref[h, 0, blk * i - 1:blk * i, :] for i in range(1, nb)]
            q_t = jnp.concatenate([rows(qb[h], blk * i, blk * (i + 1))
                                   * jnp.exp2(rows(cb[h], blk * i, blk * (i + 1)) - ref[i - 1])
                                   for i in range(1, nb)], axis=0)
            k_t = jnp.concatenate([rows(kk[h], 0, blk * i) * jnp.exp2(ref[i - 1] - rows(cb[h], 0, blk * i))
                                   for i in range(1, nb)], axis=0)
            scores.append(lax.dot_general(q_t.astype(_BF), k_t.astype(_BF), _NT, preferred_element_type=_F32))
        for h in us:
            v_t = jnp.concatenate([rows(ib[h], 0, blk * i) for i in range(1, nb)], axis=0)
            o_off = _mm(jnp.where(own, scores[h], 0.0), v_t)
            o_acc[h] = jnp.concatenate([rows(o_acc[h], 0, blk), rows(o_acc[h], blk, L) + o_off], axis=0)

    for h in us:
        pieces = []
        for i in range(nb):
            cb_i, q_i = rows(cb[h], blk * i, blk * (i + 1)), rows(qb[h], blk * i, blk * (i + 1))
            for sl in range(blk):
                s = blk * i + sl
                cbs = row_ref[h, 1, s:s + 1, :]
                if sl < half:
                    d = jnp.where(blk_row >= sl, cb_i - cbs, -jnp.inf)
                    pieces.append(q_i * jnp.exp2(d))
                else:
                    d = jnp.where(half_row >= sl - half, cb_i[half:] - cbs, -jnp.inf)
                    pieces.append(q_i[half:] * jnp.exp2(d))
        w_ref[h] = jnp.concatenate(pieces, axis=0).astype(_BF)
        r_ref[h] = jnp.dot(w_ref[h], ones, preferred_element_type=_F32)

    for h in us:
        o8 = [rows(o_acc[h], half * j, half * (j + 1)) for j in range(L // half)]
        off = 0
        for i in range(nb):
            for sl in range(blk):
                vs = row_ref[h, 2, blk * i + sl:blk * i + sl + 1, :]
                if sl < half:
                    o8[2 * i] = o8[2 * i] + r_ref[h, off:off + half, :] * vs
                    off += half
                o8[2 * i + 1] = o8[2 * i + 1] + r_ref[h, off:off + half, :] * vs
                off += half
        o = jnp.concatenate(o8, axis=0)
        ob = _rms(o, gain_ref[...], NORM_EPS) * _silu(operand(3, h))
        c, hd = divmod(h, heads)
        ob_ref[0, c * L:(c + 1) * L, hd * B_DK:(hd + 1) * B_DK] = ob.astype(ob_ref.dtype)

    @pl.when(t == pl.num_programs(1) - 1)
    def _():
        for h in hs:
            sout_ref[0, h] = st_ref[h].T


def _hgrn(hg, lb, gain, s0):
    b, t, four_w = hg.shape
    width = four_w // 4
    heads = width // B_DK
    L = min(CHUNK, t)
    half = HGRN_BLOCK // 2
    n_rows = (L // HGRN_BLOCK) * (half * HGRN_BLOCK + half * half)
    nc = HGRN_CHUNKS if t % (HGRN_CHUNKS * L) == 0 else 1
    kern = functools.partial(_hgrn_kernel, L=L, heads=heads, nc=nc)
    state_spec = pl.BlockSpec((1, heads, B_DK, B_DK), lambda bi, ti: (bi, 0, 0, 0))
    return pl.pallas_call(
        kern,
        grid=(b, t // (nc * L)),
        in_specs=[pl.BlockSpec((1, nc * L, four_w), lambda bi, ti: (bi, ti, 0)),
                  _const_spec(lb.shape), _const_spec(gain.shape), state_spec],
        out_specs=[pl.BlockSpec((1, nc * L, width), lambda bi, ti: (bi, ti, 0)), state_spec],
        out_shape=[jax.ShapeDtypeStruct((b, t, width), _BF),
                   jax.ShapeDtypeStruct((b, heads, B_DK, B_DK), _F32)],
        scratch_shapes=[pltpu.VMEM((heads, B_DK, B_DK), _F32),
                        pltpu.VMEM((nc * heads, 3, L, LANES), _F32),
                        pltpu.VMEM((nc * heads, n_rows, LANES), _BF),
                        pltpu.VMEM((nc * heads, n_rows, LANES), _F32)],
        compiler_params=_params("parallel", "arbitrary"),
        name="hgrn",
    )(hg, lb, gain, s0)


def _softplus(x):
    return jnp.maximum(x, 0.0) + jnp.log1p(jnp.exp(-jnp.abs(x)))


def _delta_kernel(q_ref, k_ref, v_ref, gate_ref, ab_ref, alog_ref, dtb_ref, gain_ref, s0_ref,
                  o_ref, sout_ref, s_ref, *, L, heads, nc):
    t = pl.program_id(1)
    hs = range(heads)
    pairs = [(c, h) for c in range(nc) for h in hs]

    @pl.when(t == 0)
    def _():
        for h in hs:
            s_ref[h] = s0_ref[0, h]

    rr = lax.broadcasted_iota(jnp.int32, (L, L), 0)
    cc = lax.broadcasted_iota(jnp.int32, (L, L), 1)
    lower = (rr >= cc).astype(_F32)
    upper = (rr <= cc).astype(_F32)

    ab = ab_ref[0]
    g_all = -jnp.exp(alog_ref[...]) * _softplus(ab + dtb_ref[...])
    beta_all = jax.nn.sigmoid(ab)
    chunk = lambda x, c: x[c * L:(c + 1) * L]
    gam_col = [jnp.dot(lower, chunk(g_all, c), precision=_HI, preferred_element_type=_F32) for c in range(nc)]
    gam_row = [lax.dot_general(chunk(g_all, c), upper, _TN, precision=_HI, preferred_element_type=_F32)
               for c in range(nc)]

    q, k, v, gc, bc, d_incl, d_strict = {}, {}, {}, {}, {}, {}, {}
    for c, h in pairs:
        p = (c, h)
        q[p], k[p], v[p] = (r[0, c * L:(c + 1) * L, h * C_DK:(h + 1) * C_DK] for r in (q_ref, k_ref, v_ref))
        gc[p] = gam_col[c][:, h:h + 1]
        bc[p] = chunk(beta_all, c)[:, heads + h:heads + h + 1]
        d_incl[p] = jnp.exp(jnp.where(rr >= cc, gc[p] - gam_row[c][h:h + 1, :], -jnp.inf))
        d_strict[p] = jnp.where(rr > cc, d_incl[p], 0.0)

    qkk = {p: lax.dot_general(jnp.concatenate([q[p], k[p]], axis=0).astype(_BF), k[p].astype(_BF), _NT,
                              preferred_element_type=_F32) for p in pairs}

    a = {p: bc[p] * qkk[p][L:] * d_strict[p] for p in pairs}
    n_inv = {p: -a[p] for p in pairs}
    pw = 2
    power = {p: _mm(a[p], a[p]) for p in pairs} if pw < L else None
    while pw < L:
        nxt = {p: _mm(power[p], power[p]) for p in pairs} if 2 * pw < L else None
        n_inv = {p: n_inv[p] + power[p] + _mm(n_inv[p], power[p]) for p in pairs}
        power = nxt
        pw *= 2

    eg = {p: jnp.exp(gc[p]) for p in pairs}
    rhs = {p: jnp.concatenate([k[p] * (bc[p] * eg[p]), v[p] * bc[p]], axis=1) for p in pairs}
    wu = {p: rhs[p] + _mm(n_inv[p], rhs[p]) for p in pairs}

    state = [s_ref[h] for h in hs]
    for c in range(nc):
        ps = [(c, h) for h in hs]
        ws = {p: _mm(jnp.concatenate([wu[p][:, :C_DK], q[p] * eg[p]], axis=0), state[p[1]]) for p in ps}
        v_new = {p: (wu[p][:, C_DK:] - ws[p][:L]).astype(_BF) for p in ps}
        o = {p: ws[p][L:] + _mm(qkk[p][:L] * d_incl[p], v_new[p]) for p in ps}
        for p in ps:
            h = p[1]
            gl = gc[p][L - 1:L, :]
            state[h] = jnp.exp(gl) * state[h] + lax.dot_general(
                (k[p] * jnp.exp(gl - gc[p])).astype(_BF), v_new[p], _TN, preferred_element_type=_F32)
            gate = gate_ref[0, c * L:(c + 1) * L, h * C_DK:(h + 1) * C_DK]
            on = _rms(o[p], gain_ref[...], NORM_EPS) * _silu(gate)
            o_ref[0, c * L:(c + 1) * L, h * C_DK:(h + 1) * C_DK] = on.astype(o_ref.dtype)
    for h in hs:
        s_ref[h] = state[h]

    @pl.when(t == pl.num_programs(1) - 1)
    def _():
        for h in hs:
            sout_ref[0, h] = s_ref[h]


def _delta(q, k, v, gate, ab, alog, dtb, gain, s0):
    b, t, width = q.shape
    heads = width // C_DK
    L = min(CHUNK, t)
    nc = DELTA_CHUNKS if t % (DELTA_CHUNKS * L) == 0 else 1
    rows = nc * L
    kern = functools.partial(_delta_kernel, L=L, heads=heads, nc=nc)
    step = lambda bi, ti: (bi, ti, 0)
    wide = pl.BlockSpec((1, rows, width), step)
    state_spec = pl.BlockSpec((1, heads, C_DK, C_DK), lambda bi, ti: (bi, 0, 0, 0))
    return pl.pallas_call(
        kern,
        grid=(b, t // rows),
        in_specs=[wide, wide, wide, wide, pl.BlockSpec((1, rows, LANES), step), _const_spec(alog.shape),
                  _const_spec(dtb.shape), _const_spec(gain.shape), state_spec],
        out_specs=[wide, state_spec],
        out_shape=[jax.ShapeDtypeStruct((b, t, width), _BF),
                   jax.ShapeDtypeStruct((b, heads, C_DK, C_DK), _F32)],
        scratch_shapes=[pltpu.VMEM((heads, C_DK, C_DK), _F32)],
        compiler_params=_params("parallel", "arbitrary"),
        name="delta",
    )(q, k, v, gate, ab, alog, dtb, gain, s0)


def _conv_proj_kernel(x_ref, halo_ref, g_ref, w_ref, cw_ref, hist_ref,
                      q_ref, k_ref, v_ref, gate_ref, ab_ref, cout_ref, pre_a, pre_b, *, width, tiles_per_seq):
    i = pl.program_id(0)
    tm = x_ref.shape[0]
    pad = halo_ref.shape[0]
    lo = pad - (CONV_W - 1)
    first = (i % tiles_per_seq) == 0
    h_all = _rms(jnp.concatenate([halo_ref[...], x_ref[...]], axis=0), g_ref[...], NORM_EPS).astype(_BF)
    step = 4 * C_DK
    n_chunk = 0
    for g, o_ref in enumerate((q_ref, k_ref, v_ref)):
        for s in range(0, width, step):
            c0 = g * width + s
            pre = pre_b if n_chunk % 2 else pre_a
            n_chunk += 1
            pre[...] = jnp.dot(h_all, w_ref[:, c0:c0 + step], preferred_element_type=_F32)
            pre[lo:pad, :] = jnp.where(first, hist_ref[0, :, c0:c0 + step], pre[lo:pad, :])
            cout_ref[0, :, c0:c0 + step] = pre[tm + lo:tm + pad, :]
            rows = pre[...]
            y = cw_ref[0:1, c0:c0 + step] * rows
            for j in range(1, CONV_W):
                y = cw_ref[j:j + 1, c0:c0 + step] * rows + pltpu.roll(y, 1, axis=0)
            y = _silu(y[pad:])
            for hh in range(step // C_DK):
                yh = y[:, hh * C_DK:(hh + 1) * C_DK]
                if g < 2:
                    yh = yh * lax.rsqrt(jnp.sum(yh * yh, axis=-1, keepdims=True) + L2_EPS)
                    if g == 0:
                        yh = yh * (C_DK ** -0.5)
                o_ref[:, s + hh * C_DK:s + (hh + 1) * C_DK] = yh
    h_main = h_all[pad:]
    for s in range(0, width, step):
        gate_ref[:, s:s + step] = jnp.dot(h_main, w_ref[:, 3 * width + s:3 * width + s + step],
                                          preferred_element_type=_F32)
    ab_ref[...] = jnp.dot(h_main, w_ref[:, 4 * width:4 * width + LANES], preferred_element_type=_F32)


def _conv_proj(x, gain, w, conv_w, hist, seq_len):
    n, d = x.shape
    width = conv_w.shape[1] // 3
    tm = min(TOKEN_TILE, seq_len)
    tiles_per_seq = seq_len // tm
    halo = 8
    row = lambda i: (i, 0)
    wide = pl.BlockSpec((tm, width), row)
    hist_spec = pl.BlockSpec((1, CONV_W - 1, 3 * width), lambda i: (i // tiles_per_seq, 0, 0))
    kern = functools.partial(_conv_proj_kernel, width=width, tiles_per_seq=tiles_per_seq)
    return pl.pallas_call(
        kern,
        grid=(n // tm,),
        in_specs=[pl.BlockSpec((tm, d), row),
                  pl.BlockSpec((halo, d), lambda i: (jnp.maximum(i * (tm // halo) - 1, 0), 0)),
                  _const_spec(gain.shape), _const_spec(w.shape), _const_spec(conv_w.shape), hist_spec],
        out_specs=[wide, wide, wide, wide, pl.BlockSpec((tm, LANES), row), hist_spec],
        out_shape=[jax.ShapeDtypeStruct((n, width), _F32)] * 4
        + [jax.ShapeDtypeStruct((n, LANES), _F32), jax.ShapeDtypeStruct(hist.shape, _F32)],
        scratch_shapes=[pltpu.VMEM((tm + halo, 4 * C_DK), _F32)] * 2,
        compiler_params=_params("arbitrary"),
        name="conv_proj",
    )(x, x, gain, w, conv_w, hist)


def _ffn_weights(w_in, w_out):
    return w_in.astype(_BF), w_out.astype(_BF)


def _row(v):
    return v.astype(_F32).reshape(1, -1)


def _pad_lanes(v):
    return jnp.pad(_row(v), ((0, 0), (0, LANES - v.shape[-1])))


def kernel(x_prompt, x_sample, cache_attn_k, cache_attn_v, state_hgrn, state_delta, cache_conv, ffn1_norm, ffn1_w_in, ffn1_w_out, mix_norm, ffn2_norm, ffn2_w_in, ffn2_w_out, even_w_in, diff_lambda, diff_norm, hgrn_lb_logits, hgrn_norm, even_w_out, odd_w_in, conv_w, delta_A_log, delta_dt_bias, delta_norm, odd_w_out, final_norm):
    depth, d_model = ffn1_norm.shape
    a_width = cache_attn_k.shape[3] * cache_attn_k.shape[4] * cache_attn_k.shape[5]
    a_heads = cache_attn_k.shape[3]
    b_heads, b_width = state_hgrn.shape[2], state_hgrn.shape[2] * state_hgrn.shape[3]
    c_heads, c_width = state_delta.shape[2], state_delta.shape[2] * state_delta.shape[3]

    lower_bounds = jnp.cumsum(jax.nn.softmax(hgrn_lb_logits.astype(_F32), axis=0), axis=0)
    slopes = 2.0 ** (-8.0 * jnp.arange(1, a_heads + 1, dtype=_F32) / a_heads)
    ffn1 = [_ffn_weights(ffn1_w_in[l], ffn1_w_out[l]) for l in range(depth)]
    ffn2 = [_ffn_weights(ffn2_w_in[l], ffn2_w_out[l]) for l in range(depth)]

    def run(x, past_k, past_v, hgrn0, delta0, conv0):
        b, t, _ = x.shape
        x = x.reshape(b * t, d_model)
        ks, vs, hs, ds, cs = [], [], [], [], []
        mix = []
        for layer in range(depth):
            j = layer // 2
            x = _ffn(x, mix, _row(ffn1_norm[layer]), *ffn1[layer])
            if layer % 2 == 0:
                groups = ((0, a_width, A_QK_DIM ** -0.5 * LOG2E), (a_width, 2 * a_width, 1.0),
                          (2 * a_width, 3 * a_width, 1.0), (3 * a_width, 3 * a_width + 4 * b_width, 1.0))
                q, k, v, hg, k_out, v_out = _proj(
                    x, _row(mix_norm[layer]), even_w_in[j].astype(_BF), groups, (_BF, _F32, _F32, _F32),
                    views={1: (a_heads, 2, A_QK_DIM), 2: (a_heads, A_HEAD)})
                lam_init = 0.8 - 0.6 * math.exp(-0.3 * layer)
                lv = diff_lambda[j].astype(_F32)
                lam = jnp.exp(jnp.sum(lv[0] * lv[1])) - jnp.exp(jnp.sum(lv[2] * lv[3])) + lam_init
                scal = jnp.concatenate([lam.reshape(1), slopes])
                seq = lambda a: a.reshape(b, t, -1)
                if past_k is None:
                    oa = _attention(scal, seq(q), seq(k), seq(v), _row(diff_norm[j]), lam_init)
                else:
                    oa = _attention_step(scal, seq(q), seq(k), seq(v), past_k[j], past_v[j], _row(diff_norm[j]),
                                         lam_init)
                ob, s_new = _hgrn(seq(hg), _row(lower_bounds[j]), _row(hgrn_norm[j]), hgrn0[j])
                ks.append(k_out.reshape(b, t, a_heads, 2, A_QK_DIM))
                vs.append(v_out.reshape(b, t, a_heads, A_HEAD))
                hs.append(s_new)
                w_out = even_w_out[j].astype(_BF)
                mix = [(oa.reshape(b * t, a_width), w_out[:a_width]), (ob.reshape(b * t, b_width), w_out[a_width:])]
            else:
                w_in = odd_w_in[j]
                w_cat = jnp.concatenate([w_in[:, :4 * c_width],
                                         jnp.pad(w_in[:, 4 * c_width:], ((0, 0), (0, LANES - 2 * c_heads)))],
                                        axis=1).astype(_BF)
                q, k, v, gate, ab, c_new = _conv_proj(x, _row(mix_norm[layer]), w_cat, conv_w[j].astype(_F32),
                                                      conv0[j].astype(_F32), t)
                seq = lambda a: a.reshape(b, t, -1)
                o, s_new = _delta(seq(q), seq(k), seq(v), seq(gate), seq(ab), _pad_lanes(delta_A_log[j]),
                                  _pad_lanes(delta_dt_bias[j]), _row(delta_norm[j]), delta0[j])
                ds.append(s_new)
                cs.append(c_new)
                mix = [(o.reshape(b * t, c_width), odd_w_out[j].astype(_BF))]
            final = _row(final_norm) if layer == depth - 1 else None
            x = _ffn(x, mix, _row(ffn2_norm[layer]), *ffn2[layer], final_gain=final)
            mix = []
        return (x.reshape(b, t, d_model), jnp.stack(ks, axis=0), jnp.stack(vs, axis=0),
                jnp.stack(hs, axis=0), jnp.stack(ds, axis=0), jnp.stack(cs, axis=0))

    bp = x_prompt.shape[0]
    n_even, n_odd = state_hgrn.shape[0], state_delta.shape[0]
    zeros_h = jnp.zeros((n_even, bp) + state_hgrn.shape[2:], _F32)
    zeros_d = jnp.zeros((n_odd, bp) + state_delta.shape[2:], _F32)
    zeros_c = jnp.zeros((n_odd, bp) + cache_conv.shape[2:], x_prompt.dtype)
    y_p, k_p, v_p, h_p, d_p, c_p = run(x_prompt, None, None, zeros_h, zeros_d, zeros_c)
    y_s, k_s, v_s, h_s, d_s, c_s = run(x_sample, cache_attn_k, cache_attn_v, state_hgrn, state_delta, cache_conv)
    return (y_p, y_s, k_p, v_p, h_p, d_p, c_p, k_s, v_s, h_s, d_s, c_s)
```

```python
import functools
import math

import jax
import jax.numpy as jnp
import ml_dtypes
import numpy as np
from jax import lax
from jax.experimental import pallas as pl
from jax.experimental.pallas import tpu as pltpu

CHUNK = 64
A_QK_DIM = 64
A_HEAD = 2 * A_QK_DIM
B_DK = 128
C_DK = 128
CONV_W = 4
NORM_EPS = 1e-6
DIFF_NORM_EPS = 1e-5
L2_EPS = 1e-6
LOG2E = math.log2(math.e)
LOG2E_HI = float(np.float32(LOG2E).astype(ml_dtypes.bfloat16))
LOG2E_LO = float(np.float32(LOG2E - LOG2E_HI).astype(ml_dtypes.bfloat16))

LANES = 128
HGRN_BLOCK = 16
FF_CHUNK = 256
TOKEN_TILE = 512
ATTN_TILE = 512
DELTA_CHUNKS = 4
HGRN_CHUNKS = 4
VMEM_LIMIT = 56 * 1024 * 1024

_BF = jnp.bfloat16
_F32 = jnp.float32
_HI = lax.Precision.HIGHEST
_NT = (((1,), (1,)), ((), ()))
_TN = (((0,), (0,)), ((), ()))


def _params(*sem):
    return pltpu.CompilerParams(dimension_semantics=sem, vmem_limit_bytes=VMEM_LIMIT)


def _rms(x, gain, eps):
    return x * lax.rsqrt(jnp.mean(x * x, axis=-1, keepdims=True) + eps) * gain


def _silu(x):
    return x * jax.nn.sigmoid(x)


def _const_spec(shape):
    nd = len(shape)
    return pl.BlockSpec(shape, lambda *_: (0,) * nd)


def _mm(a, b):
    return jnp.dot(a.astype(_BF), b.astype(_BF), preferred_element_type=_F32)


def _ffn_kernel(*refs, n_mix, final):
    refs = list(refs)
    x_ref = refs.pop(0)
    mix = [(refs.pop(0), refs.pop(0)) for _ in range(n_mix)]
    g_ref, wi_ref, wo_ref = refs[:3]
    refs = refs[3:]
    fg_ref = refs.pop(0) if final else None
    o_ref, acc_ref = refs
    d_ff = wo_ref.shape[0]

    x = x_ref[...]
    for a_ref, w_ref in mix:
        x = x + jnp.dot(a_ref[...], w_ref[...], preferred_element_type=_F32)
    o_ref[...] = x
    h = _rms(x, g_ref[...], NORM_EPS).astype(_BF)
    for c in range(0, d_ff, FF_CHUNK):
        gate = jnp.dot(h, wi_ref[:, c:c + FF_CHUNK], preferred_element_type=_F32)
        up = jnp.dot(h, wi_ref[:, d_ff + c:d_ff + c + FF_CHUNK], preferred_element_type=_F32)
        act = (_silu(gate) * up).astype(_BF)
        part = jnp.dot(act, wo_ref[c:c + FF_CHUNK, :], preferred_element_type=_F32)
        if c == 0:
            acc_ref[...] = part
        else:
            acc_ref[...] += part
    y = o_ref[...] + 0.5 * acc_ref[...]
    if final:
        y = _rms(y, fg_ref[...], NORM_EPS)
    o_ref[...] = y


def _ffn(x, mix, gain, wi, wo, final_gain=None):
    n, d = x.shape
    tm = min(TOKEN_TILE, n)
    assert wo.shape[0] % FF_CHUNK == 0
    row = lambda i: (i, 0)
    args, specs = [x], [pl.BlockSpec((tm, d), row)]
    for a, w in mix:
        args += [a, w]
        specs += [pl.BlockSpec((tm, a.shape[1]), row), _const_spec(w.shape)]
    args += [gain, wi, wo]
    specs += [_const_spec(gain.shape), _const_spec(wi.shape), _const_spec(wo.shape)]
    if final_gain is not None:
        args.append(final_gain)
        specs.append(_const_spec(final_gain.shape))
    kern = functools.partial(_ffn_kernel, n_mix=len(mix), final=final_gain is not None)
    return pl.pallas_call(
        kern,
        grid=(n // tm,),
        in_specs=specs,
        out_specs=pl.BlockSpec((tm, d), row),
        out_shape=jax.ShapeDtypeStruct((n, d), _F32),
        scratch_shapes=[pltpu.VMEM((tm, d), _F32)],
        compiler_params=_params("parallel"),
        name="ffn",
    )(*args)


def _proj_kernel(x_ref, g_ref, w_ref, *out_refs, groups, views):
    h = _rms(x_ref[...], g_ref[...], NORM_EPS).astype(_BF)
    view_refs = dict(zip(views, out_refs[len(groups):]))
    for gi, (o_ref, (c0, c1, scale)) in enumerate(zip(out_refs, groups)):
        for s in range(c0, c1, 512):
            e = min(s + 512, c1)
            r = jnp.dot(h, w_ref[:, s:e], preferred_element_type=_F32)
            if scale != 1.0:
                r = r * scale
            o_ref[:, s - c0:e - c0] = r.astype(o_ref.dtype)
            if gi in views:
                dims = views[gi]
                width = dims[-1]
                for j in range((e - s) // width):
                    idx = np.unravel_index((s - c0) // width + j, dims[:-1])
                    view_refs[gi][(slice(None),) + tuple(int(i) for i in idx) + (slice(None),)] = (
                        r[:, j * width:(j + 1) * width])


def _proj(x, gain, w, groups, dtypes, views=None):
    n, d = x.shape
    tm = min(TOKEN_TILE, n)
    views = views or {}
    row = lambda i: (i, 0)
    lead = lambda nd: (lambda i: (i,) + (0,) * nd)
    return pl.pallas_call(
        functools.partial(_proj_kernel, groups=groups, views=views),
        grid=(n // tm,),
        in_specs=[pl.BlockSpec((tm, d), row), _const_spec(gain.shape), _const_spec(w.shape)],
        out_specs=[pl.BlockSpec((tm, c1 - c0), row) for c0, c1, _ in groups]
        + [pl.BlockSpec((tm,) + dims, lead(len(dims))) for dims in views.values()],
        out_shape=[jax.ShapeDtypeStruct((n, c1 - c0), dt) for (c0, c1, _), dt in zip(groups, dtypes)]
        + [jax.ShapeDtypeStruct((n,) + dims, _F32) for dims in views.values()],
        compiler_params=_params("parallel"),
        name="proj",
    )(x, gain, w)


def _lane_bcast(m, n):
    if n <= LANES:
        return m[:, :n]
    return jnp.concatenate([m] * (n // LANES), axis=1)


def _attn_kernel(scal_ref, q_ref, k_ref, v_ref, gain_ref, o_ref, m_ref, l_ref, acc_ref, ka_ref, va_ref, diag_ref,
                 sa_ref, sb_ref, *, tk, lam_init):
    head = pl.program_id(1)
    lam = scal_ref[0]
    slope = scal_ref[1 + head]
    n_q = q_ref.shape[1] // tk

    def stage(k_rows, v_rows, pos0, dst, n):
        pos = pos0 + lax.broadcasted_iota(jnp.int32, (n, LANES), 0)
        lane = lax.broadcasted_iota(jnp.int32, (n, LANES), 1)
        hi = (slope * CHUNK) * jnp.right_shift(pos, CHUNK.bit_length() - 1).astype(_F32)
        lo = slope * jnp.bitwise_and(pos, CHUNK - 1).astype(_F32)
        pair = jnp.right_shift(lane, 1)
        for mi in range(2):
            a0 = A_QK_DIM * (1 - mi) // 2
            keys = jnp.where(pair == a0, hi, jnp.where(pair == a0 + 1, lo, k_rows))
            ka_ref[mi, pl.ds(dst, n), :] = keys.astype(_BF)
        va_ref[pl.ds(dst, n), :] = jnp.concatenate([v_rows.astype(_BF), jnp.ones((n, LANES), _BF)], axis=1)

    r = lax.broadcasted_iota(jnp.int32, (tk, tk), 0)
    c = lax.broadcasted_iota(jnp.int32, (tk, tk), 1)
    ahead = (-2.0 * LOG2E * slope) * jnp.maximum(c - r, 0).astype(_F32)
    diag_ref[...] = jnp.where((c // CHUNK) <= (r // CHUNK), ahead, -jnp.inf)

    def q_tile(qi, carry):
        own = pl.multiple_of(qi * tk, tk)
        stage(k_ref[0, pl.ds(own, tk), :], v_ref[0, pl.ds(own, tk), :], own, own, tk)

        q = q_ref[0, pl.ds(own, tk), :]
        lane = lax.broadcasted_iota(jnp.int32, q.shape, 1)
        log2e = jnp.where(jnp.bitwise_and(lane, 1) == 0, LOG2E_HI, LOG2E_LO).astype(_BF)
        zero = jnp.zeros_like(q)
        quad = jnp.right_shift(lane, 2)
        q_maps = (jnp.where(lane < A_QK_DIM, q, jnp.where(quad == A_QK_DIM // 4, log2e, zero)),
                  jnp.where(lane >= A_QK_DIM, q, jnp.where(quad == 0, log2e, zero)))

        m_ref[...] = jnp.full(m_ref.shape, -jnp.inf, _F32)
        l_ref[...] = jnp.zeros(l_ref.shape, _F32)
        acc_ref[...] = jnp.zeros(acc_ref.shape, _F32)

        def scores(off, slot, extra):
            for mi in range(2):
                s = lax.dot_general(q_maps[mi], ka_ref[mi, pl.ds(off, tk), :], _NT, preferred_element_type=_F32)
                (sb_ref if slot else sa_ref)[mi] = s if extra is None else s + extra

        def consume(off, slot):
            v_aug = va_ref[pl.ds(off, tk), :]
            p, alpha = [], []
            for mi in range(2):
                s = (sb_ref if slot else sa_ref)[mi]
                m_prev = m_ref[mi]
                m_next = jnp.maximum(m_prev, jnp.max(s, axis=1, keepdims=True))
                p.append(jnp.exp2(s - _lane_bcast(m_next, tk)).astype(_BF))
                alpha.append(jnp.exp2(m_prev - m_next))
                m_ref[mi] = m_next
            for mi in range(2):
                pv = jnp.dot(p[mi], v_aug, preferred_element_type=_F32)
                acc_ref[mi] = alpha[mi] * acc_ref[mi] + pv[:, :LANES]
                l_ref[mi] = alpha[mi] * l_ref[mi] + pv[:, LANES:]

        A, B = 0, 1
        tile_off = lambda i: pl.multiple_of(jnp.where(i == 0, qi, i - 1) * tk, tk)
        scores(own, A, diag_ref[...])

        def pair(k, carry):
            scores(tile_off(2 * k + 1), B, None)
            consume(tile_off(2 * k), A)
            scores(tile_off(2 * k + 2), A, None)
            consume(tile_off(2 * k + 1), B)
            return carry

        lax.fori_loop(0, qi // 2, pair, 0)

        @pl.when(qi % 2 == 0)
        def _():
            consume(tile_off(qi), A)

        @pl.when(qi % 2 == 1)
        def _():
            scores(tile_off(qi), B, None)
            consume(tile_off(qi - 1), A)
            consume(tile_off(qi), B)

        o = acc_ref[0] / l_ref[0] - lam * (acc_ref[1] / l_ref[1])
        o = _rms(o, gain_ref[...], DIFF_NORM_EPS) * (1.0 - lam_init)
        o_ref[0, pl.ds(own, tk), :] = o.astype(o_ref.dtype)
        return carry

    lax.fori_loop(0, n_q, q_tile, 0)


def _attention(scal, q, k, v, gain, lam_init):
    b, t, width = q.shape
    heads = width // A_HEAD
    tk = min(ATTN_TILE, t)
    assert t % tk == 0 and tk % CHUNK == 0
    assert 8 % heads == 0, "ALiBi slopes 2^(-8 (h+1) / heads) must be powers of two for the bf16 bias lanes"
    whole = lambda bi, hi: (bi, 0, hi)
    kern = functools.partial(_attn_kernel, tk=tk, lam_init=lam_init)
    return pl.pallas_call(
        kern,
        grid=(b, heads),
        in_specs=[pl.BlockSpec(memory_space=pltpu.SMEM),
                  pl.BlockSpec((1, t, A_HEAD), whole),
                  pl.BlockSpec((1, t, A_HEAD), whole),
                  pl.BlockSpec((1, t, A_HEAD), whole),
                  _const_spec(gain.shape)],
        out_specs=pl.BlockSpec((1, t, A_HEAD), whole),
        out_shape=jax.ShapeDtypeStruct((b, t, width), _BF),
        scratch_shapes=[pltpu.VMEM((2, tk, LANES), _F32)] * 3
        + [pltpu.VMEM((2, t, LANES), _BF), pltpu.VMEM((t, 2 * LANES), _BF), pltpu.VMEM((tk, tk), _F32),
           pltpu.VMEM((2, tk, tk), _F32), pltpu.VMEM((2, tk, tk), _F32)],
        compiler_params=_params("parallel", "parallel"),
        name="attn",
    )(scal, q, k, v, gain)


def _attn_step_kernel(scal_ref, q_ref, k_ref, v_ref, pk_ref, pv_ref, gain_ref, o_ref, m_ref, l_ref, acc_ref,
                      *, heads, tk, past_len, lam_init):
    j = pl.program_id(1)
    lam = scal_ref[0]
    tq = q_ref.shape[1]
    maps = [(h, mi) for h in range(heads) for mi in range(2)]

    @pl.when(j == 0)
    def _():
        m_ref[...] = jnp.full(m_ref.shape, -jnp.inf, _F32)
        l_ref[...] = jnp.zeros(l_ref.shape, _F32)
        acc_ref[...] = jnp.zeros(acc_ref.shape, _F32)

    q = q_ref[0]
    q_maps = [q[:, (2 * h + mi) * A_QK_DIM:(2 * h + mi + 1) * A_QK_DIM] for h, mi in maps]

    def update(keys, values, bias):
        n = keys[0].shape[1]
        s = [jnp.dot(q_maps[i], keys[i].astype(_BF), preferred_element_type=_F32) + bias[maps[i][0]]
             for i in range(len(maps))]
        v_aug = [jnp.concatenate([v.astype(_BF), jnp.ones((n, LANES), _BF)], axis=1) for v in values]
        p, alpha = [], []
        for i in range(len(maps)):
            m_prev = m_ref[i]
            m_next = jnp.maximum(m_prev, jnp.max(s[i], axis=1, keepdims=True))
            p.append(jnp.exp2(s[i] - _lane_bcast(m_next, n)).astype(_BF))
            alpha.append(jnp.exp2(m_prev - m_next))
            m_ref[i] = m_next
        for i, (h, _) in enumerate(maps):
            pv = jnp.dot(p[i], v_aug[h], preferred_element_type=_F32)
            acc_ref[i] = alpha[i] * acc_ref[i] + pv[:, :LANES]
            l_ref[i] = alpha[i] * l_ref[i] + pv[:, LANES:]

    r = lax.broadcasted_iota(jnp.int32, (tq, tk), 0)
    c = lax.broadcasted_iota(jnp.int32, (tq, tk), 1)
    ahead = (r - c + (past_len - j * tk)).astype(_F32)
    update([pk_ref[0, h, mi] for h, mi in maps],
           [pv_ref[0, pl.ds(h, tk, stride=heads), :] for h in range(heads)],
           [(-LOG2E * scal_ref[1 + h]) * ahead for h in range(heads)])

    @pl.when(j == pl.num_programs(1) - 1)
    def _():
        r = lax.broadcasted_iota(jnp.int32, (tq, tq), 0)
        c = lax.broadcasted_iota(jnp.int32, (tq, tq), 1)
        dist = jnp.abs(r - c).astype(_F32)
        visible = ((past_len + c) // CHUNK) <= ((past_len + r) // CHUNK)
        k_cur, v_cur = k_ref[0], v_ref[0]
        update([k_cur[:, (2 * h + mi) * A_QK_DIM:(2 * h + mi + 1) * A_QK_DIM].T for h, mi in maps],
               [v_cur[:, h * A_HEAD:(h + 1) * A_HEAD] for h in range(heads)],
               [jnp.where(visible, (-LOG2E * scal_ref[1 + h]) * dist, -jnp.inf) for h in range(heads)])
        for h in range(heads):
            o = acc_ref[2 * h] / l_ref[2 * h] - lam * (acc_ref[2 * h + 1] / l_ref[2 * h + 1])
            o = _rms(o, gain_ref[...], DIFF_NORM_EPS) * (1.0 - lam_init)
            o_ref[0, :, h * A_HEAD:(h + 1) * A_HEAD] = o.astype(o_ref.dtype)


def _attention_step(scal, q, k, v, past_k, past_v, gain, lam_init):
    b, t, width = q.shape
    heads = width // A_HEAD
    past_len = past_k.shape[1]
    tk = min(ATTN_TILE, past_len)
    assert past_len % tk == 0 and t <= CHUNK
    pk = jnp.transpose(past_k, (0, 2, 3, 4, 1))
    pv = past_v.reshape(b, past_len * heads, A_HEAD)
    cur = pl.BlockSpec((1, t, width), lambda bi, j: (bi, 0, 0))
    kern = functools.partial(_attn_step_kernel, heads=heads, tk=tk, past_len=past_len, lam_init=lam_init)
    return pl.pallas_call(
        kern,
        grid=(b, past_len // tk),
        in_specs=[pl.BlockSpec(memory_space=pltpu.SMEM), cur, cur, cur,
                  pl.BlockSpec((1, heads, 2, A_QK_DIM, tk), lambda bi, j: (bi, 0, 0, 0, j)),
                  pl.BlockSpec((1, tk * heads, A_HEAD), lambda bi, j: (bi, j, 0)),
                  _const_spec(gain.shape)],
        out_specs=cur,
        out_shape=jax.ShapeDtypeStruct((b, t, width), _BF),
        scratch_shapes=[pltpu.VMEM((2 * heads, t, LANES), _F32)] * 3,
        compiler_params=_params("parallel", "arbitrary"),
        name="attn_step",
    )(scal, q, k, v, pk, pv, gain)


def _hgrn_kernel(hg_ref, lb_ref, gain_ref, s0_ref, ob_ref, sout_ref, st_ref, row_ref, w_ref, r_ref, *, L, heads,
                 nc):
    t = pl.program_id(1)
    width = heads * B_DK
    hs = range(heads)
    us = range(nc * heads)

    @pl.when(t == 0)
    def _():
        for h in hs:
            st_ref[h] = s0_ref[0, h].T

    rr = lax.broadcasted_iota(jnp.int32, (L, L), 0)
    cc = lax.broadcasted_iota(jnp.int32, (L, L), 1)
    tri = (rr >= cc).astype(_F32)
    ones = jnp.ones((LANES, LANES), _BF)
    blk, half = HGRN_BLOCK, HGRN_BLOCK // 2
    nb = L // blk
    blk_row = lax.broadcasted_iota(jnp.int32, (blk, LANES), 0)
    half_row = lax.broadcasted_iota(jnp.int32, (half, LANES), 0)
    rows = lambda x, a, b: x[a:b]

    def operand(g, u):
        c, h = divmod(u, heads)
        return hg_ref[0, c * L:(c + 1) * L, g * width + h * B_DK:g * width + (h + 1) * B_DK]

    qb = [operand(0, u) for u in us]
    ib = [operand(2, u) for u in us]
    f = []
    for u in us:
        lb = lb_ref[:, (u % heads) * B_DK:(u % heads + 1) * B_DK]
        f.append(lb + (1.0 - lb) * jax.nn.sigmoid(operand(1, u)))
    kk = [1.0 - f[u] for u in us]
    cb = [jnp.dot(tri, jnp.log2(f[u]), precision=_HI, preferred_element_type=_F32) for u in us]
    st = [st_ref[h] for h in hs]
    o_acc = [None] * len(us)
    for c in range(nc):
        for h in hs:
            u = c * heads + h
            o_acc[u] = lax.dot_general((qb[u] * jnp.exp2(cb[u])).astype(_BF), st[h].astype(_BF), _NT,
                                       preferred_element_type=_F32)
        for h in hs:
            u = c * heads + h
            cl = cb[u][L - 1:L, :]
            k_tail = kk[u] * jnp.exp2(cl - cb[u])
            st[h] = st[h] * jnp.exp2(cl) + lax.dot_general(ib[u].astype(_BF), k_tail.astype(_BF), _TN,
                                                           preferred_element_type=_F32)
    for h in hs:
        st_ref[h] = st[h]
    for h in us:
        row_ref[h, 0] = cb[h]
        row_ref[h, 1] = cb[h] - jnp.log2(jnp.maximum(kk[h], 0.0))
        row_ref[h, 2] = ib[h]

    if nb > 1:
        seg_start = [blk * i * (i - 1) // 2 for i in range(1, nb + 1)]
        n_keys = seg_start[-1]
        qr = lax.broadcasted_iota(jnp.int32, (L - blk, n_keys), 0)
        kc = lax.broadcasted_iota(jnp.int32, (L - blk, n_keys), 1)
        seg = jnp.zeros((L - blk, n_keys), jnp.int32)
        for i in range(2, nb):
            seg = jnp.where(kc >= seg_start[i - 1], i - 1, seg)
        own = seg == qr // blk
        scores = []
        for h in us:
            ref = [row_ref[h, 0, blk * i - 1:blk * i, :] for i in range(1, nb)]
            q_t = jnp.concatenate([rows(qb[h], blk * i, blk * (i + 1))
                                   * jnp.exp2(rows(cb[h], blk * i, blk * (i + 1)) - ref[i - 1])
                                   for i in range(1, nb)], axis=0)
            k_t = jnp.concatenate([rows(kk[h], 0, blk * i) * jnp.exp2(ref[i - 1] - rows(cb[h], 0, blk * i))
                                   for i in range(1, nb)], axis=0)
            scores.append(lax.dot_general(q_t.astype(_BF), k_t.astype(_BF), _NT, preferred_element_type=_F32))
        for h in us:
            v_t = jnp.concatenate([rows(ib[h], 0, blk * i) for i in range(1, nb)], axis=0)
            o_off = _mm(jnp.where(own, scores[h], 0.0), v_t)
            o_acc[h] = jnp.concatenate([rows(o_acc[h], 0, blk), rows(o_acc[h], blk, L) + o_off], axis=0)

    for h in us:
        pieces = []
        for i in range(nb):
            cb_i, q_i = rows(cb[h], blk * i, blk * (i + 1)), rows(qb[h], blk * i, blk * (i + 1))
            for sl in range(blk):
                s = blk * i + sl
                cbs = row_ref[h, 1, s:s + 1, :]
                if sl < half:
                    d = jnp.where(blk_row >= sl, cb_i - cbs, -jnp.inf)
                    pieces.append(q_i * jnp.exp2(d))
                else:
                    d = jnp.where(half_row >= sl - half, cb_i[half:] - cbs, -jnp.inf)
                    pieces.append(q_i[half:] * jnp.exp2(d))
        w_ref[h] = jnp.concatenate(pieces, axis=0).astype(_BF)
        r_ref[h] = jnp.dot(w_ref[h], ones, preferred_element_type=_F32)

    for h in us:
        o8 = [rows(o_acc[h], half * j, half * (j + 1)) for j in range(L // half)]
        off = 0
        for i in range(nb):
            for sl in range(blk):
                vs = row_ref[h, 2, blk * i + sl:blk * i + sl + 1, :]
                if sl < half:
                    o8[2 * i] = o8[2 * i] + r_ref[h, off:off + half, :] * vs
                    off += half
                o8[2 * i + 1] = o8[2 * i + 1] + r_ref[h, off:off + half, :] * vs
                off += half
        o = jnp.concatenate(o8, axis=0)
        ob = _rms(o, gain_ref[...], NORM_EPS) * _silu(operand(3, h))
        c, hd = divmod(h, heads)
        ob_ref[0, c * L:(c + 1) * L, hd * B_DK:(hd + 1) * B_DK] = ob.astype(ob_ref.dtype)

    @pl.when(t == pl.num_programs(1) - 1)
    def _():
        for h in hs:
            sout_ref[0, h] = st_ref[h].T


def _hgrn(hg, lb, gain, s0):
    b, t, four_w = hg.shape
    width = four_w // 4
    heads = width // B_DK
    L = min(CHUNK, t)
    half = HGRN_BLOCK // 2
    n_rows = (L // HGRN_BLOCK) * (half * HGRN_BLOCK + half * half)
    nc = HGRN_CHUNKS if t % (HGRN_CHUNKS * L) == 0 else 1
    kern = functools.partial(_hgrn_kernel, L=L, heads=heads, nc=nc)
    state_spec = pl.BlockSpec((1, heads, B_DK, B_DK), lambda bi, ti: (bi, 0, 0, 0))
    return pl.pallas_call(
        kern,
        grid=(b, t // (nc * L)),
        in_specs=[pl.BlockSpec((1, nc * L, four_w), lambda bi, ti: (bi, ti, 0)),
                  _const_spec(lb.shape), _const_spec(gain.shape), state_spec],
        out_specs=[pl.BlockSpec((1, nc * L, width), lambda bi, ti: (bi, ti, 0)), state_spec],
        out_shape=[jax.ShapeDtypeStruct((b, t, width), _BF),
                   jax.ShapeDtypeStruct((b, heads, B_DK, B_DK), _F32)],
        scratch_shapes=[pltpu.VMEM((heads, B_DK, B_DK), _F32),
                        pltpu.VMEM((nc * heads, 3, L, LANES), _F32),
                        pltpu.VMEM((nc * heads, n_rows, LANES), _BF),
                        pltpu.VMEM((nc * heads, n_rows, LANES), _F32)],
        compiler_params=_params("parallel", "arbitrary"),
        name="hgrn",
    )(hg, lb, gain, s0)


def _softplus(x):
    return jnp.maximum(x, 0.0) + jnp.log1p(jnp.exp(-jnp.abs(x)))


def _delta_kernel(q_ref, k_ref, v_ref, gate_ref, ab_ref, alog_ref, dtb_ref, gain_ref, s0_ref,
                  o_ref, sout_ref, s_ref, *, L, heads, nc):
    t = pl.program_id(1)
    hs = range(heads)
    pairs = [(c, h) for c in range(nc) for h in hs]

    @pl.when(t == 0)
    def _():
        for h in hs:
            s_ref[h] = s0_ref[0, h]

    rr = lax.broadcasted_iota(jnp.int32, (L, L), 0)
    cc = lax.broadcasted_iota(jnp.int32, (L, L), 1)
    lower = (rr >= cc).astype(_F32)
    upper = (rr <= cc).astype(_F32)

    ab = ab_ref[0]
    g_all = -jnp.exp(alog_ref[...]) * _softplus(ab + dtb_ref[...])
    beta_all = jax.nn.sigmoid(ab)
    chunk = lambda x, c: x[c * L:(c + 1) * L]
    gam_col = [jnp.dot(lower, chunk(g_all, c), precision=_HI, preferred_element_type=_F32) for c in range(nc)]
    gam_row = [lax.dot_general(chunk(g_all, c), upper, _TN, precision=_HI, preferred_element_type=_F32)
               for c in range(nc)]

    q, k, v, gc, bc, d_incl, d_strict = {}, {}, {}, {}, {}, {}, {}
    for c, h in pairs:
        p = (c, h)
        q[p], k[p], v[p] = (r[0, c * L:(c + 1) * L, h * C_DK:(h + 1) * C_DK] for r in (q_ref, k_ref, v_ref))
        gc[p] = gam_col[c][:, h:h + 1]
        bc[p] = chunk(beta_all, c)[:, heads + h:heads + h + 1]
        d_incl[p] = jnp.exp(jnp.where(rr >= cc, gc[p] - gam_row[c][h:h + 1, :], -jnp.inf))
        d_strict[p] = jnp.where(rr > cc, d_incl[p], 0.0)

    qkk = {p: lax.dot_general(jnp.concatenate([q[p], k[p]], axis=0).astype(_BF), k[p].astype(_BF), _NT,
                              preferred_element_type=_F32) for p in pairs}

    a = {p: bc[p] * qkk[p][L:] * d_strict[p] for p in pairs}
    n_inv = {p: -a[p] for p in pairs}
    pw = 2
    power = {p: _mm(a[p], a[p]) for p in pairs} if pw < L else None
    while pw < L:
        nxt = {p: _mm(power[p], power[p]) for p in pairs} if 2 * pw < L else None
        n_inv = {p: n_inv[p] + power[p] + _mm(n_inv[p], power[p]) for p in pairs}
        power = nxt
        pw *= 2

    eg = {p: jnp.exp(gc[p]) for p in pairs}
    rhs = {p: jnp.concatenate([k[p] * (bc[p] * eg[p]), v[p] * bc[p]], axis=1) for p in pairs}
    wu = {p: rhs[p] + _mm(n_inv[p], rhs[p]) for p in pairs}

    state = [s_ref[h] for h in hs]
    for c in range(nc):
        ps = [(c, h) for h in hs]
        ws = {p: _mm(jnp.concatenate([wu[p][:, :C_DK], q[p] * eg[p]], axis=0), state[p[1]]) for p in ps}
        v_new = {p: (wu[p][:, C_DK:] - ws[p][:L]).astype(_BF) for p in ps}
        o = {p: ws[p][L:] + _mm(qkk[p][:L] * d_incl[p], v_new[p]) for p in ps}
        for p in ps:
            h = p[1]
            gl = gc[p][L - 1:L, :]
            state[h] = jnp.exp(gl) * state[h] + lax.dot_general(
                (k[p] * jnp.exp(gl - gc[p])).astype(_BF), v_new[p], _TN, preferred_element_type=_F32)
            gate = gate_ref[0, c * L:(c + 1) * L, h * C_DK:(h + 1) * C_DK]
            on = _rms(o[p], gain_ref[...], NORM_EPS) * _silu(gate)
            o_ref[0, c * L:(c + 1) * L, h * C_DK:(h + 1) * C_DK] = on.astype(o_ref.dtype)
    for h in hs:
        s_ref[h] = state[h]

    @pl.when(t == pl.num_programs(1) - 1)
    def _():
        for h in hs:
            sout_ref[0, h] = s_ref[h]


def _delta(q, k, v, gate, ab, alog, dtb, gain, s0):
    b, t, width = q.shape
    heads = width // C_DK
    L = min(CHUNK, t)
    nc = DELTA_CHUNKS if t % (DELTA_CHUNKS * L) == 0 else 1
    rows = nc * L
    kern = functools.partial(_delta_kernel, L=L, heads=heads, nc=nc)
    step = lambda bi, ti: (bi, ti, 0)
    wide = pl.BlockSpec((1, rows, width), step)
    state_spec = pl.BlockSpec((1, heads, C_DK, C_DK), lambda bi, ti: (bi, 0, 0, 0))
    return pl.pallas_call(
        kern,
        grid=(b, t // rows),
        in_specs=[wide, wide, wide, wide, pl.BlockSpec((1, rows, LANES), step), _const_spec(alog.shape),
                  _const_spec(dtb.shape), _const_spec(gain.shape), state_spec],
        out_specs=[wide, state_spec],
        out_shape=[jax.ShapeDtypeStruct((b, t, width), _BF),
                   jax.ShapeDtypeStruct((b, heads, C_DK, C_DK), _F32)],
        scratch_shapes=[pltpu.VMEM((heads, C_DK, C_DK), _F32)],
        compiler_params=_params("parallel", "arbitrary"),
        name="delta",
    )(q, k, v, gate, ab, alog, dtb, gain, s0)


def _conv_proj_kernel(x_ref, halo_ref, g_ref, w_ref, cw_ref, hist_ref,
                      q_ref, k_ref, v_ref, gate_ref, ab_ref, cout_ref, pre_a, pre_b, *, width, tiles_per_seq):
    i = pl.program_id(0)
    tm = x_ref.shape[0]
    pad = halo_ref.shape[0]
    lo = pad - (CONV_W - 1)
    first = (i % tiles_per_seq) == 0
    h_all = _rms(jnp.concatenate([halo_ref[...], x_ref[...]], axis=0), g_ref[...], NORM_EPS).astype(_BF)
    step = 4 * C_DK
    n_chunk = 0
    for g, o_ref in enumerate((q_ref, k_ref, v_ref)):
        for s in range(0, width, step):
            c0 = g * width + s
            pre = pre_b if n_chunk % 2 else pre_a
            n_chunk += 1
            pre[...] = jnp.dot(h_all, w_ref[:, c0:c0 + step], preferred_element_type=_F32)
            pre[lo:pad, :] = jnp.where(first, hist_ref[0, :, c0:c0 + step], pre[lo:pad, :])
            cout_ref[0, :, c0:c0 + step] = pre[tm + lo:tm + pad, :]
            rows = pre[...]
            y = cw_ref[0:1, c0:c0 + step] * rows
            for j in range(1, CONV_W):
                y = cw_ref[j:j + 1, c0:c0 + step] * rows + pltpu.roll(y, 1, axis=0)
            y = _silu(y[pad:])
            for hh in range(step // C_DK):
                yh = y[:, hh * C_DK:(hh + 1) * C_DK]
                if g < 2:
                    yh = yh * lax.rsqrt(jnp.sum(yh * yh, axis=-1, keepdims=True) + L2_EPS)
                    if g == 0:
                        yh = yh * (C_DK ** -0.5)
                o_ref[:, s + hh * C_DK:s + (hh + 1) * C_DK] = yh
    h_main = h_all[pad:]
    for s in range(0, width, step):
        gate_ref[:, s:s + step] = jnp.dot(h_main, w_ref[:, 3 * width + s:3 * width + s + step],
                                          preferred_element_type=_F32)
    ab_ref[...] = jnp.dot(h_main, w_ref[:, 4 * width:4 * width + LANES], preferred_element_type=_F32)


def _conv_proj(x, gain, w, conv_w, hist, seq_len):
    n, d = x.shape
    width = conv_w.shape[1] // 3
    tm = min(TOKEN_TILE, seq_len)
    tiles_per_seq = seq_len // tm
    halo = 8
    row = lambda i: (i, 0)
    wide = pl.BlockSpec((tm, width), row)
    hist_spec = pl.BlockSpec((1, CONV_W - 1, 3 * width), lambda i: (i // tiles_per_seq, 0, 0))
    kern = functools.partial(_conv_proj_kernel, width=width, tiles_per_seq=tiles_per_seq)
    return pl.pallas_call(
        kern,
        grid=(n // tm,),
        in_specs=[pl.BlockSpec((tm, d), row),
                  pl.BlockSpec((halo, d), lambda i: (jnp.maximum(i * (tm // halo) - 1, 0), 0)),
                  _const_spec(gain.shape), _const_spec(w.shape), _const_spec(conv_w.shape), hist_spec],
        out_specs=[wide, wide, wide, wide, pl.BlockSpec((tm, LANES), row), hist_spec],
        out_shape=[jax.ShapeDtypeStruct((n, width), _F32)] * 4
        + [jax.ShapeDtypeStruct((n, LANES), _F32), jax.ShapeDtypeStruct(hist.shape, _F32)],
        scratch_shapes=[pltpu.VMEM((tm + halo, 4 * C_DK), _F32)] * 2,
        compiler_params=_params("arbitrary"),
        name="conv_proj",
    )(x, x, gain, w, conv_w, hist)


def _ffn_weights(w_in, w_out):
    return w_in.astype(_BF), w_out.astype(_BF)


def _row(v):
    return v.astype(_F32).reshape(1, -1)


def _pad_lanes(v):
    return jnp.pad(_row(v), ((0, 0), (0, LANES - v.shape[-1])))


def kernel(x_prompt, x_sample, cache_attn_k, cache_attn_v, state_hgrn, state_delta, cache_conv, ffn1_norm, ffn1_w_in, ffn1_w_out, mix_norm, ffn2_norm, ffn2_w_in, ffn2_w_out, even_w_in, diff_lambda, diff_norm, hgrn_lb_logits, hgrn_norm, even_w_out, odd_w_in, conv_w, delta_A_log, delta_dt_bias, delta_norm, odd_w_out, final_norm):
    depth, d_model = ffn1_norm.shape
    a_width = cache_attn_k.shape[3] * cache_attn_k.shape[4] * cache_attn_k.shape[5]
    a_heads = cache_attn_k.shape[3]
    b_heads, b_width = state_hgrn.shape[2], state_hgrn.shape[2] * state_hgrn.shape[3]
    c_heads, c_width = state_delta.shape[2], state_delta.shape[2] * state_delta.shape[3]

    lower_bounds = jnp.cumsum(jax.nn.softmax(hgrn_lb_logits.astype(_F32), axis=0), axis=0)
    slopes = 2.0 ** (-8.0 * jnp.arange(1, a_heads + 1, dtype=_F32) / a_heads)
    ffn1 = [_ffn_weights(ffn1_w_in[l], ffn1_w_out[l]) for l in range(depth)]
    ffn2 = [_ffn_weights(ffn2_w_in[l], ffn2_w_out[l]) for l in range(depth)]

    def run(x, past_k, past_v, hgrn0, delta0, conv0):
        b, t, _ = x.shape
        x = x.reshape(b * t, d_model)
        ks, vs, hs, ds, cs = [], [], [], [], []
        mix = []
        for layer in range(depth):
            j = layer // 2
            x = _ffn(x, mix, _row(ffn1_norm[layer]), *ffn1[layer])
            if layer % 2 == 0:
                groups = ((0, a_width, A_QK_DIM ** -0.5 * LOG2E), (a_width, 2 * a_width, 1.0),
                          (2 * a_width, 3 * a_width, 1.0), (3 * a_width, 3 * a_width + 4 * b_width, 1.0))
                q, k, v, hg, k_out, v_out = _proj(
                    x, _row(mix_norm[layer]), even_w_in[j].astype(_BF), groups, (_BF, _F32, _F32, _F32),
                    views={1: (a_heads, 2, A_QK_DIM), 2: (a_heads, A_HEAD)})
                lam_init = 0.8 - 0.6 * math.exp(-0.3 * layer)
                lv = diff_lambda[j].astype(_F32)
                lam = jnp.exp(jnp.sum(lv[0] * lv[1])) - jnp.exp(jnp.sum(lv[2] * lv[3])) + lam_init
                scal = jnp.concatenate([lam.reshape(1), slopes])
                seq = lambda a: a.reshape(b, t, -1)
                if past_k is None:
                    oa = _attention(scal, seq(q), seq(k), seq(v), _row(diff_norm[j]), lam_init)
                else:
                    oa = _attention_step(scal, seq(q), seq(k), seq(v), past_k[j], past_v[j], _row(diff_norm[j]),
                                         lam_init)
                ob, s_new = _hgrn(seq(hg), _row(lower_bounds[j]), _row(hgrn_norm[j]), hgrn0[j])
                ks.append(k_out.reshape(b, t, a_heads, 2, A_QK_DIM))
                vs.append(v_out.reshape(b, t, a_heads, A_HEAD))
                hs.append(s_new)
                w_out = even_w_out[j].astype(_BF)
                mix = [(oa.reshape(b * t, a_width), w_out[:a_width]), (ob.reshape(b * t, b_width), w_out[a_width:])]
            else:
                w_in = odd_w_in[j]
                w_cat = jnp.concatenate([w_in[:, :4 * c_width],
                                         jnp.pad(w_in[:, 4 * c_width:], ((0, 0), (0, LANES - 2 * c_heads)))],
                                        axis=1).astype(_BF)
                q, k, v, gate, ab, c_new = _conv_proj(x, _row(mix_norm[layer]), w_cat, conv_w[j].astype(_F32),
                                                      conv0[j].astype(_F32), t)
                seq = lambda a: a.reshape(b, t, -1)
                o, s_new = _delta(seq(q), seq(k), seq(v), seq(gate), seq(ab), _pad_lanes(delta_A_log[j]),
                                  _pad_lanes(delta_dt_bias[j]), _row(delta_norm[j]), delta0[j])
                ds.append(s_new)
                cs.append(c_new)
                mix = [(o.reshape(b * t, c_width), odd_w_out[j].astype(_BF))]
            final = _row(final_norm) if layer == depth - 1 else None
            x = _ffn(x, mix, _row(ffn2_norm[layer]), *ffn2[layer], final_gain=final)
            mix = []
        return (x.reshape(b, t, d_model), jnp.stack(ks, axis=0), jnp.stack(vs, axis=0),
                jnp.stack(hs, axis=0), jnp.stack(ds, axis=0), jnp.stack(cs, axis=0))

    bp = x_prompt.shape[0]
    n_even, n_odd = state_hgrn.shape[0], state_delta.shape[0]
    zeros_h = jnp.zeros((n_even, bp) + state_hgrn.shape[2:], _F32)
    zeros_d = jnp.zeros((n_odd, bp) + state_delta.shape[2:], _F32)
    zeros_c = jnp.zeros((n_odd, bp) + cache_conv.shape[2:], x_prompt.dtype)
    y_p, k_p, v_p, h_p, d_p, c_p = run(x_prompt, None, None, zeros_h, zeros_d, zeros_c)
    y_s, k_s, v_s, h_s, d_s, c_s = run(x_sample, cache_attn_k, cache_attn_v, state_hgrn, state_delta, cache_conv)
    return (y_p, y_s, k_p, v_p, h_p, d_p, c_p, k_s, v_s, h_s, d_s, c_s)
```

```python
import functools
import math

import jax
import jax.numpy as jnp
import ml_dtypes
import numpy as np
from jax import lax
from jax.experimental import pallas as pl
from jax.experimental.pallas import tpu as pltpu

CHUNK = 64
A_QK_DIM = 64
A_HEAD = 2 * A_QK_DIM
B_DK = 128
C_DK = 128
CONV_W = 4
NORM_EPS = 1e-6
DIFF_NORM_EPS = 1e-5
L2_EPS = 1e-6
LOG2E = math.log2(math.e)
LOG2E_HI = float(np.float32(LOG2E).astype(ml_dtypes.bfloat16))
LOG2E_LO = float(np.float32(LOG2E - LOG2E_HI).astype(ml_dtypes.bfloat16))

LANES = 128
HGRN_BLOCK = 16
FF_CHUNK = 256
TOKEN_TILE = 512
ATTN_TILE = 512
DELTA_CHUNKS = 4
HGRN_CHUNKS = 8
VMEM_LIMIT = 56 * 1024 * 1024

_BF = jnp.bfloat16
_F32 = jnp.float32
_HI = lax.Precision.HIGHEST
_NT = (((1,), (1,)), ((), ()))
_TN = (((0,), (0,)), ((), ()))


def _params(*sem):
    return pltpu.CompilerParams(dimension_semantics=sem, vmem_limit_bytes=VMEM_LIMIT)


def _rms(x, gain, eps):
    return x * lax.rsqrt(jnp.mean(x * x, axis=-1, keepdims=True) + eps) * gain


def _silu(x):
    return x * jax.nn.sigmoid(x)


def _const_spec(shape):
    nd = len(shape)
    return pl.BlockSpec(shape, lambda *_: (0,) * nd)


def _mm(a, b):
    return jnp.dot(a.astype(_BF), b.astype(_BF), preferred_element_type=_F32)


def _ffn_kernel(*refs, n_mix, final):
    refs = list(refs)
    x_ref = refs.pop(0)
    mix = [(refs.pop(0), refs.pop(0)) for _ in range(n_mix)]
    g_ref, wi_ref, wo_ref = refs[:3]
    refs = refs[3:]
    fg_ref = refs.pop(0) if final else None
    o_ref, acc_ref = refs
    d_ff = wo_ref.shape[0]

    x = x_ref[...]
    for a_ref, w_ref in mix:
        x = x + jnp.dot(a_ref[...], w_ref[...], preferred_element_type=_F32)
    o_ref[...] = x
    h = _rms(x, g_ref[...], NORM_EPS).astype(_BF)
    for c in range(0, d_ff, FF_CHUNK):
        gate = jnp.dot(h, wi_ref[:, c:c + FF_CHUNK], preferred_element_type=_F32)
        up = jnp.dot(h, wi_ref[:, d_ff + c:d_ff + c + FF_CHUNK], preferred_element_type=_F32)
        act = (_silu(gate) * up).astype(_BF)
        part = jnp.dot(act, wo_ref[c:c + FF_CHUNK, :], preferred_element_type=_F32)
        if c == 0:
            acc_ref[...] = part
        else:
            acc_ref[...] += part
    y = o_ref[...] + 0.5 * acc_ref[...]
    if final:
        y = _rms(y, fg_ref[...], NORM_EPS)
    o_ref[...] = y


def _ffn(x, mix, gain, wi, wo, final_gain=None):
    n, d = x.shape
    tm = min(TOKEN_TILE, n)
    assert wo.shape[0] % FF_CHUNK == 0
    row = lambda i: (i, 0)
    args, specs = [x], [pl.BlockSpec((tm, d), row)]
    for a, w in mix:
        args += [a, w]
        specs += [pl.BlockSpec((tm, a.shape[1]), row), _const_spec(w.shape)]
    args += [gain, wi, wo]
    specs += [_const_spec(gain.shape), _const_spec(wi.shape), _const_spec(wo.shape)]
    if final_gain is not None:
        args.append(final_gain)
        specs.append(_const_spec(final_gain.shape))
    kern = functools.partial(_ffn_kernel, n_mix=len(mix), final=final_gain is not None)
    return pl.pallas_call(
        kern,
        grid=(n // tm,),
        in_specs=specs,
        out_specs=pl.BlockSpec((tm, d), row),
        out_shape=jax.ShapeDtypeStruct((n, d), _F32),
        scratch_shapes=[pltpu.VMEM((tm, d), _F32)],
        compiler_params=_params("parallel"),
        name="ffn",
    )(*args)


def _proj_kernel(x_ref, g_ref, w_ref, *out_refs, groups, views):
    h = _rms(x_ref[...], g_ref[...], NORM_EPS).astype(_BF)
    view_refs = dict(zip(views, out_refs[len(groups):]))
    for gi, (o_ref, (c0, c1, scale)) in enumerate(zip(out_refs, groups)):
        for s in range(c0, c1, 512):
            e = min(s + 512, c1)
            r = jnp.dot(h, w_ref[:, s:e], preferred_element_type=_F32)
            if scale != 1.0:
                r = r * scale
            o_ref[:, s - c0:e - c0] = r.astype(o_ref.dtype)
            if gi in views:
                dims = views[gi]
                width = dims[-1]
                for j in range((e - s) // width):
                    idx = np.unravel_index((s - c0) // width + j, dims[:-1])
                    view_refs[gi][(slice(None),) + tuple(int(i) for i in idx) + (slice(None),)] = (
                        r[:, j * width:(j + 1) * width])


def _proj(x, gain, w, groups, dtypes, views=None):
    n, d = x.shape
    tm = min(TOKEN_TILE, n)
    views = views or {}
    row = lambda i: (i, 0)
    lead = lambda nd: (lambda i: (i,) + (0,) * nd)
    return pl.pallas_call(
        functools.partial(_proj_kernel, groups=groups, views=views),
        grid=(n // tm,),
        in_specs=[pl.BlockSpec((tm, d), row), _const_spec(gain.shape), _const_spec(w.shape)],
        out_specs=[pl.BlockSpec((tm, c1 - c0), row) for c0, c1, _ in groups]
        + [pl.BlockSpec((tm,) + dims, lead(len(dims))) for dims in views.values()],
        out_shape=[jax.ShapeDtypeStruct((n, c1 - c0), dt) for (c0, c1, _), dt in zip(groups, dtypes)]
        + [jax.ShapeDtypeStruct((n,) + dims, _F32) for dims in views.values()],
        compiler_params=_params("parallel"),
        name="proj",
    )(x, gain, w)


def _lane_bcast(m, n):
    if n <= LANES:
        return m[:, :n]
    return jnp.concatenate([m] * (n // LANES), axis=1)


def _attn_kernel(scal_ref, q_ref, k_ref, v_ref, gain_ref, o_ref, m_ref, l_ref, acc_ref, ka_ref, va_ref, diag_ref,
                 sa_ref, sb_ref, *, tk, lam_init):
    head = pl.program_id(1)
    lam = scal_ref[0]
    slope = scal_ref[1 + head]
    n_q = q_ref.shape[1] // tk

    def stage(k_rows, v_rows, pos0, dst, n):
        pos = pos0 + lax.broadcasted_iota(jnp.int32, (n, LANES), 0)
        lane = lax.broadcasted_iota(jnp.int32, (n, LANES), 1)
        hi = (slope * CHUNK) * jnp.right_shift(pos, CHUNK.bit_length() - 1).astype(_F32)
        lo = slope * jnp.bitwise_and(pos, CHUNK - 1).astype(_F32)
        pair = jnp.right_shift(lane, 1)
        for mi in range(2):
            a0 = A_QK_DIM * (1 - mi) // 2
            keys = jnp.where(pair == a0, hi, jnp.where(pair == a0 + 1, lo, k_rows))
            ka_ref[mi, pl.ds(dst, n), :] = keys.astype(_BF)
        va_ref[pl.ds(dst, n), :] = jnp.concatenate([v_rows.astype(_BF), jnp.ones((n, LANES), _BF)], axis=1)

    r = lax.broadcasted_iota(jnp.int32, (tk, tk), 0)
    c = lax.broadcasted_iota(jnp.int32, (tk, tk), 1)
    ahead = (-2.0 * LOG2E * slope) * jnp.maximum(c - r, 0).astype(_F32)
    diag_ref[...] = jnp.where((c // CHUNK) <= (r // CHUNK), ahead, -jnp.inf)

    def q_tile(qi, carry):
        own = pl.multiple_of(qi * tk, tk)
        stage(k_ref[0, pl.ds(own, tk), :], v_ref[0, pl.ds(own, tk), :], own, own, tk)

        q = q_ref[0, pl.ds(own, tk), :]
        lane = lax.broadcasted_iota(jnp.int32, q.shape, 1)
        log2e = jnp.where(jnp.bitwise_and(lane, 1) == 0, LOG2E_HI, LOG2E_LO).astype(_BF)
        zero = jnp.zeros_like(q)
        quad = jnp.right_shift(lane, 2)
        q_maps = (jnp.where(lane < A_QK_DIM, q, jnp.where(quad == A_QK_DIM // 4, log2e, zero)),
                  jnp.where(lane >= A_QK_DIM, q, jnp.where(quad == 0, log2e, zero)))

        m_ref[...] = jnp.full(m_ref.shape, -jnp.inf, _F32)
        l_ref[...] = jnp.zeros(l_ref.shape, _F32)
        acc_ref[...] = jnp.zeros(acc_ref.shape, _F32)

        def scores(off, slot, extra):
            for mi in range(2):
                s = lax.dot_general(q_maps[mi], ka_ref[mi, pl.ds(off, tk), :], _NT, preferred_element_type=_F32)
                (sb_ref if slot else sa_ref)[mi] = s if extra is None else s + extra

        def consume(off, slot):
            v_aug = va_ref[pl.ds(off, tk), :]
            p, alpha = [], []
            for mi in range(2):
                s = (sb_ref if slot else sa_ref)[mi]
                m_prev = m_ref[mi]
                m_next = jnp.maximum(m_prev, jnp.max(s, axis=1, keepdims=True))
                p.append(jnp.exp2(s - _lane_bcast(m_next, tk)).astype(_BF))
                alpha.append(jnp.exp2(m_prev - m_next))
                m_ref[mi] = m_next
            for mi in range(2):
                pv = jnp.dot(p[mi], v_aug, preferred_element_type=_F32)
                acc_ref[mi] = alpha[mi] * acc_ref[mi] + pv[:, :LANES]
                l_ref[mi] = alpha[mi] * l_ref[mi] + pv[:, LANES:]

        A, B = 0, 1
        tile_off = lambda i: pl.multiple_of(jnp.where(i == 0, qi, i - 1) * tk, tk)
        scores(own, A, diag_ref[...])

        def pair(k, carry):
            scores(tile_off(2 * k + 1), B, None)
            consume(tile_off(2 * k), A)
            scores(tile_off(2 * k + 2), A, None)
            consume(tile_off(2 * k + 1), B)
            return carry

        lax.fori_loop(0, qi // 2, pair, 0)

        @pl.when(qi % 2 == 0)
        def _():
            consume(tile_off(qi), A)

        @pl.when(qi % 2 == 1)
        def _():
            scores(tile_off(qi), B, None)
            consume(tile_off(qi - 1), A)
            consume(tile_off(qi), B)

        o = acc_ref[0] / l_ref[0] - lam * (acc_ref[1] / l_ref[1])
        o = _rms(o, gain_ref[...], DIFF_NORM_EPS) * (1.0 - lam_init)
        o_ref[0, pl.ds(own, tk), :] = o.astype(o_ref.dtype)
        return carry

    lax.fori_loop(0, n_q, q_tile, 0)


def _attention(scal, q, k, v, gain, lam_init):
    b, t, width = q.shape
    heads = width // A_HEAD
    tk = min(ATTN_TILE, t)
    assert t % tk == 0 and tk % CHUNK == 0
    assert 8 % heads == 0, "ALiBi slopes 2^(-8 (h+1) / heads) must be powers of two for the bf16 bias lanes"
    whole = lambda bi, hi: (bi, 0, hi)
    kern = functools.partial(_attn_kernel, tk=tk, lam_init=lam_init)
    return pl.pallas_call(
        kern,
        grid=(b, heads),
        in_specs=[pl.BlockSpec(memory_space=pltpu.SMEM),
                  pl.BlockSpec((1, t, A_HEAD), whole),
                  pl.BlockSpec((1, t, A_HEAD), whole),
                  pl.BlockSpec((1, t, A_HEAD), whole),
                  _const_spec(gain.shape)],
        out_specs=pl.BlockSpec((1, t, A_HEAD), whole),
        out_shape=jax.ShapeDtypeStruct((b, t, width), _BF),
        scratch_shapes=[pltpu.VMEM((2, tk, LANES), _F32)] * 3
        + [pltpu.VMEM((2, t, LANES), _BF), pltpu.VMEM((t, 2 * LANES), _BF), pltpu.VMEM((tk, tk), _F32),
           pltpu.VMEM((2, tk, tk), _F32), pltpu.VMEM((2, tk, tk), _F32)],
        compiler_params=_params("parallel", "parallel"),
        name="attn",
    )(scal, q, k, v, gain)


def _attn_step_kernel(scal_ref, q_ref, k_ref, v_ref, pk_ref, pv_ref, gain_ref, o_ref, m_ref, l_ref, acc_ref,
                      *, heads, tk, past_len, lam_init):
    j = pl.program_id(1)
    lam = scal_ref[0]
    tq = q_ref.shape[1]
    maps = [(h, mi) for h in range(heads) for mi in range(2)]

    @pl.when(j == 0)
    def _():
        m_ref[...] = jnp.full(m_ref.shape, -jnp.inf, _F32)
        l_ref[...] = jnp.zeros(l_ref.shape, _F32)
        acc_ref[...] = jnp.zeros(acc_ref.shape, _F32)

    q = q_ref[0]
    q_maps = [q[:, (2 * h + mi) * A_QK_DIM:(2 * h + mi + 1) * A_QK_DIM] for h, mi in maps]

    def update(keys, values, bias):
        n = keys[0].shape[1]
        s = [jnp.dot(q_maps[i], keys[i].astype(_BF), preferred_element_type=_F32) + bias[maps[i][0]]
             for i in range(len(maps))]
        v_aug = [jnp.concatenate([v.astype(_BF), jnp.ones((n, LANES), _BF)], axis=1) for v in values]
        p, alpha = [], []
        for i in range(len(maps)):
            m_prev = m_ref[i]
            m_next = jnp.maximum(m_prev, jnp.max(s[i], axis=1, keepdims=True))
            p.append(jnp.exp2(s[i] - _lane_bcast(m_next, n)).astype(_BF))
            alpha.append(jnp.exp2(m_prev - m_next))
            m_ref[i] = m_next
        for i, (h, _) in enumerate(maps):
            pv = jnp.dot(p[i], v_aug[h], preferred_element_type=_F32)
            acc_ref[i] = alpha[i] * acc_ref[i] + pv[:, :LANES]
            l_ref[i] = alpha[i] * l_ref[i] + pv[:, LANES:]

    r = lax.broadcasted_iota(jnp.int32, (tq, tk), 0)
    c = lax.broadcasted_iota(jnp.int32, (tq, tk), 1)
    ahead = (r - c + (past_len - j * tk)).astype(_F32)
    update([pk_ref[0, h, mi] for h, mi in maps],
           [pv_ref[0, pl.ds(h, tk, stride=heads), :] for h in range(heads)],
           [(-LOG2E * scal_ref[1 + h]) * ahead for h in range(heads)])

    @pl.when(j == pl.num_programs(1) - 1)
    def _():
        r = lax.broadcasted_iota(jnp.int32, (tq, tq), 0)
        c = lax.broadcasted_iota(jnp.int32, (tq, tq), 1)
        dist = jnp.abs(r - c).astype(_F32)
        visible = ((past_len + c) // CHUNK) <= ((past_len + r) // CHUNK)
        k_cur, v_cur = k_ref[0], v_ref[0]
        update([k_cur[:, (2 * h + mi) * A_QK_DIM:(2 * h + mi + 1) * A_QK_DIM].T for h, mi in maps],
               [v_cur[:, h * A_HEAD:(h + 1) * A_HEAD] for h in range(heads)],
               [jnp.where(visible, (-LOG2E * scal_ref[1 + h]) * dist, -jnp.inf) for h in range(heads)])
        for h in range(heads):
            o = acc_ref[2 * h] / l_ref[2 * h] - lam * (acc_ref[2 * h + 1] / l_ref[2 * h + 1])
            o = _rms(o, gain_ref[...], DIFF_NORM_EPS) * (1.0 - lam_init)
            o_ref[0, :, h * A_HEAD:(h + 1) * A_HEAD] = o.astype(o_ref.dtype)


def _attention_step(scal, q, k, v, past_k, past_v, gain, lam_init):
    b, t, width = q.shape
    heads = width // A_HEAD
    past_len = past_k.shape[1]
    tk = min(ATTN_TILE, past_len)
    assert past_len % tk == 0 and t <= CHUNK
    pk = jnp.transpose(past_k, (0, 2, 3, 4, 1))
    pv = past_v.reshape(b, past_len * heads, A_HEAD)
    cur = pl.BlockSpec((1, t, width), lambda bi, j: (bi, 0, 0))
    kern = functools.partial(_attn_step_kernel, heads=heads, tk=tk, past_len=past_len, lam_init=lam_init)
    return pl.pallas_call(
        kern,
        grid=(b, past_len // tk),
        in_specs=[pl.BlockSpec(memory_space=pltpu.SMEM), cur, cur, cur,
                  pl.BlockSpec((1, heads, 2, A_QK_DIM, tk), lambda bi, j: (bi, 0, 0, 0, j)),
                  pl.BlockSpec((1, tk * heads, A_HEAD), lambda bi, j: (bi, j, 0)),
                  _const_spec(gain.shape)],
        out_specs=cur,
        out_shape=jax.ShapeDtypeStruct((b, t, width), _BF),
        scratch_shapes=[pltpu.VMEM((2 * heads, t, LANES), _F32)] * 3,
        compiler_params=_params("parallel", "arbitrary"),
        name="attn_step",
    )(scal, q, k, v, pk, pv, gain)


def _hgrn_kernel(hg_ref, lb_ref, gain_ref, s0_ref, ob_ref, sout_ref, st_ref, row_ref, w_ref, r_ref, *, L, heads,
                 nc):
    t = pl.program_id(1)
    width = heads * B_DK
    hs = range(heads)
    us = range(nc * heads)

    @pl.when(t == 0)
    def _():
        for h in hs:
            st_ref[h] = s0_ref[0, h].T

    rr = lax.broadcasted_iota(jnp.int32, (L, L), 0)
    cc = lax.broadcasted_iota(jnp.int32, (L, L), 1)
    tri = (rr >= cc).astype(_F32)
    ones = jnp.ones((LANES, LANES), _BF)
    blk, half = HGRN_BLOCK, HGRN_BLOCK // 2
    nb = L // blk
    blk_row = lax.broadcasted_iota(jnp.int32, (blk, LANES), 0)
    half_row = lax.broadcasted_iota(jnp.int32, (half, LANES), 0)
    rows = lambda x, a, b: x[a:b]

    def operand(g, u):
        c, h = divmod(u, heads)
        return hg_ref[0, c * L:(c + 1) * L, g * width + h * B_DK:g * width + (h + 1) * B_DK]

    qb = [operand(0, u) for u in us]
    ib = [operand(2, u) for u in us]
    f = []
    for u in us:
        lb = lb_ref[:, (u % heads) * B_DK:(u % heads + 1) * B_DK]
        f.append(lb + (1.0 - lb) * jax.nn.sigmoid(operand(1, u)))
    kk = [1.0 - f[u] for u in us]
    cb = [jnp.dot(tri, jnp.log2(f[u]), precision=_HI, preferred_element_type=_F32) for u in us]
    st = [st_ref[h] for h in hs]
    o_acc = [None] * len(us)
    for c in range(nc):
        for h in hs:
            u = c * heads + h
            o_acc[u] = lax.dot_general((qb[u] * jnp.exp2(cb[u])).astype(_BF), st[h].astype(_BF), _NT,
                                       preferred_element_type=_F32)
        for h in hs:
            u = c * heads + h
            cl = cb[u][L - 1:L, :]
            k_tail = kk[u] * jnp.exp2(cl - cb[u])
            st[h] = st[h] * jnp.exp2(cl) + lax.dot_general(ib[u].astype(_BF), k_tail.astype(_BF), _TN,
                                                           preferred_element_type=_F32)
    for h in hs:
        st_ref[h] = st[h]
    for h in us:
        row_ref[h, 0] = cb[h]
        row_ref[h, 1] = cb[h] - jnp.log2(jnp.maximum(kk[h], 0.0))
        row_ref[h, 2] = ib[h]

    if nb > 1:
        seg_start = [blk * i * (i - 1) // 2 for i in range(1, nb + 1)]
        n_keys = seg_start[-1]
        qr = lax.broadcasted_iota(jnp.int32, (L - blk, n_keys), 0)
        kc = lax.broadcasted_iota(jnp.int32, (L - blk, n_keys), 1)
        seg = jnp.zeros((L - blk, n_keys), jnp.int32)
        for i in range(2, nb):
            seg = jnp.where(kc >= seg_start[i - 1], i - 1, seg)
        own = seg == qr // blk
        scores = []
        for h in us:
            ref = [row_ref[h, 0, blk * i - 1:blk * i, :] for i in range(1, nb)]
            q_t = jnp.concatenate([rows(qb[h], blk * i, blk * (i + 1))
                                   * jnp.exp2(rows(cb[h], blk * i, blk * (i + 1)) - ref[i - 1])
                                   for i in range(1, nb)], axis=0)
            k_t = jnp.concatenate([rows(kk[h], 0, blk * i) * jnp.exp2(ref[i - 1] - rows(cb[h], 0, blk * i))
                                   for i in range(1, nb)], axis=0)
            scores.append(lax.dot_general(q_t.astype(_BF), k_t.astype(_BF), _NT, preferred_element_type=_F32))
        for h in us:
            v_t = jnp.concatenate([rows(ib[h], 0, blk * i) for i in range(1, nb)], axis=0)
            o_off = _mm(jnp.where(own, scores[h], 0.0), v_t)
            o_acc[h] = jnp.concatenate([rows(o_acc[h], 0, blk), rows(o_acc[h], blk, L) + o_off], axis=0)

    for h in us:
        pieces = []
        for i in range(nb):
            cb_i, q_i = rows(cb[h], blk * i, blk * (i + 1)), rows(qb[h], blk * i, blk * (i + 1))
            for sl in range(blk):
                s = blk * i + sl
                cbs = row_ref[h, 1, s:s + 1, :]
                if sl < half:
                    d = jnp.where(blk_row >= sl, cb_i - cbs, -jnp.inf)
                    pieces.append(q_i * jnp.exp2(d))
                else:
                    d = jnp.where(half_row >= sl - half, cb_i[half:] - cbs, -jnp.inf)
                    pieces.append(q_i[half:] * jnp.exp2(d))
        w_ref[h] = jnp.concatenate(pieces, axis=0).astype(_BF)
        r_ref[h] = jnp.dot(w_ref[h], ones, preferred_element_type=_F32)

    for h in us:
        o8 = [rows(o_acc[h], half * j, half * (j + 1)) for j in range(L // half)]
        off = 0
        for i in range(nb):
            for sl in range(blk):
                vs = row_ref[h, 2, blk * i + sl:blk * i + sl + 1, :]
                if sl < half:
                    o8[2 * i] = o8[2 * i] + r_ref[h, off:off + half, :] * vs
                    off += half
                o8[2 * i + 1] = o8[2 * i + 1] + r_ref[h, off:off + half, :] * vs
                off += half
        o = jnp.concatenate(o8, axis=0)
        ob = _rms(o, gain_ref[...], NORM_EPS) * _silu(operand(3, h))
        c, hd = divmod(h, heads)
        ob_ref[0, c * L:(c + 1) * L, hd * B_DK:(hd + 1) * B_DK] = ob.astype(ob_ref.dtype)

    @pl.when(t == pl.num_programs(1) - 1)
    def _():
        for h in hs:
            sout_ref[0, h] = st_ref[h].T


def _hgrn(hg, lb, gain, s0):
    b, t, four_w = hg.shape
    width = four_w // 4
    heads = width // B_DK
    L = min(CHUNK, t)
    half = HGRN_BLOCK // 2
    n_rows = (L // HGRN_BLOCK) * (half * HGRN_BLOCK + half * half)
    nc = HGRN_CHUNKS if t % (HGRN_CHUNKS * L) == 0 else 1
    kern = functools.partial(_hgrn_kernel, L=L, heads=heads, nc=nc)
    state_spec = pl.BlockSpec((1, heads, B_DK, B_DK), lambda bi, ti: (bi, 0, 0, 0))
    return pl.pallas_call(
        kern,
        grid=(b, t // (nc * L)),
        in_specs=[pl.BlockSpec((1, nc * L, four_w), lambda bi, ti: (bi, ti, 0)),
                  _const_spec(lb.shape), _const_spec(gain.shape), state_spec],
        out_specs=[pl.BlockSpec((1, nc * L, width), lambda bi, ti: (bi, ti, 0)), state_spec],
        out_shape=[jax.ShapeDtypeStruct((b, t, width), _BF),
                   jax.ShapeDtypeStruct((b, heads, B_DK, B_DK), _F32)],
        scratch_shapes=[pltpu.VMEM((heads, B_DK, B_DK), _F32),
                        pltpu.VMEM((nc * heads, 3, L, LANES), _F32),
                        pltpu.VMEM((nc * heads, n_rows, LANES), _BF),
                        pltpu.VMEM((nc * heads, n_rows, LANES), _F32)],
        compiler_params=_params("parallel", "arbitrary"),
        name="hgrn",
    )(hg, lb, gain, s0)


def _softplus(x):
    return jnp.maximum(x, 0.0) + jnp.log1p(jnp.exp(-jnp.abs(x)))


def _delta_kernel(q_ref, k_ref, v_ref, gate_ref, ab_ref, alog_ref, dtb_ref, gain_ref, s0_ref,
                  o_ref, sout_ref, s_ref, *, L, heads, nc):
    t = pl.program_id(1)
    hs = range(heads)
    pairs = [(c, h) for c in range(nc) for h in hs]

    @pl.when(t == 0)
    def _():
        for h in hs:
            s_ref[h] = s0_ref[0, h]

    rr = lax.broadcasted_iota(jnp.int32, (L, L), 0)
    cc = lax.broadcasted_iota(jnp.int32, (L, L), 1)
    lower = (rr >= cc).astype(_F32)
    upper = (rr <= cc).astype(_F32)

    ab = ab_ref[0]
    g_all = -jnp.exp(alog_ref[...]) * _softplus(ab + dtb_ref[...])
    beta_all = jax.nn.sigmoid(ab)
    chunk = lambda x, c: x[c * L:(c + 1) * L]
    gam_col = [jnp.dot(lower, chunk(g_all, c), precision=_HI, preferred_element_type=_F32) for c in range(nc)]
    gam_row = [lax.dot_general(chunk(g_all, c), upper, _TN, precision=_HI, preferred_element_type=_F32)
               for c in range(nc)]

    q, k, v, gc, bc, d_incl, d_strict = {}, {}, {}, {}, {}, {}, {}
    for c, h in pairs:
        p = (c, h)
        q[p], k[p], v[p] = (r[0, c * L:(c + 1) * L, h * C_DK:(h + 1) * C_DK] for r in (q_ref, k_ref, v_ref))
        gc[p] = gam_col[c][:, h:h + 1]
        bc[p] = chunk(beta_all, c)[:, heads + h:heads + h + 1]
        d_incl[p] = jnp.exp(jnp.where(rr >= cc, gc[p] - gam_row[c][h:h + 1, :], -jnp.inf))
        d_strict[p] = jnp.where(rr > cc, d_incl[p], 0.0)

    qkk = {p: lax.dot_general(jnp.concatenate([q[p], k[p]], axis=0).astype(_BF), k[p].astype(_BF), _NT,
                              preferred_element_type=_F32) for p in pairs}

    a = {p: bc[p] * qkk[p][L:] * d_strict[p] for p in pairs}
    n_inv = {p: -a[p] for p in pairs}
    pw = 2
    power = {p: _mm(a[p], a[p]) for p in pairs} if pw < L else None
    while pw < L:
        nxt = {p: _mm(power[p], power[p]) for p in pairs} if 2 * pw < L else None
        n_inv = {p: n_inv[p] + power[p] + _mm(n_inv[p], power[p]) for p in pairs}
        power = nxt
        pw *= 2

    eg = {p: jnp.exp(gc[p]) for p in pairs}
    rhs = {p: jnp.concatenate([k[p] * (bc[p] * eg[p]), v[p] * bc[p]], axis=1) for p in pairs}
    wu = {p: rhs[p] + _mm(n_inv[p], rhs[p]) for p in pairs}

    state = [s_ref[h] for h in hs]
    for c in range(nc):
        ps = [(c, h) for h in hs]
        ws = {p: _mm(jnp.concatenate([wu[p][:, :C_DK], q[p] * eg[p]], axis=0), state[p[1]]) for p in ps}
        v_new = {p: (wu[p][:, C_DK:] - ws[p][:L]).astype(_BF) for p in ps}
        o = {p: ws[p][L:] + _mm(qkk[p][:L] * d_incl[p], v_new[p]) for p in ps}
        for p in ps:
            h = p[1]
            gl = gc[p][L - 1:L, :]
            state[h] = jnp.exp(gl) * state[h] + lax.dot_general(
                (k[p] * jnp.exp(gl - gc[p])).astype(_BF), v_new[p], _TN, preferred_element_type=_F32)
            gate = gate_ref[0, c * L:(c + 1) * L, h * C_DK:(h + 1) * C_DK]
            on = _rms(o[p], gain_ref[...], NORM_EPS) * _silu(gate)
            o_ref[0, c * L:(c + 1) * L, h * C_DK:(h + 1) * C_DK] = on.astype(o_ref.dtype)
    for h in hs:
        s_ref[h] = state[h]

    @pl.when(t == pl.num_programs(1) - 1)
    def _():
        for h in hs:
            sout_ref[0, h] = s_ref[h]


def _delta(q, k, v, gate, ab, alog, dtb, gain, s0):
    b, t, width = q.shape
    heads = width // C_DK
    L = min(CHUNK, t)
    nc = DELTA_CHUNKS if t % (DELTA_CHUNKS * L) == 0 else 1
    rows = nc * L
    kern = functools.partial(_delta_kernel, L=L, heads=heads, nc=nc)
    step = lambda bi, ti: (bi, ti, 0)
    wide = pl.BlockSpec((1, rows, width), step)
    state_spec = pl.BlockSpec((1, heads, C_DK, C_DK), lambda bi, ti: (bi, 0, 0, 0))
    return pl.pallas_call(
        kern,
        grid=(b, t // rows),
        in_specs=[wide, wide, wide, wide, pl.BlockSpec((1, rows, LANES), step), _const_spec(alog.shape),
                  _const_spec(dtb.shape), _const_spec(gain.shape), state_spec],
        out_specs=[wide, state_spec],
        out_shape=[jax.ShapeDtypeStruct((b, t, width), _BF),
                   jax.ShapeDtypeStruct((b, heads, C_DK, C_DK), _F32)],
        scratch_shapes=[pltpu.VMEM((heads, C_DK, C_DK), _F32)],
        compiler_params=_params("parallel", "arbitrary"),
        name="delta",
    )(q, k, v, gate, ab, alog, dtb, gain, s0)


def _conv_proj_kernel(x_ref, halo_ref, g_ref, w_ref, cw_ref, hist_ref,
                      q_ref, k_ref, v_ref, gate_ref, ab_ref, cout_ref, pre_a, pre_b, *, width, tiles_per_seq):
    i = pl.program_id(0)
    tm = x_ref.shape[0]
    pad = halo_ref.shape[0]
    lo = pad - (CONV_W - 1)
    first = (i % tiles_per_seq) == 0
    h_all = _rms(jnp.concatenate([halo_ref[...], x_ref[...]], axis=0), g_ref[...], NORM_EPS).astype(_BF)
    step = 4 * C_DK
    n_chunk = 0
    for g, o_ref in enumerate((q_ref, k_ref, v_ref)):
        for s in range(0, width, step):
            c0 = g * width + s
            pre = pre_b if n_chunk % 2 else pre_a
            n_chunk += 1
            pre[...] = jnp.dot(h_all, w_ref[:, c0:c0 + step], preferred_element_type=_F32)
            pre[lo:pad, :] = jnp.where(first, hist_ref[0, :, c0:c0 + step], pre[lo:pad, :])
            cout_ref[0, :, c0:c0 + step] = pre[tm + lo:tm + pad, :]
            rows = pre[...]
            y = cw_ref[0:1, c0:c0 + step] * rows
            for j in range(1, CONV_W):
                y = cw_ref[j:j + 1, c0:c0 + step] * rows + pltpu.roll(y, 1, axis=0)
            y = _silu(y[pad:])
            for hh in range(step // C_DK):
                yh = y[:, hh * C_DK:(hh + 1) * C_DK]
                if g < 2:
                    yh = yh * lax.rsqrt(jnp.sum(yh * yh, axis=-1, keepdims=True) + L2_EPS)
                    if g == 0:
                        yh = yh * (C_DK ** -0.5)
                o_ref[:, s + hh * C_DK:s + (hh + 1) * C_DK] = yh
    h_main = h_all[pad:]
    for s in range(0, width, step):
        gate_ref[:, s:s + step] = jnp.dot(h_main, w_ref[:, 3 * width + s:3 * width + s + step],
                                          preferred_element_type=_F32)
    ab_ref[...] = jnp.dot(h_main, w_ref[:, 4 * width:4 * width + LANES], preferred_element_type=_F32)


def _conv_proj(x, gain, w, conv_w, hist, seq_len):
    n, d = x.shape
    width = conv_w.shape[1] // 3
    tm = min(TOKEN_TILE, seq_len)
    tiles_per_seq = seq_len // tm
    halo = 8
    row = lambda i: (i, 0)
    wide = pl.BlockSpec((tm, width), row)
    hist_spec = pl.BlockSpec((1, CONV_W - 1, 3 * width), lambda i: (i // tiles_per_seq, 0, 0))
    kern = functools.partial(_conv_proj_kernel, width=width, tiles_per_seq=tiles_per_seq)
    return pl.pallas_call(
        kern,
        grid=(n // tm,),
        in_specs=[pl.BlockSpec((tm, d), row),
                  pl.BlockSpec((halo, d), lambda i: (jnp.maximum(i * (tm // halo) - 1, 0), 0)),
                  _const_spec(gain.shape), _const_spec(w.shape), _const_spec(conv_w.shape), hist_spec],
        out_specs=[wide, wide, wide, wide, pl.BlockSpec((tm, LANES), row), hist_spec],
        out_shape=[jax.ShapeDtypeStruct((n, width), _F32)] * 4
        + [jax.ShapeDtypeStruct((n, LANES), _F32), jax.ShapeDtypeStruct(hist.shape, _F32)],
        scratch_shapes=[pltpu.VMEM((tm + halo, 4 * C_DK), _F32)] * 2,
        compiler_params=_params("arbitrary"),
        name="conv_proj",
    )(x, x, gain, w, conv_w, hist)


def _ffn_weights(w_in, w_out):
    return w_in.astype(_BF), w_out.astype(_BF)


def _row(v):
    return v.astype(_F32).reshape(1, -1)


def _pad_lanes(v):
    return jnp.pad(_row(v), ((0, 0), (0, LANES - v.shape[-1])))


def kernel(x_prompt, x_sample, cache_attn_k, cache_attn_v, state_hgrn, state_delta, cache_conv, ffn1_norm, ffn1_w_in, ffn1_w_out, mix_norm, ffn2_norm, ffn2_w_in, ffn2_w_out, even_w_in, diff_lambda, diff_norm, hgrn_lb_logits, hgrn_norm, even_w_out, odd_w_in, conv_w, delta_A_log, delta_dt_bias, delta_norm, odd_w_out, final_norm):
    depth, d_model = ffn1_norm.shape
    a_width = cache_attn_k.shape[3] * cache_attn_k.shape[4] * cache_attn_k.shape[5]
    a_heads = cache_attn_k.shape[3]
    b_heads, b_width = state_hgrn.shape[2], state_hgrn.shape[2] * state_hgrn.shape[3]
    c_heads, c_width = state_delta.shape[2], state_delta.shape[2] * state_delta.shape[3]

    lower_bounds = jnp.cumsum(jax.nn.softmax(hgrn_lb_logits.astype(_F32), axis=0), axis=0)
    slopes = 2.0 ** (-8.0 * jnp.arange(1, a_heads + 1, dtype=_F32) / a_heads)
    ffn1 = [_ffn_weights(ffn1_w_in[l], ffn1_w_out[l]) for l in range(depth)]
    ffn2 = [_ffn_weights(ffn2_w_in[l], ffn2_w_out[l]) for l in range(depth)]

    def run(x, past_k, past_v, hgrn0, delta0, conv0):
        b, t, _ = x.shape
        x = x.reshape(b * t, d_model)
        ks, vs, hs, ds, cs = [], [], [], [], []
        mix = []
        for layer in range(depth):
            j = layer // 2
            x = _ffn(x, mix, _row(ffn1_norm[layer]), *ffn1[layer])
            if layer % 2 == 0:
                groups = ((0, a_width, A_QK_DIM ** -0.5 * LOG2E), (a_width, 2 * a_width, 1.0),
                          (2 * a_width, 3 * a_width, 1.0), (3 * a_width, 3 * a_width + 4 * b_width, 1.0))
                q, k, v, hg, k_out, v_out = _proj(
                    x, _row(mix_norm[layer]), even_w_in[j].astype(_BF), groups, (_BF, _F32, _F32, _F32),
                    views={1: (a_heads, 2, A_QK_DIM), 2: (a_heads, A_HEAD)})
                lam_init = 0.8 - 0.6 * math.exp(-0.3 * layer)
                lv = diff_lambda[j].astype(_F32)
                lam = jnp.exp(jnp.sum(lv[0] * lv[1])) - jnp.exp(jnp.sum(lv[2] * lv[3])) + lam_init
                scal = jnp.concatenate([lam.reshape(1), slopes])
                seq = lambda a: a.reshape(b, t, -1)
                if past_k is None:
                    oa = _attention(scal, seq(q), seq(k), seq(v), _row(diff_norm[j]), lam_init)
                else:
                    oa = _attention_step(scal, seq(q), seq(k), seq(v), past_k[j], past_v[j], _row(diff_norm[j]),
                                         lam_init)
                ob, s_new = _hgrn(seq(hg), _row(lower_bounds[j]), _row(hgrn_norm[j]), hgrn0[j])
                ks.append(k_out.reshape(b, t, a_heads, 2, A_QK_DIM))
                vs.append(v_out.reshape(b, t, a_heads, A_HEAD))
                hs.append(s_new)
                w_out = even_w_out[j].astype(_BF)
                mix = [(oa.reshape(b * t, a_width), w_out[:a_width]), (ob.reshape(b * t, b_width), w_out[a_width:])]
            else:
                w_in = odd_w_in[j]
                w_cat = jnp.concatenate([w_in[:, :4 * c_width],
                                         jnp.pad(w_in[:, 4 * c_width:], ((0, 0), (0, LANES - 2 * c_heads)))],
                                        axis=1).astype(_BF)
                q, k, v, gate, ab, c_new = _conv_proj(x, _row(mix_norm[layer]), w_cat, conv_w[j].astype(_F32),
                                                      conv0[j].astype(_F32), t)
                seq = lambda a: a.reshape(b, t, -1)
                o, s_new = _delta(seq(q), seq(k), seq(v), seq(gate), seq(ab), _pad_lanes(delta_A_log[j]),
                                  _pad_lanes(delta_dt_bias[j]), _row(delta_norm[j]), delta0[j])
                ds.append(s_new)
                cs.append(c_new)
                mix = [(o.reshape(b * t, c_width), odd_w_out[j].astype(_BF))]
            final = _row(final_norm) if layer == depth - 1 else None
            x = _ffn(x, mix, _row(ffn2_norm[layer]), *ffn2[layer], final_gain=final)
            mix = []
        return (x.reshape(b, t, d_model), jnp.stack(ks, axis=0), jnp.stack(vs, axis=0),
                jnp.stack(hs, axis=0), jnp.stack(ds, axis=0), jnp.stack(cs, axis=0))

    bp = x_prompt.shape[0]
    n_even, n_odd = state_hgrn.shape[0], state_delta.shape[0]
    zeros_h = jnp.zeros((n_even, bp) + state_hgrn.shape[2:], _F32)
    zeros_d = jnp.zeros((n_odd, bp) + state_delta.shape[2:], _F32)
    zeros_c = jnp.zeros((n_odd, bp) + cache_conv.shape[2:], x_prompt.dtype)
    y_p, k_p, v_p, h_p, d_p, c_p = run(x_prompt, None, None, zeros_h, zeros_d, zeros_c)
    y_s, k_s, v_s, h_s, d_s, c_s = run(x_sample, cache_attn_k, cache_attn_v, state_hgrn, state_delta, cache_conv)
    return (y_p, y_s, k_p, v_p, h_p, d_p, c_p, k_s, v_s, h_s, d_s, c_s)
```

```python
import functools
import math

import jax
import jax.numpy as jnp
import ml_dtypes
import numpy as np
from jax import lax
from jax.experimental import pallas as pl
from jax.experimental.pallas import tpu as pltpu

CHUNK = 64
A_QK_DIM = 64
A_HEAD = 2 * A_QK_DIM
B_DK = 128
C_DK = 128
CONV_W = 4
NORM_EPS = 1e-6
DIFF_NORM_EPS = 1e-5
L2_EPS = 1e-6
LOG2E = math.log2(math.e)
LOG2E_HI = float(np.float32(LOG2E).astype(ml_dtypes.bfloat16))
LOG2E_LO = float(np.float32(LOG2E - LOG2E_HI).astype(ml_dtypes.bfloat16))

LANES = 128
HGRN_BLOCK = 16
FF_CHUNK = 256
TOKEN_TILE = 512
ATTN_TILE = 512
DELTA_CHUNKS = 4
HGRN_CHUNKS = 8
VMEM_LIMIT = 56 * 1024 * 1024

_BF = jnp.bfloat16
_F32 = jnp.float32
_HI = lax.Precision.HIGHEST
_NT = (((1,), (1,)), ((), ()))
_TN = (((0,), (0,)), ((), ()))


def _params(*sem):
    return pltpu.CompilerParams(dimension_semantics=sem, vmem_limit_bytes=VMEM_LIMIT)


def _rms(x, gain, eps):
    return x * lax.rsqrt(jnp.mean(x * x, axis=-1, keepdims=True) + eps) * gain


def _silu(x):
    return x * jax.nn.sigmoid(x)


def _const_spec(shape):
    nd = len(shape)
    return pl.BlockSpec(shape, lambda *_: (0,) * nd)


def _mm(a, b):
    return jnp.dot(a.astype(_BF), b.astype(_BF), preferred_element_type=_F32)


def _ffn_kernel(*refs, n_mix, final):
    refs = list(refs)
    x_ref = refs.pop(0)
    mix = [(refs.pop(0), refs.pop(0)) for _ in range(n_mix)]
    g_ref, wi_ref, wo_ref = refs[:3]
    refs = refs[3:]
    fg_ref = refs.pop(0) if final else None
    o_ref, acc_ref = refs
    d_ff = wo_ref.shape[0]

    x = x_ref[...]
    for a_ref, w_ref in mix:
        x = x + jnp.dot(a_ref[...], w_ref[...], preferred_element_type=_F32)
    o_ref[...] = x
    h = _rms(x, g_ref[...], NORM_EPS).astype(_BF)
    for c in range(0, d_ff, FF_CHUNK):
        gate = jnp.dot(h, wi_ref[:, c:c + FF_CHUNK], preferred_element_type=_F32)
        up = jnp.dot(h, wi_ref[:, d_ff + c:d_ff + c + FF_CHUNK], preferred_element_type=_F32)
        act = (_silu(gate) * up).astype(_BF)
        part = jnp.dot(act, wo_ref[c:c + FF_CHUNK, :], preferred_element_type=_F32)
        if c == 0:
            acc_ref[...] = part
        else:
            acc_ref[...] += part
    y = o_ref[...] + 0.5 * acc_ref[...]
    if final:
        y = _rms(y, fg_ref[...], NORM_EPS)
    o_ref[...] = y


def _ffn(x, mix, gain, wi, wo, final_gain=None):
    n, d = x.shape
    tm = min(TOKEN_TILE, n)
    assert wo.shape[0] % FF_CHUNK == 0
    row = lambda i: (i, 0)
    args, specs = [x], [pl.BlockSpec((tm, d), row)]
    for a, w in mix:
        args += [a, w]
        specs += [pl.BlockSpec((tm, a.shape[1]), row), _const_spec(w.shape)]
    args += [gain, wi, wo]
    specs += [_const_spec(gain.shape), _const_spec(wi.shape), _const_spec(wo.shape)]
    if final_gain is not None:
        args.append(final_gain)
        specs.append(_const_spec(final_gain.shape))
    kern = functools.partial(_ffn_kernel, n_mix=len(mix), final=final_gain is not None)
    return pl.pallas_call(
        kern,
        grid=(n // tm,),
        in_specs=specs,
        out_specs=pl.BlockSpec((tm, d), row),
        out_shape=jax.ShapeDtypeStruct((n, d), _F32),
        scratch_shapes=[pltpu.VMEM((tm, d), _F32)],
        compiler_params=_params("parallel"),
        name="ffn",
    )(*args)


def _proj_kernel(x_ref, g_ref, w_ref, *out_refs, groups, views):
    h = _rms(x_ref[...], g_ref[...], NORM_EPS).astype(_BF)
    view_refs = dict(zip(views, out_refs[len(groups):]))
    for gi, (o_ref, (c0, c1, scale)) in enumerate(zip(out_refs, groups)):
        for s in range(c0, c1, 512):
            e = min(s + 512, c1)
            r = jnp.dot(h, w_ref[:, s:e], preferred_element_type=_F32)
            if scale != 1.0:
                r = r * scale
            o_ref[:, s - c0:e - c0] = r.astype(o_ref.dtype)
            if gi in views:
                dims = views[gi]
                width = dims[-1]
                for j in range((e - s) // width):
                    idx = np.unravel_index((s - c0) // width + j, dims[:-1])
                    view_refs[gi][(slice(None),) + tuple(int(i) for i in idx) + (slice(None),)] = (
                        r[:, j * width:(j + 1) * width])


def _proj(x, gain, w, groups, dtypes, views=None):
    n, d = x.shape
    tm = min(TOKEN_TILE, n)
    views = views or {}
    row = lambda i: (i, 0)
    lead = lambda nd: (lambda i: (i,) + (0,) * nd)
    return pl.pallas_call(
        functools.partial(_proj_kernel, groups=groups, views=views),
        grid=(n // tm,),
        in_specs=[pl.BlockSpec((tm, d), row), _const_spec(gain.shape), _const_spec(w.shape)],
        out_specs=[pl.BlockSpec((tm, c1 - c0), row) for c0, c1, _ in groups]
        + [pl.BlockSpec((tm,) + dims, lead(len(dims))) for dims in views.values()],
        out_shape=[jax.ShapeDtypeStruct((n, c1 - c0), dt) for (c0, c1, _), dt in zip(groups, dtypes)]
        + [jax.ShapeDtypeStruct((n,) + dims, _F32) for dims in views.values()],
        compiler_params=_params("parallel"),
        name="proj",
    )(x, gain, w)


def _lane_bcast(m, n):
    if n <= LANES:
        return m[:, :n]
    return jnp.concatenate([m] * (n // LANES), axis=1)


def _attn_kernel(scal_ref, q_ref, k_ref, v_ref, gain_ref, o_ref, m_ref, l_ref, acc_ref, ka_ref, va_ref, diag_ref,
                 sa_ref, sb_ref, *, tk, lam_init):
    head = pl.program_id(1)
    lam = scal_ref[0]
    slope = scal_ref[1 + head]
    n_q = q_ref.shape[1] // tk

    def stage(k_rows, v_rows, pos0, dst, n):
        pos = pos0 + lax.broadcasted_iota(jnp.int32, (n, LANES), 0)
        lane = lax.broadcasted_iota(jnp.int32, (n, LANES), 1)
        hi = (slope * CHUNK) * jnp.right_shift(pos, CHUNK.bit_length() - 1).astype(_F32)
        lo = slope * jnp.bitwise_and(pos, CHUNK - 1).astype(_F32)
        pair = jnp.right_shift(lane, 1)
        for mi in range(2):
            a0 = A_QK_DIM * (1 - mi) // 2
            keys = jnp.where(pair == a0, hi, jnp.where(pair == a0 + 1, lo, k_rows))
            ka_ref[mi, pl.ds(dst, n), :] = keys.astype(_BF)
        va_ref[pl.ds(dst, n), :] = jnp.concatenate([v_rows.astype(_BF), jnp.ones((n, LANES), _BF)], axis=1)

    r = lax.broadcasted_iota(jnp.int32, (tk, tk), 0)
    c = lax.broadcasted_iota(jnp.int32, (tk, tk), 1)
    ahead = (-2.0 * LOG2E * slope) * jnp.maximum(c - r, 0).astype(_F32)
    diag_ref[...] = jnp.where((c // CHUNK) <= (r // CHUNK), ahead, -jnp.inf)

    def q_tile(qi, carry):
        own = pl.multiple_of(qi * tk, tk)
        stage(k_ref[0, pl.ds(own, tk), :], v_ref[0, pl.ds(own, tk), :], own, own, tk)

        q = q_ref[0, pl.ds(own, tk), :]
        lane = lax.broadcasted_iota(jnp.int32, q.shape, 1)
        log2e = jnp.where(jnp.bitwise_and(lane, 1) == 0, LOG2E_HI, LOG2E_LO).astype(_BF)
        zero = jnp.zeros_like(q)
        quad = jnp.right_shift(lane, 2)
        q_maps = (jnp.where(lane < A_QK_DIM, q, jnp.where(quad == A_QK_DIM // 4, log2e, zero)),
                  jnp.where(lane >= A_QK_DIM, q, jnp.where(quad == 0, log2e, zero)))

        m_ref[...] = jnp.full(m_ref.shape, -jnp.inf, _F32)
        l_ref[...] = jnp.zeros(l_ref.shape, _F32)
        acc_ref[...] = jnp.zeros(acc_ref.shape, _F32)

        def scores(off, slot, extra):
            for mi in range(2):
                s = lax.dot_general(q_maps[mi], ka_ref[mi, pl.ds(off, tk), :], _NT, preferred_element_type=_F32)
                (sb_ref if slot else sa_ref)[mi] = s if extra is None else s + extra

        def consume(off, slot):
            v_aug = va_ref[pl.ds(off, tk), :]
            p, alpha = [], []
            for mi in range(2):
                s = (sb_ref if slot else sa_ref)[mi]
                m_prev = m_ref[mi]
                m_next = jnp.maximum(m_prev, jnp.max(s, axis=1, keepdims=True))
                p.append(jnp.exp2(s - _lane_bcast(m_next, tk)).astype(_BF))
                alpha.append(jnp.exp2(m_prev - m_next))
                m_ref[mi] = m_next
            for mi in range(2):
                pv = jnp.dot(p[mi], v_aug, preferred_element_type=_F32)
                acc_ref[mi] = alpha[mi] * acc_ref[mi] + pv[:, :LANES]
                l_ref[mi] = alpha[mi] * l_ref[mi] + pv[:, LANES:]

        A, B = 0, 1
        tile_off = lambda i: pl.multiple_of(jnp.where(i == 0, qi, i - 1) * tk, tk)
        scores(own, A, diag_ref[...])

        def pair(k, carry):
            scores(tile_off(2 * k + 1), B, None)
            consume(tile_off(2 * k), A)
            scores(tile_off(2 * k + 2), A, None)
            consume(tile_off(2 * k + 1), B)
            return carry

        lax.fori_loop(0, qi // 2, pair, 0)

        @pl.when(qi % 2 == 0)
        def _():
            consume(tile_off(qi), A)

        @pl.when(qi % 2 == 1)
        def _():
            scores(tile_off(qi), B, None)
            consume(tile_off(qi - 1), A)
            consume(tile_off(qi), B)

        o = acc_ref[0] / l_ref[0] - lam * (acc_ref[1] / l_ref[1])
        o = _rms(o, gain_ref[...], DIFF_NORM_EPS) * (1.0 - lam_init)
        o_ref[0, pl.ds(own, tk), :] = o.astype(o_ref.dtype)
        return carry

    lax.fori_loop(0, n_q, q_tile, 0)


def _attention(scal, q, k, v, gain, lam_init):
    b, t, width = q.shape
    heads = width // A_HEAD
    tk = min(ATTN_TILE, t)
    assert t % tk == 0 and tk % CHUNK == 0
    assert 8 % heads == 0, "ALiBi slopes 2^(-8 (h+1) / heads) must be powers of two for the bf16 bias lanes"
    whole = lambda bi, hi: (bi, 0, hi)
    kern = functools.partial(_attn_kernel, tk=tk, lam_init=lam_init)
    return pl.pallas_call(
        kern,
        grid=(b, heads),
        in_specs=[pl.BlockSpec(memory_space=pltpu.SMEM),
                  pl.BlockSpec((1, t, A_HEAD), whole),
                  pl.BlockSpec((1, t, A_HEAD), whole),
                  pl.BlockSpec((1, t, A_HEAD), whole),
                  _const_spec(gain.shape)],
        out_specs=pl.BlockSpec((1, t, A_HEAD), whole),
        out_shape=jax.ShapeDtypeStruct((b, t, width), _BF),
        scratch_shapes=[pltpu.VMEM((2, tk, LANES), _F32)] * 3
        + [pltpu.VMEM((2, t, LANES), _BF), pltpu.VMEM((t, 2 * LANES), _BF), pltpu.VMEM((tk, tk), _F32),
           pltpu.VMEM((2, tk, tk), _F32), pltpu.VMEM((2, tk, tk), _F32)],
        compiler_params=_params("parallel", "parallel"),
        name="attn",
    )(scal, q, k, v, gain)


def _attn_step_kernel(scal_ref, q_ref, k_ref, v_ref, pk_ref, pv_ref, gain_ref, o_ref, m_ref, l_ref, acc_ref,
                      *, heads, tk, past_len, lam_init):
    j = pl.program_id(1)
    lam = scal_ref[0]
    tq = q_ref.shape[1]
    maps = [(h, mi) for h in range(heads) for mi in range(2)]

    @pl.when(j == 0)
    def _():
        m_ref[...] = jnp.full(m_ref.shape, -jnp.inf, _F32)
        l_ref[...] = jnp.zeros(l_ref.shape, _F32)
        acc_ref[...] = jnp.zeros(acc_ref.shape, _F32)

    q = q_ref[0]
    q_maps = [q[:, (2 * h + mi) * A_QK_DIM:(2 * h + mi + 1) * A_QK_DIM] for h, mi in maps]

    def update(keys, values, bias):
        n = keys[0].shape[1]
        s = [jnp.dot(q_maps[i], keys[i].astype(_BF), preferred_element_type=_F32) + bias[maps[i][0]]
             for i in range(len(maps))]
        v_aug = [jnp.concatenate([v.astype(_BF), jnp.ones((n, LANES), _BF)], axis=1) for v in values]
        p, alpha = [], []
        for i in range(len(maps)):
            m_prev = m_ref[i]
            m_next = jnp.maximum(m_prev, jnp.max(s[i], axis=1, keepdims=True))
            p.append(jnp.exp2(s[i] - _lane_bcast(m_next, n)).astype(_BF))
            alpha.append(jnp.exp2(m_prev - m_next))
            m_ref[i] = m_next
        for i, (h, _) in enumerate(maps):
            pv = jnp.dot(p[i], v_aug[h], preferred_element_type=_F32)
            acc_ref[i] = alpha[i] * acc_ref[i] + pv[:, :LANES]
            l_ref[i] = alpha[i] * l_ref[i] + pv[:, LANES:]

    r = lax.broadcasted_iota(jnp.int32, (tq, tk), 0)
    c = lax.broadcasted_iota(jnp.int32, (tq, tk), 1)
    ahead = (r - c + (past_len - j * tk)).astype(_F32)
    update([pk_ref[0, h, mi] for h, mi in maps],
           [pv_ref[0, pl.ds(h, tk, stride=heads), :] for h in range(heads)],
           [(-LOG2E * scal_ref[1 + h]) * ahead for h in range(heads)])

    @pl.when(j == pl.num_programs(1) - 1)
    def _():
        r = lax.broadcasted_iota(jnp.int32, (tq, tq), 0)
        c = lax.broadcasted_iota(jnp.int32, (tq, tq), 1)
        dist = jnp.abs(r - c).astype(_F32)
        visible = ((past_len + c) // CHUNK) <= ((past_len + r) // CHUNK)
        k_cur, v_cur = k_ref[0], v_ref[0]
        update([k_cur[:, (2 * h + mi) * A_QK_DIM:(2 * h + mi + 1) * A_QK_DIM].T for h, mi in maps],
               [v_cur[:, h * A_HEAD:(h + 1) * A_HEAD] for h in range(heads)],
               [jnp.where(visible, (-LOG2E * scal_ref[1 + h]) * dist, -jnp.inf) for h in range(heads)])
        for h in range(heads):
            o = acc_ref[2 * h] / l_ref[2 * h] - lam * (acc_ref[2 * h + 1] / l_ref[2 * h + 1])
            o = _rms(o, gain_ref[...], DIFF_NORM_EPS) * (1.0 - lam_init)
            o_ref[0, :, h * A_HEAD:(h + 1) * A_HEAD] = o.astype(o_ref.dtype)


def _attention_step(scal, q, k, v, past_k, past_v, gain, lam_init):
    b, t, width = q.shape
    heads = width // A_HEAD
    past_len = past_k.shape[1]
    tk = min(ATTN_TILE, past_len)
    assert past_len % tk == 0 and t <= CHUNK
    pk = jnp.transpose(past_k, (0, 2, 3, 4, 1))
    pv = past_v.reshape(b, past_len * heads, A_HEAD)
    cur = pl.BlockSpec((1, t, width), lambda bi, j: (bi, 0, 0))
    kern = functools.partial(_attn_step_kernel, heads=heads, tk=tk, past_len=past_len, lam_init=lam_init)
    return pl.pallas_call(
        kern,
        grid=(b, past_len // tk),
        in_specs=[pl.BlockSpec(memory_space=pltpu.SMEM), cur, cur, cur,
                  pl.BlockSpec((1, heads, 2, A_QK_DIM, tk), lambda bi, j: (bi, 0, 0, 0, j)),
                  pl.BlockSpec((1, tk * heads, A_HEAD), lambda bi, j: (bi, j, 0)),
                  _const_spec(gain.shape)],
        out_specs=cur,
        out_shape=jax.ShapeDtypeStruct((b, t, width), _BF),
        scratch_shapes=[pltpu.VMEM((2 * heads, t, LANES), _F32)] * 3,
        compiler_params=_params("parallel", "arbitrary"),
        name="attn_step",
    )(scal, q, k, v, pk, pv, gain)


def _hgrn_kernel(hg_ref, lb_ref, gain_ref, s0_ref, ob_ref, sout_ref, st_ref, row_ref, w_ref, r_ref, *, L, heads,
                 nc):
    t = pl.program_id(1)
    width = heads * B_DK
    hs = range(heads)
    us = range(nc * heads)

    @pl.when(t == 0)
    def _():
        for h in hs:
            st_ref[h] = s0_ref[0, h].T

    rr = lax.broadcasted_iota(jnp.int32, (L, L), 0)
    cc = lax.broadcasted_iota(jnp.int32, (L, L), 1)
    tri = (rr >= cc).astype(_F32)
    ones = jnp.ones((LANES, LANES), _BF)
    blk, half = HGRN_BLOCK, HGRN_BLOCK // 2
    nb = L // blk
    blk_row = lax.broadcasted_iota(jnp.int32, (blk, LANES), 0)
    half_row = lax.broadcasted_iota(jnp.int32, (half, LANES), 0)
    rows = lambda x, a, b: x[a:b]

    def operand(g, u):
        c, h = divmod(u, heads)
        return hg_ref[0, c * L:(c + 1) * L, g * width + h * B_DK:g * width + (h + 1) * B_DK]

    qb = [operand(0, u) for u in us]
    ib = [operand(2, u) for u in us]
    f = []
    for u in us:
        lb = lb_ref[:, (u % heads) * B_DK:(u % heads + 1) * B_DK]
        f.append(lb + (1.0 - lb) * jax.nn.sigmoid(operand(1, u)))
    kk = [1.0 - f[u] for u in us]
    cb = [jnp.dot(tri, jnp.log2(f[u]), precision=_HI, preferred_element_type=_F32) for u in us]
    st = [st_ref[h] for h in hs]
    o_acc = [None] * len(us)
    for c in range(nc):
        for h in hs:
            u = c * heads + h
            o_acc[u] = lax.dot_general((qb[u] * jnp.exp2(cb[u])).astype(_BF), st[h].astype(_BF), _NT,
                                       preferred_element_type=_F32)
        for h in hs:
            u = c * heads + h
            cl = cb[u][L - 1:L, :]
            k_tail = kk[u] * jnp.exp2(cl - cb[u])
            st[h] = st[h] * jnp.exp2(cl) + lax.dot_general(ib[u].astype(_BF), k_tail.astype(_BF), _TN,
                                                           preferred_element_type=_F32)
    for h in hs:
        st_ref[h] = st[h]
    for h in us:
        row_ref[h, 0] = cb[h]
        row_ref[h, 1] = cb[h] - jnp.log2(jnp.maximum(kk[h], 0.0))
        row_ref[h, 2] = ib[h]

    if nb > 1:
        seg_start = [blk * i * (i - 1) // 2 for i in range(1, nb + 1)]
        n_keys = seg_start[-1]
        qr = lax.broadcasted_iota(jnp.int32, (L - blk, n_keys), 0)
        kc = lax.broadcasted_iota(jnp.int32, (L - blk, n_keys), 1)
        seg = jnp.zeros((L - blk, n_keys), jnp.int32)
        for i in range(2, nb):
            seg = jnp.where(kc >= seg_start[i - 1], i - 1, seg)
        own = seg == qr // blk
        scores = []
        for h in us:
            ref = [row_ref[h, 0, blk * i - 1:blk * i, :] for i in range(1, nb)]
            q_t = jnp.concatenate([rows(qb[h], blk * i, blk * (i + 1))
                                   * jnp.exp2(rows(cb[h], blk * i, blk * (i + 1)) - ref[i - 1])
                                   for i in range(1, nb)], axis=0)
            k_t = jnp.concatenate([rows(kk[h], 0, blk * i) * jnp.exp2(ref[i - 1] - rows(cb[h], 0, blk * i))
                                   for i in range(1, nb)], axis=0)
            scores.append(lax.dot_general(q_t.astype(_BF), k_t.astype(_BF), _NT, preferred_element_type=_F32))
        for h in us:
            v_t = jnp.concatenate([rows(ib[h], 0, blk * i) for i in range(1, nb)], axis=0)
            o_off = _mm(jnp.where(own, scores[h], 0.0), v_t)
            o_acc[h] = jnp.concatenate([rows(o_acc[h], 0, blk), rows(o_acc[h], blk, L) + o_off], axis=0)

    for h in us:
        pieces = []
        for i in range(nb):
            cb_i, q_i = rows(cb[h], blk * i, blk * (i + 1)), rows(qb[h], blk * i, blk * (i + 1))
            for sl in range(blk):
                s = blk * i + sl
                cbs = row_ref[h, 1, s:s + 1, :]
                if sl < half:
                    d = jnp.where(blk_row >= sl, cb_i - cbs, -jnp.inf)
                    pieces.append(q_i * jnp.exp2(d))
                else:
                    d = jnp.where(half_row >= sl - half, cb_i[half:] - cbs, -jnp.inf)
                    pieces.append(q_i[half:] * jnp.exp2(d))
        w_ref[h] = jnp.concatenate(pieces, axis=0).astype(_BF)
        r_ref[h] = jnp.dot(w_ref[h], ones, preferred_element_type=_F32)

    for h in us:
        o8 = [rows(o_acc[h], half * j, half * (j + 1)) for j in range(L // half)]
        off = 0
        for i in range(nb):
            for sl in range(blk):
                vs = row_ref[h, 2, blk * i + sl:blk * i + sl + 1, :]
                if sl < half:
                    o8[2 * i] = o8[2 * i] + r_ref[h, off:off + half, :] * vs
                    off += half
                o8[2 * i + 1] = o8[2 * i + 1] + r_ref[h, off:off + half, :] * vs
                off += half
        o = jnp.concatenate(o8, axis=0)
        ob = _rms(o, gain_ref[...], NORM_EPS) * _silu(operand(3, h))
        c, hd = divmod(h, heads)
        ob_ref[0, c * L:(c + 1) * L, hd * B_DK:(hd + 1) * B_DK] = ob.astype(ob_ref.dtype)

    @pl.when(t == pl.num_programs(1) - 1)
    def _():
        for h in hs:
            sout_ref[0, h] = st_ref[h].T


def _hgrn(hg, lb, gain, s0):
    b, t, four_w = hg.shape
    width = four_w // 4
    heads = width // B_DK
    L = min(CHUNK, t)
    half = HGRN_BLOCK // 2
    n_rows = (L // HGRN_BLOCK) * (half * HGRN_BLOCK + half * half)
    nc = HGRN_CHUNKS if t % (HGRN_CHUNKS * L) == 0 else 1
    kern = functools.partial(_hgrn_kernel, L=L, heads=heads, nc=nc)
    state_spec = pl.BlockSpec((1, heads, B_DK, B_DK), lambda bi, ti: (bi, 0, 0, 0))
    return pl.pallas_call(
        kern,
        grid=(b, t // (nc * L)),
        in_specs=[pl.BlockSpec((1, nc * L, four_w), lambda bi, ti: (bi, ti, 0)),
                  _const_spec(lb.shape), _const_spec(gain.shape), state_spec],
        out_specs=[pl.BlockSpec((1, nc * L, width), lambda bi, ti: (bi, ti, 0)), state_spec],
        out_shape=[jax.ShapeDtypeStruct((b, t, width), _BF),
                   jax.ShapeDtypeStruct((b, heads, B_DK, B_DK), _F32)],
        scratch_shapes=[pltpu.VMEM((heads, B_DK, B_DK), _F32),
                        pltpu.VMEM((nc * heads, 3, L, LANES), _F32),
                        pltpu.VMEM((nc * heads, n_rows, LANES), _BF),
                        pltpu.VMEM((nc * heads, n_rows, LANES), _F32)],
        compiler_params=_params("parallel", "arbitrary"),
        name="hgrn",
    )(hg, lb, gain, s0)


def _softplus(x):
    return jnp.maximum(x, 0.0) + jnp.log1p(jnp.exp(-jnp.abs(x)))


def _delta_kernel(q_ref, k_ref, v_ref, gate_ref, ab_ref, alog_ref, dtb_ref, gain_ref, s0_ref,
                  o_ref, sout_ref, s_ref, *, L, heads, nc):
    t = pl.program_id(1)
    hs = range(heads)
    pairs = [(c, h) for c in range(nc) for h in hs]

    @pl.when(t == 0)
    def _():
        for h in hs:
            s_ref[h] = s0_ref[0, h]

    rr = lax.broadcasted_iota(jnp.int32, (L, L), 0)
    cc = lax.broadcasted_iota(jnp.int32, (L, L), 1)
    lower = (rr >= cc).astype(_F32)
    upper = (rr <= cc).astype(_F32)

    ab = ab_ref[0]
    g_all = -jnp.exp(alog_ref[...]) * _softplus(ab + dtb_ref[...])
    beta_all = jax.nn.sigmoid(ab)
    chunk = lambda x, c: x[c * L:(c + 1) * L]
    gam_col = [jnp.dot(lower, chunk(g_all, c), precision=_HI, preferred_element_type=_F32) for c in range(nc)]
    gam_row = [lax.dot_general(chunk(g_all, c), upper, _TN, precision=_HI, preferred_element_type=_F32)
               for c in range(nc)]

    q, k, v, gc, bc, d_incl, d_strict = {}, {}, {}, {}, {}, {}, {}
    for c, h in pairs:
        p = (c, h)
        q[p], k[p], v[p] = (r[0, c * L:(c + 1) * L, h * C_DK:(h + 1) * C_DK] for r in (q_ref, k_ref, v_ref))
        gc[p] = gam_col[c][:, h:h + 1]
        bc[p] = chunk(beta_all, c)[:, heads + h:heads + h + 1]
        d_incl[p] = jnp.exp(jnp.where(rr >= cc, gc[p] - gam_row[c][h:h + 1, :], -jnp.inf))
        d_strict[p] = jnp.where(rr > cc, d_incl[p], 0.0)

    qkk = {p: lax.dot_general(jnp.concatenate([q[p], k[p]], axis=0).astype(_BF), k[p].astype(_BF), _NT,
                              preferred_element_type=_F32) for p in pairs}

    a = {p: bc[p] * qkk[p][L:] * d_strict[p] for p in pairs}
    n_inv = {p: -a[p] for p in pairs}
    pw = 2
    power = {p: _mm(a[p], a[p]) for p in pairs} if pw < L else None
    while pw < L:
        if 2 * pw < L:
            both = {p: _mm(jnp.concatenate([n_inv[p], power[p]], axis=0), power[p]) for p in pairs}
            n_inv = {p: n_inv[p] + power[p] + both[p][:L] for p in pairs}
            power = {p: both[p][L:] for p in pairs}
        else:
            n_inv = {p: n_inv[p] + power[p] + _mm(n_inv[p], power[p]) for p in pairs}
        pw *= 2

    eg = {p: jnp.exp(gc[p]) for p in pairs}
    rhs = {p: jnp.concatenate([k[p] * (bc[p] * eg[p]), v[p] * bc[p]], axis=1) for p in pairs}
    wu = {p: rhs[p] + _mm(n_inv[p], rhs[p]) for p in pairs}

    state = [s_ref[h] for h in hs]
    for c in range(nc):
        ps = [(c, h) for h in hs]
        ws = {p: _mm(jnp.concatenate([wu[p][:, :C_DK], q[p] * eg[p]], axis=0), state[p[1]]) for p in ps}
        v_new = {p: (wu[p][:, C_DK:] - ws[p][:L]).astype(_BF) for p in ps}
        gl = {p: gc[p][L - 1:L, :] for p in ps}
        ov = {p: _mm(jnp.concatenate([qkk[p][:L] * d_incl[p], (k[p] * jnp.exp(gl[p] - gc[p])).T], axis=0), v_new[p])
              for p in ps}
        o = {p: ws[p][L:] + ov[p][:L] for p in ps}
        for p in ps:
            h = p[1]
            state[h] = jnp.exp(gl[p]) * state[h] + ov[p][L:]
            gate = gate_ref[0, c * L:(c + 1) * L, h * C_DK:(h + 1) * C_DK]
            on = _rms(o[p], gain_ref[...], NORM_EPS) * _silu(gate)
            o_ref[0, c * L:(c + 1) * L, h * C_DK:(h + 1) * C_DK] = on.astype(o_ref.dtype)
    for h in hs:
        s_ref[h] = state[h]

    @pl.when(t == pl.num_programs(1) - 1)
    def _():
        for h in hs:
            sout_ref[0, h] = s_ref[h]


def _delta(q, k, v, gate, ab, alog, dtb, gain, s0):
    b, t, width = q.shape
    heads = width // C_DK
    L = min(CHUNK, t)
    nc = DELTA_CHUNKS if t % (DELTA_CHUNKS * L) == 0 else 1
    rows = nc * L
    kern = functools.partial(_delta_kernel, L=L, heads=heads, nc=nc)
    step = lambda bi, ti: (bi, ti, 0)
    wide = pl.BlockSpec((1, rows, width), step)
    state_spec = pl.BlockSpec((1, heads, C_DK, C_DK), lambda bi, ti: (bi, 0, 0, 0))
    return pl.pallas_call(
        kern,
        grid=(b, t // rows),
        in_specs=[wide, wide, wide, wide, pl.BlockSpec((1, rows, LANES), step), _const_spec(alog.shape),
                  _const_spec(dtb.shape), _const_spec(gain.shape), state_spec],
        out_specs=[wide, state_spec],
        out_shape=[jax.ShapeDtypeStruct((b, t, width), _BF),
                   jax.ShapeDtypeStruct((b, heads, C_DK, C_DK), _F32)],
        scratch_shapes=[pltpu.VMEM((heads, C_DK, C_DK), _F32)],
        compiler_params=_params("parallel", "arbitrary"),
        name="delta",
    )(q, k, v, gate, ab, alog, dtb, gain, s0)


def _conv_proj_kernel(x_ref, halo_ref, g_ref, w_ref, cw_ref, hist_ref,
                      q_ref, k_ref, v_ref, gate_ref, ab_ref, cout_ref, pre_a, pre_b, *, width, tiles_per_seq):
    i = pl.program_id(0)
    tm = x_ref.shape[0]
    pad = halo_ref.shape[0]
    lo = pad - (CONV_W - 1)
    first = (i % tiles_per_seq) == 0
    h_all = _rms(jnp.concatenate([halo_ref[...], x_ref[...]], axis=0), g_ref[...], NORM_EPS).astype(_BF)
    step = 4 * C_DK
    n_chunk = 0
    for g, o_ref in enumerate((q_ref, k_ref, v_ref)):
        for s in range(0, width, step):
            c0 = g * width + s
            pre = pre_b if n_chunk % 2 else pre_a
            n_chunk += 1
            pre[...] = jnp.dot(h_all, w_ref[:, c0:c0 + step], preferred_element_type=_F32)
            pre[lo:pad, :] = jnp.where(first, hist_ref[0, :, c0:c0 + step], pre[lo:pad, :])
            cout_ref[0, :, c0:c0 + step] = pre[tm + lo:tm + pad, :]
            rows = pre[...]
            y = cw_ref[0:1, c0:c0 + step] * rows
            for j in range(1, CONV_W):
                y = cw_ref[j:j + 1, c0:c0 + step] * rows + pltpu.roll(y, 1, axis=0)
            y = _silu(y[pad:])
            for hh in range(step // C_DK):
                yh = y[:, hh * C_DK:(hh + 1) * C_DK]
                if g < 2:
                    yh = yh * lax.rsqrt(jnp.sum(yh * yh, axis=-1, keepdims=True) + L2_EPS)
                    if g == 0:
                        yh = yh * (C_DK ** -0.5)
                o_ref[:, s + hh * C_DK:s + (hh + 1) * C_DK] = yh
    h_main = h_all[pad:]
    for s in range(0, width, step):
        gate_ref[:, s:s + step] = jnp.dot(h_main, w_ref[:, 3 * width + s:3 * width + s + step],
                                          preferred_element_type=_F32)
    ab_ref[...] = jnp.dot(h_main, w_ref[:, 4 * width:4 * width + LANES], preferred_element_type=_F32)


def _conv_proj(x, gain, w, conv_w, hist, seq_len):
    n, d = x.shape
    width = conv_w.shape[1] // 3
    tm = min(TOKEN_TILE, seq_len)
    tiles_per_seq = seq_len // tm
    halo = 8
    row = lambda i: (i, 0)
    wide = pl.BlockSpec((tm, width), row)
    hist_spec = pl.BlockSpec((1, CONV_W - 1, 3 * width), lambda i: (i // tiles_per_seq, 0, 0))
    kern = functools.partial(_conv_proj_kernel, width=width, tiles_per_seq=tiles_per_seq)
    return pl.pallas_call(
        kern,
        grid=(n // tm,),
        in_specs=[pl.BlockSpec((tm, d), row),
                  pl.BlockSpec((halo, d), lambda i: (jnp.maximum(i * (tm // halo) - 1, 0), 0)),
                  _const_spec(gain.shape), _const_spec(w.shape), _const_spec(conv_w.shape), hist_spec],
        out_specs=[wide, wide, wide, wide, pl.BlockSpec((tm, LANES), row), hist_spec],
        out_shape=[jax.ShapeDtypeStruct((n, width), _F32)] * 4
        + [jax.ShapeDtypeStruct((n, LANES), _F32), jax.ShapeDtypeStruct(hist.shape, _F32)],
        scratch_shapes=[pltpu.VMEM((tm + halo, 4 * C_DK), _F32)] * 2,
        compiler_params=_params("arbitrary"),
        name="conv_proj",
    )(x, x, gain, w, conv_w, hist)


def _ffn_weights(w_in, w_out):
    return w_in.astype(_BF), w_out.astype(_BF)


def _row(v):
    return v.astype(_F32).reshape(1, -1)


def _pad_lanes(v):
    return jnp.pad(_row(v), ((0, 0), (0, LANES - v.shape[-1])))


def kernel(x_prompt, x_sample, cache_attn_k, cache_attn_v, state_hgrn, state_delta, cache_conv, ffn1_norm, ffn1_w_in, ffn1_w_out, mix_norm, ffn2_norm, ffn2_w_in, ffn2_w_out, even_w_in, diff_lambda, diff_norm, hgrn_lb_logits, hgrn_norm, even_w_out, odd_w_in, conv_w, delta_A_log, delta_dt_bias, delta_norm, odd_w_out, final_norm):
    depth, d_model = ffn1_norm.shape
    a_width = cache_attn_k.shape[3] * cache_attn_k.shape[4] * cache_attn_k.shape[5]
    a_heads = cache_attn_k.shape[3]
    b_heads, b_width = state_hgrn.shape[2], state_hgrn.shape[2] * state_hgrn.shape[3]
    c_heads, c_width = state_delta.shape[2], state_delta.shape[2] * state_delta.shape[3]

    lower_bounds = jnp.cumsum(jax.nn.softmax(hgrn_lb_logits.astype(_F32), axis=0), axis=0)
    slopes = 2.0 ** (-8.0 * jnp.arange(1, a_heads + 1, dtype=_F32) / a_heads)
    ffn1 = [_ffn_weights(ffn1_w_in[l], ffn1_w_out[l]) for l in range(depth)]
    ffn2 = [_ffn_weights(ffn2_w_in[l], ffn2_w_out[l]) for l in range(depth)]

    def run(x, past_k, past_v, hgrn0, delta0, conv0):
        b, t, _ = x.shape
        x = x.reshape(b * t, d_model)
        ks, vs, hs, ds, cs = [], [], [], [], []
        mix = []
        for layer in range(depth):
            j = layer // 2
            x = _ffn(x, mix, _row(ffn1_norm[layer]), *ffn1[layer])
            if layer % 2 == 0:
                groups = ((0, a_width, A_QK_DIM ** -0.5 * LOG2E), (a_width, 2 * a_width, 1.0),
                          (2 * a_width, 3 * a_width, 1.0), (3 * a_width, 3 * a_width + 4 * b_width, 1.0))
                q, k, v, hg, k_out, v_out = _proj(
                    x, _row(mix_norm[layer]), even_w_in[j].astype(_BF), groups, (_BF, _F32, _F32, _F32),
                    views={1: (a_heads, 2, A_QK_DIM), 2: (a_heads, A_HEAD)})
                lam_init = 0.8 - 0.6 * math.exp(-0.3 * layer)
                lv = diff_lambda[j].astype(_F32)
                lam = jnp.exp(jnp.sum(lv[0] * lv[1])) - jnp.exp(jnp.sum(lv[2] * lv[3])) + lam_init
                scal = jnp.concatenate([lam.reshape(1), slopes])
                seq = lambda a: a.reshape(b, t, -1)
                if past_k is None:
                    oa = _attention(scal, seq(q), seq(k), seq(v), _row(diff_norm[j]), lam_init)
                else:
                    oa = _attention_step(scal, seq(q), seq(k), seq(v), past_k[j], past_v[j], _row(diff_norm[j]),
                                         lam_init)
                ob, s_new = _hgrn(seq(hg), _row(lower_bounds[j]), _row(hgrn_norm[j]), hgrn0[j])
                ks.append(k_out.reshape(b, t, a_heads, 2, A_QK_DIM))
                vs.append(v_out.reshape(b, t, a_heads, A_HEAD))
                hs.append(s_new)
                w_out = even_w_out[j].astype(_BF)
                mix = [(oa.reshape(b * t, a_width), w_out[:a_width]), (ob.reshape(b * t, b_width), w_out[a_width:])]
            else:
                w_in = odd_w_in[j]
                w_cat = jnp.concatenate([w_in[:, :4 * c_width],
                                         jnp.pad(w_in[:, 4 * c_width:], ((0, 0), (0, LANES - 2 * c_heads)))],
                                        axis=1).astype(_BF)
                q, k, v, gate, ab, c_new = _conv_proj(x, _row(mix_norm[layer]), w_cat, conv_w[j].astype(_F32),
                                                      conv0[j].astype(_F32), t)
                seq = lambda a: a.reshape(b, t, -1)
                o, s_new = _delta(seq(q), seq(k), seq(v), seq(gate), seq(ab), _pad_lanes(delta_A_log[j]),
                                  _pad_lanes(delta_dt_bias[j]), _row(delta_norm[j]), delta0[j])
                ds.append(s_new)
                cs.append(c_new)
                mix = [(o.reshape(b * t, c_width), odd_w_out[j].astype(_BF))]
            final = _row(final_norm) if layer == depth - 1 else None
            x = _ffn(x, mix, _row(ffn2_norm[layer]), *ffn2[layer], final_gain=final)
            mix = []
        return (x.reshape(b, t, d_model), jnp.stack(ks, axis=0), jnp.stack(vs, axis=0),
                jnp.stack(hs, axis=0), jnp.stack(ds, axis=0), jnp.stack(cs, axis=0))

    bp = x_prompt.shape[0]
    n_even, n_odd = state_hgrn.shape[0], state_delta.shape[0]
    zeros_h = jnp.zeros((n_even, bp) + state_hgrn.shape[2:], _F32)
    zeros_d = jnp.zeros((n_odd, bp) + state_delta.shape[2:], _F32)
    zeros_c = jnp.zeros((n_odd, bp) + cache_conv.shape[2:], x_prompt.dtype)
    y_p, k_p, v_p, h_p, d_p, c_p = run(x_prompt, None, None, zeros_h, zeros_d, zeros_c)
    y_s, k_s, v_s, h_s, d_s, c_s = run(x_sample, cache_attn_k, cache_attn_v, state_hgrn, state_delta, cache_conv)
    return (y_p, y_s, k_p, v_p, h_p, d_p, c_p, k_s, v_s, h_s, d_s, c_s)
```
